```python
import jax, jax.numpy as jnp
from jax import lax
import numpy as np

D_MODEL = 1024
BATCH = 16
SEQ = 2048
DEPTH = 1
DEC_BATCH = 32
DEC_SEQ = 16
PAST_LEN = 1024

CHUNK = 64
D_MIX = D_MODEL
D_RET = D_MIX // 2
D_CONV = D_MIX - D_RET
N_RET_HEADS = 4
HEAD_DIM = D_RET // N_RET_HEADS
CONV_W = 3
N_EXPERTS = 32
TOP_K = 4
D_FF = D_MODEL
SWIGLU_LIMIT = 7.0
SWIGLU_ALPHA = 1.702
ROPE_BASE = 10000.0
LN_EPS = 1e-5
GN_EPS = 1e-5
MOE_BLOCK = 128
DEEPNORM_ALPHA = (2.0 * DEPTH) ** 0.25
DEEPNORM_BETA = (8.0 * DEPTH) ** -0.25
D_IN_PROJ = 4 * D_RET + 3 * D_CONV

kernel_name = "hybrid_retention_shortconv_moe_step"


def layer_norm(x, g, b):
    xf = x.astype(jnp.float32)
    mu = jnp.mean(xf, axis=-1, keepdims=True)
    var = jnp.mean(jnp.square(xf - mu), axis=-1, keepdims=True)
    y = (xf - mu) * lax.rsqrt(var + LN_EPS) * g.astype(jnp.float32) + b.astype(jnp.float32)
    return y.astype(x.dtype)


def rotary(x, pos):
    half = HEAD_DIM // 2
    inv_freq = ROPE_BASE ** (-jnp.arange(0, HEAD_DIM, 2, dtype=jnp.float32) / HEAD_DIM)
    ang = pos.astype(jnp.float32)[:, None] * inv_freq[None, :]
    cos = jnp.cos(ang)[None, :, None, :]
    sin = jnp.sin(ang)[None, :, None, :]
    x1, x2 = x[..., :half], x[..., half:]
    return jnp.concatenate([x1 * cos - x2 * sin, x1 * sin + x2 * cos], axis=-1)


def log_gammas():
    return jnp.log1p(-jnp.power(2.0, -5.0 - jnp.arange(N_RET_HEADS, dtype=jnp.float32)))


def retention_block(q, k, v, s):
    L = q.shape[1]
    lg = log_gammas()
    i = jnp.arange(L, dtype=jnp.float32)
    diff = i[:, None] - i[None, :]
    decay = jnp.where(diff[None] >= 0, jnp.exp(lg[:, None, None] * jnp.maximum(diff, 0.0)[None]), 0.0)
    scores = jnp.einsum('bihd,bjhd->bhij', q, k) * decay[None]
    intra = jnp.einsum('bhij,bjhe->bihe', scores, v)
    q_decay = jnp.exp(lg[None, :] * (i + 1.0)[:, None])
    cross = jnp.einsum('bihd,bhde->bihe', q, s) * q_decay[None, :, :, None]
    k_decay = jnp.exp(lg[None, :] * (L - 1.0 - i)[:, None])
    s_new = jnp.exp(lg * L)[None, :, None, None] * s + jnp.einsum('bjhd,bjhe->bhde', k * k_decay[None, :, :, None], v)
    return intra + cross, s_new


def retention_prompt(q, k, v, s0):
    B, S = q.shape[0], q.shape[1]
    nc = S // CHUNK
    to_chunks = lambda t: jnp.moveaxis(t.reshape(B, nc, CHUNK, N_RET_HEADS, HEAD_DIM), 1, 0)

    def step(s, qkv):
        o, s = retention_block(qkv[0], qkv[1], qkv[2], s)
        return s, o

    s_final, o = lax.scan(step, s0, (to_chunks(q), to_chunks(k), to_chunks(v)))
    o = jnp.moveaxis(o, 0, 1).reshape(B, S, N_RET_HEADS, HEAD_DIM)
    return o, s_final


def token_mixer(x, s_ret, s_conv, pos, chunked, w_in, beta_ret, beta_conv, conv_w, conv_b, w_out):
    B, L, _ = x.shape
    proj = jnp.einsum('bld,de->ble', x, w_in)
    q, k, v, g, bg, cg, hc = jnp.split(
        proj, [D_RET, 2 * D_RET, 3 * D_RET, 4 * D_RET, 4 * D_RET + D_CONV, 4 * D_RET + 2 * D_CONV], axis=-1)
    q = rotary(q.astype(jnp.float32).reshape(B, L, N_RET_HEADS, HEAD_DIM), pos)
    k = rotary(k.astype(jnp.float32).reshape(B, L, N_RET_HEADS, HEAD_DIM), pos) * (HEAD_DIM ** -0.5)
    v = v.astype(jnp.float32).reshape(B, L, N_RET_HEADS, HEAD_DIM)
    s0 = s_ret.astype(jnp.float32)
    if chunked:
        o, s_ret_new = retention_prompt(q, k, v, s0)
    else:
        o, s_ret_new = retention_block(q, k, v, s0)
    mu = jnp.mean(o, axis=-1, keepdims=True)
    var = jnp.mean(jnp.square(o - mu), axis=-1, keepdims=True)
    o = ((o - mu) * lax.rsqrt(var + GN_EPS)).reshape(B, L, D_RET)
    ret_out = (o * jax.nn.silu(g.astype(jnp.float32)) * beta_ret.astype(jnp.float32)).astype(x.dtype)
    u = cg * hc
    upad = jnp.concatenate([s_conv.astype(u.dtype), u], axis=1)
    z = conv_b
    for t in range(CONV_W):
        z = z + conv_w[t] * upad[:, t:t + L]
    conv_out = bg * z * beta_conv
    s_conv_new = upad[:, -(CONV_W - 1):]
    y = jnp.einsum('ble,ed->bld', jnp.concatenate([ret_out, conv_out], axis=-1), w_out)
    return y, s_ret_new.astype(s_ret.dtype), s_conv_new.astype(s_conv.dtype)


def moe(x2, w_router, b_router, w_gu, b_gu, w_down, b_down):
    T, D = x2.shape
    logits = jnp.einsum('td,de->te', x2.astype(jnp.float32), w_router.astype(jnp.float32)) + b_router.astype(jnp.float32)
    top_vals, top_idx = lax.top_k(logits, TOP_K)
    gates = jax.nn.softmax(top_vals, axis=-1)
    A = T * TOP_K
    flat_e = top_idx.reshape(A)
    flat_tok = jnp.repeat(jnp.arange(T, dtype=jnp.int32), TOP_K)
    flat_g = gates.reshape(A)
    order = jnp.argsort(flat_e)
    sorted_e = flat_e[order]
    counts = jnp.bincount(flat_e, length=N_EXPERTS)
    padded = ((counts + MOE_BLOCK - 1) // MOE_BLOCK) * MOE_BLOCK
    start = jnp.cumsum(counts) - counts
    pstart = jnp.cumsum(padded) - padded
    dest = pstart[sorted_e] + (jnp.arange(A, dtype=jnp.int32) - start[sorted_e])
    n_blocks = -(-A // MOE_BLOCK) + N_EXPERTS
    n_slots = n_blocks * MOE_BLOCK
    slot_tok = jnp.full((n_slots,), T, dtype=jnp.int32).at[dest].set(flat_tok[order])
    slot_gate = jnp.zeros((n_slots,), jnp.float32).at[dest].set(flat_g[order])
    block_start = jnp.arange(n_blocks, dtype=jnp.int32) * MOE_BLOCK
    block_e = jnp.minimum(jnp.searchsorted(jnp.cumsum(padded), block_start, side='right'), N_EXPERTS - 1)
    x_pad = jnp.concatenate([x2, jnp.zeros((1, D), x2.dtype)], axis=0)

    def expert_block(args):
        tok, e = args
        xb = x_pad[tok]
        gu = xb @ w_gu[e] + b_gu[e]
        gate, up = gu[:, :D_FF], gu[:, D_FF:]
        gate = jnp.minimum(gate, SWIGLU_LIMIT)
        up = jnp.clip(up, -SWIGLU_LIMIT, SWIGLU_LIMIT)
        h = (up + 1.0) * gate * jax.nn.sigmoid(SWIGLU_ALPHA * gate)
        return h @ w_down[e] + b_down[e]

    out = lax.map(expert_block, (slot_tok.reshape(n_blocks, MOE_BLOCK), block_e))
    out = out.reshape(n_slots, D) * slot_gate[:, None].astype(out.dtype)
    return jnp.zeros((T + 1, D), out.dtype).at[slot_tok].add(out)[:T]


def encoder_layer(x, s_ret, s_conv, pos, chunked, w_in, beta_ret, beta_conv, conv_w, conv_b, w_out,
                  ln1_g, ln1_b, w_router, b_router, w_gu, b_gu, w_down, b_down, ln2_g, ln2_b):
    B, L, D = x.shape
    m, s_ret_new, s_conv_new = token_mixer(x, s_ret, s_conv, pos, chunked, w_in, beta_ret, beta_conv,
                                           conv_w, conv_b, w_out)
    h = layer_norm(DEEPNORM_ALPHA * x + m, ln1_g, ln1_b)
    f = moe(h.reshape(B * L, D), w_router, b_router, w_gu, b_gu, w_down, b_down).reshape(B, L, D)
    y = layer_norm(DEEPNORM_ALPHA * h + f, ln2_g, ln2_b)
    return y, s_ret_new, s_conv_new


def setup_inputs(seed: int = 0) -> dict:
    key = jax.random.key(seed)
    ks = jax.random.split(key, 24)
    f32 = jnp.float32
    nrm = lambda k, shape, scale: jax.random.normal(k, shape, f32) * scale
    return {
        "x_prompt": nrm(ks[0], (BATCH, SEQ, D_MODEL), 1.0),
        "x_sample": nrm(ks[1], (DEC_BATCH, DEC_SEQ, D_MODEL), 1.0),
        "state_retention": nrm(ks[2], (DEPTH, DEC_BATCH, N_RET_HEADS, HEAD_DIM, HEAD_DIM), 0.3),
        "state_conv": nrm(ks[3], (DEPTH, DEC_BATCH, CONV_W - 1, D_CONV), 1.0),
        "w_in": nrm(ks[4], (DEPTH, D_MODEL, D_IN_PROJ), D_MODEL ** -0.5),
        "beta_ret": 1.0 + nrm(ks[5], (DEPTH, D_RET), 0.02),
        "beta_conv": 1.0 + nrm(ks[6], (DEPTH, D_CONV), 0.02),
        "conv_w": nrm(ks[7], (DEPTH, CONV_W, D_CONV), CONV_W ** -0.5),
        "conv_b": nrm(ks[8], (DEPTH, D_CONV), 0.02),
        "w_out": nrm(ks[9], (DEPTH, D_MIX, D_MODEL), D_MIX ** -0.5 * DEEPNORM_BETA),
        "ln1_g": 1.0 + nrm(ks[10], (DEPTH, D_MODEL), 0.02),
        "ln1_b": nrm(ks[11], (DEPTH, D_MODEL), 0.02),
        "w_router": nrm(ks[12], (DEPTH, D_MODEL, N_EXPERTS), D_MODEL ** -0.5),
        "b_router": nrm(ks[13], (DEPTH, N_EXPERTS), 0.01),
        "w_gu": nrm(ks[14], (DEPTH, N_EXPERTS, D_MODEL, 2 * D_FF), D_MODEL ** -0.5),
        "b_gu": nrm(ks[15], (DEPTH, N_EXPERTS, 2 * D_FF), 0.01),
        "w_down": nrm(ks[16], (DEPTH, N_EXPERTS, D_FF, D_MODEL), D_FF ** -0.5 * DEEPNORM_BETA),
        "b_down": nrm(ks[17], (DEPTH, N_EXPERTS, D_MODEL), 0.01),
        "ln2_g": 1.0 + nrm(ks[18], (DEPTH, D_MODEL), 0.02),
        "ln2_b": nrm(ks[19], (DEPTH, D_MODEL), 0.02),
    }


def reference(x_prompt, x_sample, state_retention, state_conv, w_in, beta_ret, beta_conv, conv_w, conv_b,
              w_out, ln1_g, ln1_b, w_router, b_router, w_gu, b_gu, w_down, b_down, ln2_g, ln2_b):
    B, S, _ = x_prompt.shape
    Bd, Ld, _ = x_sample.shape
    pos_prompt = jnp.arange(S, dtype=jnp.int32)
    pos_sample = PAST_LEN + jnp.arange(Ld, dtype=jnp.int32)
    hp, hs = x_prompt, x_sample
    ret_p, conv_p, ret_s, conv_s = [], [], [], []
    for l in range(DEPTH):
        w = (w_in[l], beta_ret[l], beta_conv[l], conv_w[l], conv_b[l], w_out[l], ln1_g[l], ln1_b[l],
             w_router[l], b_router[l], w_gu[l], b_gu[l], w_down[l], b_down[l], ln2_g[l], ln2_b[l])
        s_ret0 = jnp.zeros((B, N_RET_HEADS, HEAD_DIM, HEAD_DIM), x_prompt.dtype)
        s_conv0 = jnp.zeros((B, CONV_W - 1, D_CONV), x_prompt.dtype)
        hp, sr, sc = encoder_layer(hp, s_ret0, s_conv0, pos_prompt, True, *w)
        ret_p.append(sr)
        conv_p.append(sc)
        hs, sr, sc = encoder_layer(hs, state_retention[l], state_conv[l], pos_sample, False, *w)
        ret_s.append(sr)
        conv_s.append(sc)
    new_ret_prompt = jnp.stack(ret_p, axis=0)
    new_conv_prompt = jnp.stack(conv_p, axis=0)
    new_ret_sample = jnp.stack(ret_s, axis=0)
    new_conv_sample = jnp.stack(conv_s, axis=0)
    return (hp, hs, new_ret_prompt, new_conv_prompt, new_ret_sample, new_conv_sample)
```

```python
import functools

import jax
import jax.numpy as jnp
from jax import lax
from jax.experimental import pallas as pl
from jax.experimental.pallas import tpu as pltpu

D_MODEL = 1024
N_RET_HEADS = 4
HEAD_DIM = 128
D_RET = N_RET_HEADS * HEAD_DIM
D_CONV = D_MODEL - D_RET
CONV_W = 3
N_EXPERTS = 32
TOP_K = 4
D_FF = D_MODEL
SWIGLU_LIMIT = 7.0
SWIGLU_ALPHA = 1.702
ROPE_BASE = 10000.0
LN_EPS = 1e-5
GN_EPS = 1e-5
DEPTH = 1
PAST_LEN = 1024
DEEPNORM_ALPHA = (2.0 * DEPTH) ** 0.25

PROMPT_CHUNK = 256
SAMPLE_BATCH_BLOCK = 8
EXPERT_BLOCK = 256
COMBINE_TILE = 256
RANK_GROUP = 256
VMEM_LIMIT_BYTES = 56 * 1024 * 1024

F32 = jnp.float32
BF16 = jnp.bfloat16


def _layer_norm(x, g, b):
    mu = jnp.mean(x, axis=-1, keepdims=True)
    xc = x - mu
    var = jnp.mean(xc * xc, axis=-1, keepdims=True)
    return xc * lax.rsqrt(var + LN_EPS) * g + b


def _mixer_body(x_ref, cos_ref, sin_ref, decay_ref, qdec_ref, kdec_ref, sdec_ref, win_ref, wout_ref,
                convw_ref, convb_ref, bret_ref, bconv_ref, g1_ref, b1_ref, wr_ref, br_ref,
                sret_ref, sconv_ref,
                h_ref, logit_ref, sret_out, sconv_out,
                mix_ref, *, nb, C):
    c = pl.program_id(1)

    @pl.when(c == 0)
    def _():
        sret_out[...] = sret_ref[...]
        sconv_out[...] = sconv_ref[...]

    x = x_ref[...].reshape(nb * C, D_MODEL)
    xb = x.astype(BF16)

    def proj(col0, width):
        return jnp.dot(xb, win_ref[:, col0:col0 + width], preferred_element_type=F32)

    q = proj(0, D_RET)
    k = proj(D_RET, D_RET)
    v = proj(2 * D_RET, D_RET)
    g = proj(3 * D_RET, D_RET)
    bg = proj(4 * D_RET, D_CONV)
    cg = proj(4 * D_RET + D_CONV, D_CONV)
    hc = proj(4 * D_RET + 2 * D_CONV, D_CONV)

    cos = cos_ref[...]
    sin = sin_ref[...]
    row = lax.broadcasted_iota(jnp.int32, (C, D_CONV), 0)
    k_scale = HEAD_DIM ** -0.5

    for b in range(nb):
        r0 = b * C
        for h in range(N_RET_HEADS):
            c0 = h * HEAD_DIM
            qh = q[r0:r0 + C, c0:c0 + HEAD_DIM]
            kh = k[r0:r0 + C, c0:c0 + HEAD_DIM]
            vh = v[r0:r0 + C, c0:c0 + HEAD_DIM]
            qh = qh * cos + pltpu.roll(qh, HEAD_DIM // 2, axis=1) * sin
            kh = (kh * cos + pltpu.roll(kh, HEAD_DIM // 2, axis=1) * sin) * k_scale
            qb = qh.astype(BF16)
            kb = kh.astype(BF16)
            vb = vh.astype(BF16)
            s_old = sret_out[b, h]
            scores = lax.dot_general(qb, kb, (((1,), (1,)), ((), ())), preferred_element_type=F32)
            scores = scores * decay_ref[h]
            intra = jnp.dot(scores.astype(BF16), vb, preferred_element_type=F32)
            cross = jnp.dot(qb, s_old.astype(BF16), preferred_element_type=F32) * qdec_ref[h]
            o = intra + cross
            kd = (kh * kdec_ref[h]).astype(BF16)
            s_new = sdec_ref[h] * s_old + lax.dot_general(
                kd, vb, (((0,), (0,)), ((), ())), preferred_element_type=F32)
            sret_out[b, h] = s_new
            mu = jnp.mean(o, axis=-1, keepdims=True)
            oc = o - mu
            var = jnp.mean(oc * oc, axis=-1, keepdims=True)
            on = oc * lax.rsqrt(var + GN_EPS)
            gh = g[r0:r0 + C, c0:c0 + HEAD_DIM]
            ret = on * (gh * jax.nn.sigmoid(gh)) * bret_ref[:, c0:c0 + HEAD_DIM]
            mix_ref[r0:r0 + C, c0:c0 + HEAD_DIM] = ret.astype(BF16)

        u = cg[r0:r0 + C] * hc[r0:r0 + C]
        prev = sconv_out[b]
        u1 = jnp.where(row == 0, prev[1:2], pltpu.roll(u, 1, axis=0))
        u2 = jnp.where(row == 0, prev[0:1], jnp.where(row == 1, prev[1:2], pltpu.roll(u, 2, axis=0)))
        z = convb_ref[...] + convw_ref[0:1] * u2 + convw_ref[1:2] * u1 + convw_ref[2:3] * u
        conv_out = bg[r0:r0 + C] * z * bconv_ref[...]
        mix_ref[r0:r0 + C, D_RET:D_RET + D_CONV] = conv_out.astype(BF16)
        sconv_out[b] = u[C - 2:C]

    m = jnp.dot(mix_ref[...], wout_ref[...], preferred_element_type=F32)
    hval = _layer_norm(DEEPNORM_ALPHA * x + m, g1_ref[...], b1_ref[...])
    h_ref[...] = hval.reshape(nb, C, D_MODEL)
    logits = jnp.dot(hval, wr_ref[...], preferred_element_type=F32,
                     precision=lax.Precision.HIGHEST) + br_ref[...]
    logit_ref[...] = logits.reshape(nb, C, N_EXPERTS)


def _retention_tables(C, pos0, L):
    lg = jnp.log1p(-jnp.power(2.0, -5.0 - jnp.arange(N_RET_HEADS, dtype=F32)))
    i = jnp.arange(C, dtype=F32)
    diff = i[:, None] - i[None, :]
    decay = jnp.where(diff[None] >= 0, jnp.exp(lg[:, None, None] * jnp.maximum(diff, 0.0)[None]), 0.0)
    qdec = jnp.exp(lg[:, None] * (i + 1.0)[None, :])
    kdec = jnp.exp(lg[:, None] * (C - 1.0 - i)[None, :])
    sdec = jnp.exp(lg * C)
    qdec = jnp.broadcast_to(qdec[:, :, None], (N_RET_HEADS, C, HEAD_DIM))
    kdec = jnp.broadcast_to(kdec[:, :, None], (N_RET_HEADS, C, HEAD_DIM))
    sdec = jnp.broadcast_to(sdec[:, None, None], (N_RET_HEADS, 1, HEAD_DIM))
    pos = pos0 + jnp.arange(L, dtype=jnp.int32)
    inv_freq = ROPE_BASE ** (-jnp.arange(0, HEAD_DIM, 2, dtype=F32) / HEAD_DIM)
    ang = pos.astype(F32)[:, None] * inv_freq[None, :]
    cos = jnp.cos(ang)
    sin = jnp.sin(ang)
    cos_full = jnp.concatenate([cos, cos], axis=-1)
    sin_signed = jnp.concatenate([-sin, sin], axis=-1)
    return cos_full, sin_signed, decay, qdec, kdec, sdec


def _mixer(x, s_ret, s_conv, pos0, nb, C, w):
    B, L, _ = x.shape
    cos, sin, decay, qdec, kdec, sdec = _retention_tables(C, pos0, L)
    grid = (B // nb, L // C)
    const2 = lambda b, c: (0, 0)
    const3 = lambda b, c: (0, 0, 0)
    full = lambda a: pl.BlockSpec(a.shape, const2 if a.ndim == 2 else const3)
    in_arrays = [x, cos, sin, decay, qdec, kdec, sdec, w["w_in"], w["w_out"], w["conv_w"], w["conv_b"],
                 w["beta_ret"], w["beta_conv"], w["ln1_g"], w["ln1_b"], w["w_router"], w["b_router"],
                 s_ret, s_conv]
    in_specs = [pl.BlockSpec((nb, C, D_MODEL), lambda b, c: (b, c, 0)),
                pl.BlockSpec((C, HEAD_DIM), lambda b, c: (c, 0)),
                pl.BlockSpec((C, HEAD_DIM), lambda b, c: (c, 0))]
    in_specs += [full(a) for a in in_arrays[3:17]]
    in_specs += [pl.BlockSpec((nb, N_RET_HEADS, HEAD_DIM, HEAD_DIM), lambda b, c: (b, 0, 0, 0)),
                 pl.BlockSpec((nb, CONV_W - 1, D_CONV), lambda b, c: (b, 0, 0))]
    out_shape = [jax.ShapeDtypeStruct((B, L, D_MODEL), F32),
                 jax.ShapeDtypeStruct((B, L, N_EXPERTS), F32),
                 jax.ShapeDtypeStruct((B, N_RET_HEADS, HEAD_DIM, HEAD_DIM), F32),
                 jax.ShapeDtypeStruct((B, CONV_W - 1, D_CONV), F32)]
    out_specs = [pl.BlockSpec((nb, C, D_MODEL), lambda b, c: (b, c, 0)),
                 pl.BlockSpec((nb, C, N_EXPERTS), lambda b, c: (b, c, 0)),
                 pl.BlockSpec((nb, N_RET_HEADS, HEAD_DIM, HEAD_DIM), lambda b, c: (b, 0, 0, 0)),
                 pl.BlockSpec((nb, CONV_W - 1, D_CONV), lambda b, c: (b, 0, 0))]
    return pl.pallas_call(
        functools.partial(_mixer_body, nb=nb, C=C),
        grid=grid, in_specs=in_specs, out_specs=out_specs, out_shape=out_shape,
        scratch_shapes=[pltpu.VMEM((nb * C, D_MODEL), BF16)],
        compiler_params=pltpu.CompilerParams(
            dimension_semantics=("arbitrary", "arbitrary"), vmem_limit_bytes=VMEM_LIMIT_BYTES),
        name="mixer",
    )(*in_arrays)


def _routing_plan(logits, blk):
    T = logits.shape[0]
    A = T * TOP_K
    top_vals, top_idx = lax.top_k(logits, TOP_K)
    gates = jax.nn.softmax(top_vals, axis=-1)
    flat_e = top_idx.reshape(A)
    n_groups = A // RANK_GROUP
    onehot = (flat_e[:, None] == jnp.arange(N_EXPERTS, dtype=jnp.int32)[None, :]).astype(F32)
    oh = onehot.reshape(n_groups, RANK_GROUP, N_EXPERTS)
    ar = jnp.arange(RANK_GROUP)
    tril = (ar[:, None] > ar[None, :]).astype(F32)
    within = jnp.einsum("ij,gjk->gik", tril, oh)
    gcounts = oh.sum(axis=1)
    goff = jnp.cumsum(gcounts, axis=0) - gcounts
    rank = ((within + goff[:, None, :]) * oh).sum(axis=-1).reshape(A).astype(jnp.int32)
    counts = gcounts.sum(axis=0).astype(jnp.int32)
    padded = ((counts + blk - 1) // blk) * blk
    pend = jnp.cumsum(padded)
    pstart = pend - padded
    pos = (onehot * pstart.astype(F32)[None, :]).sum(axis=-1).astype(jnp.int32) + rank
    n_blocks = A // blk + N_EXPERTS
    slot_tok = jnp.zeros((n_blocks * blk,), jnp.int32).at[pos].set(
        jnp.arange(A, dtype=jnp.int32) // TOP_K, unique_indices=True)
    block_start = jnp.arange(n_blocks, dtype=jnp.int32) * blk
    block_e = jnp.minimum(jnp.searchsorted(pend, block_start, side="right"), N_EXPERTS - 1).astype(jnp.int32)
    n_valid = (pend[-1] // blk).astype(jnp.int32)
    block_e = jnp.where(jnp.arange(n_blocks) < n_valid, block_e, block_e[n_valid - 1])
    return gates, pos.reshape(T, TOP_K), slot_tok, block_e, n_valid.reshape(1)


def _expert_body(be_ref, nv_ref, tokc_ref, tokn_ref, h_hbm, wgu_ref, bgu_ref, wd_ref, bd_ref,
                 y_ref, xbuf, sem, wgu_bf, wd_bf, *, blk):
    i = pl.program_id(0)
    n_valid = nv_ref[0]
    slot = i % 2

    def issue(tok_ref, s):
        def body(r, carry):
            t = tok_ref[0, 0, r]
            pltpu.make_async_copy(h_hbm.at[pl.ds(t, 1)], xbuf.at[s, pl.ds(r, 1)], sem.at[s]).start()
            return carry
        lax.fori_loop(0, blk, body, 0)

    @pl.when(i == 0)
    def _():
        issue(tokc_ref, 0)

    @pl.when(i + 1 < n_valid)
    def _():
        issue(tokn_ref, 1 - slot)

    @pl.when(i < n_valid)
    def _():
        pltpu.make_async_copy(h_hbm.at[pl.ds(0, blk)], xbuf.at[slot], sem.at[slot]).wait()
        prev_e = be_ref[jnp.maximum(i - 1, 0)]

        @pl.when((i == 0) | (be_ref[i] != prev_e))
        def _():
            wgu_bf[...] = wgu_ref[0].astype(BF16)
            wd_bf[...] = wd_ref[0].astype(BF16)

        xb = xbuf[slot].astype(BF16)
        gu = jnp.dot(xb, wgu_bf[...], preferred_element_type=F32) + bgu_ref[0]
        gate = jnp.minimum(gu[:, :D_FF], SWIGLU_LIMIT)
        up = jnp.clip(gu[:, D_FF:], -SWIGLU_LIMIT, SWIGLU_LIMIT)
        act = (up + 1.0) * gate * jax.nn.sigmoid(SWIGLU_ALPHA * gate)
        y_ref[...] = jnp.dot(act.astype(BF16), wd_bf[...], preferred_element_type=F32) + bd_ref[0]

    @pl.when(i >= n_valid)
    def _():
        y_ref[...] = jnp.zeros_like(y_ref)


def _experts(h_all, slot_tok, block_e, n_valid, w_gu, b_gu, w_down, b_down, blk):
    n_blocks = block_e.shape[0]
    tok3 = slot_tok.reshape(n_blocks, 1, blk)
    grid_spec = pltpu.PrefetchScalarGridSpec(
        num_scalar_prefetch=2,
        grid=(n_blocks,),
        in_specs=[
            pl.BlockSpec((1, 1, blk), lambda i, be, nv: (i, 0, 0), memory_space=pltpu.SMEM),
            pl.BlockSpec((1, 1, blk), lambda i, be, nv: (jnp.minimum(i + 1, n_blocks - 1), 0, 0),
                         memory_space=pltpu.SMEM),
            pl.BlockSpec(memory_space=pl.ANY),
            pl.BlockSpec((1, D_MODEL, 2 * D_FF), lambda i, be, nv: (be[i], 0, 0)),
            pl.BlockSpec((1, 1, 2 * D_FF), lambda i, be, nv: (be[i], 0, 0)),
            pl.BlockSpec((1, D_FF, D_MODEL), lambda i, be, nv: (be[i], 0, 0)),
            pl.BlockSpec((1, 1, D_MODEL), lambda i, be, nv: (be[i], 0, 0)),
        ],
        out_specs=pl.BlockSpec((blk, D_MODEL), lambda i, be, nv: (i, 0)),
        scratch_shapes=[pltpu.VMEM((2, blk, D_MODEL), F32),
                        pltpu.SemaphoreType.DMA((2,)),
                        pltpu.VMEM((D_MODEL, 2 * D_FF), BF16),
                        pltpu.VMEM((D_FF, D_MODEL), BF16)],
    )
    return pl.pallas_call(
        functools.partial(_expert_body, blk=blk),
        grid_spec=grid_spec,
        out_shape=jax.ShapeDtypeStruct((n_blocks * blk, D_MODEL), F32),
        compiler_params=pltpu.CompilerParams(
            dimension_semantics=("arbitrary",), vmem_limit_bytes=VMEM_LIMIT_BYTES),
        name="experts",
    )(block_e, n_valid, tok3, tok3, h_all, w_gu, b_gu.reshape(N_EXPERTS, 1, 2 * D_FF),
      w_down, b_down.reshape(N_EXPERTS, 1, D_MODEL))


def _combine_body(posc_ref, posn_ref, ys_hbm, h_ref, gate_ref, g2_ref, b2_ref, y_ref, gbuf, sem, *, tt):
    i = pl.program_id(0)
    n = pl.num_programs(0)
    slot = i % 2
    rows = TOP_K * tt

    def issue(pos_ref, s):
        def body(r, carry):
            p = pos_ref[0, 0, r]
            pltpu.make_async_copy(ys_hbm.at[pl.ds(p, 1)], gbuf.at[s, pl.ds(r, 1)], sem.at[s]).start()
            return carry
        lax.fori_loop(0, rows, body, 0)

    @pl.when(i == 0)
    def _():
        issue(posc_ref, 0)

    @pl.when(i + 1 < n)
    def _():
        issue(posn_ref, 1 - slot)

    pltpu.make_async_copy(ys_hbm.at[pl.ds(0, rows)], gbuf.at[slot], sem.at[slot]).wait()
    gates = gate_ref[...]
    f = gates[:, 0:1] * gbuf[slot, 0:tt]
    for kk in range(1, TOP_K):
        f = f + gates[:, kk:kk + 1] * gbuf[slot, kk * tt:(kk + 1) * tt]
    y_ref[...] = _layer_norm(DEEPNORM_ALPHA * h_ref[...] + f, g2_ref[...], b2_ref[...])


def _combine(y_slots, pos, gates, h_all, ln2_g, ln2_b, tt):
    T = h_all.shape[0]
    n_tiles = T // tt
    pos3 = pos.reshape(n_tiles, tt, TOP_K).transpose(0, 2, 1).reshape(n_tiles, 1, TOP_K * tt)
    return pl.pallas_call(
        functools.partial(_combine_body, tt=tt),
        grid=(n_tiles,),
        in_specs=[
            pl.BlockSpec((1, 1, TOP_K * tt), lambda i: (i, 0, 0), memory_space=pltpu.SMEM),
            pl.BlockSpec((1, 1, TOP_K * tt), lambda i: (jnp.minimum(i + 1, n_tiles - 1), 0, 0),
                         memory_space=pltpu.SMEM),
            pl.BlockSpec(memory_space=pl.ANY),
            pl.BlockSpec((tt, D_MODEL), lambda i: (i, 0)),
            pl.BlockSpec((tt, TOP_K), lambda i: (i, 0)),
            pl.BlockSpec((1, D_MODEL), lambda i: (0, 0)),
            pl.BlockSpec((1, D_MODEL), lambda i: (0, 0)),
        ],
        out_specs=pl.BlockSpec((tt, D_MODEL), lambda i: (i, 0)),
        out_shape=jax.ShapeDtypeStruct((T, D_MODEL), F32),
        scratch_shapes=[pltpu.VMEM((2, TOP_K * tt, D_MODEL), F32),
                        pltpu.SemaphoreType.DMA((2,))],
        compiler_params=pltpu.CompilerParams(
            dimension_semantics=("arbitrary",), vmem_limit_bytes=VMEM_LIMIT_BYTES),
        name="combine",
    )(pos3, pos3, y_slots, h_all, gates, ln2_g, ln2_b)


def kernel(x_prompt, x_sample, state_retention, state_conv, w_in, beta_ret, beta_conv, conv_w, conv_b,
           w_out, ln1_g, ln1_b, w_router, b_router, w_gu, b_gu, w_down, b_down, ln2_g, ln2_b):
    assert w_in.shape[0] == DEPTH == 1
    B, S, _ = x_prompt.shape
    Bd, Ld, _ = x_sample.shape
    w = dict(w_in=w_in[0].astype(BF16), w_out=w_out[0].astype(BF16), conv_w=conv_w[0],
             conv_b=conv_b[0][None], beta_ret=beta_ret[0][None], beta_conv=beta_conv[0][None],
             ln1_g=ln1_g[0][None], ln1_b=ln1_b[0][None], w_router=w_router[0], b_router=b_router[0][None])

    s_ret0 = jnp.zeros((B, N_RET_HEADS, HEAD_DIM, HEAD_DIM), F32)
    s_conv0 = jnp.zeros((B, CONV_W - 1, D_CONV), F32)
    h_p, lg_p, sret_p, sconv_p = _mixer(x_prompt, s_ret0, s_conv0, 0, 1, PROMPT_CHUNK, w)
    h_s, lg_s, sret_s, sconv_s = _mixer(x_sample, state_retention[0], state_conv[0], PAST_LEN,
                                        SAMPLE_BATCH_BLOCK, Ld, w)

    Tp, Ts = B * S, Bd * Ld
    h_all = jnp.concatenate([h_p.reshape(Tp, D_MODEL), h_s.reshape(Ts, D_MODEL)], axis=0)
    logits = jnp.concatenate([lg_p.reshape(Tp, N_EXPERTS), lg_s.reshape(Ts, N_EXPERTS)], axis=0)

    gates, pos, slot_tok, block_e, n_valid = _routing_plan(logits, EXPERT_BLOCK)
    y_slots = _experts(h_all, slot_tok, block_e, n_valid, w_gu[0], b_gu[0], w_down[0], b_down[0],
                       EXPERT_BLOCK)
    y_all = _combine(y_slots, pos, gates, h_all, ln2_g[0][None], ln2_b[0][None], COMBINE_TILE)

    y_prompt = y_all[:Tp].reshape(B, S, D_MODEL)
    y_sample = y_all[Tp:].reshape(Bd, Ld, D_MODEL)
    return (y_prompt, y_sample, sret_p[None], sconv_p[None], sret_s[None], sconv_s[None])
```

```python
import functools

import jax
import jax.numpy as jnp
from jax import lax
from jax.experimental import pallas as pl
from jax.experimental.pallas import tpu as pltpu

D_MODEL = 1024
N_RET_HEADS = 4
HEAD_DIM = 128
D_RET = N_RET_HEADS * HEAD_DIM
D_CONV = D_MODEL - D_RET
CONV_W = 3
N_EXPERTS = 32
TOP_K = 4
D_FF = D_MODEL
SWIGLU_LIMIT = 7.0
SWIGLU_ALPHA = 1.702
ROPE_BASE = 10000.0
LN_EPS = 1e-5
GN_EPS = 1e-5
DEPTH = 1
PAST_LEN = 1024
DEEPNORM_ALPHA = (2.0 * DEPTH) ** 0.25

LANES = 128
SUBLANES = 8
ROW_CHUNKS = D_MODEL // LANES
assert ROW_CHUNKS == SUBLANES

PROMPT_CHUNK = 256
SAMPLE_BATCH_BLOCK = 8
EXPERT_BLOCK = 256
COMBINE_TILE = 256
DMA_UNROLL = 8
ORDER_BITS = 18
VMEM_LIMIT_BYTES = 56 * 1024 * 1024

F32 = jnp.float32
BF16 = jnp.bfloat16


def _layer_norm(x, g, b):
    mu = jnp.mean(x, axis=-1, keepdims=True)
    xc = x - mu
    var = jnp.mean(xc * xc, axis=-1, keepdims=True)
    return xc * lax.rsqrt(var + LN_EPS) * g + b


def _load_rows(ref, n_rows):
    return jnp.concatenate(
        [ref[pl.ds(j, n_rows, stride=SUBLANES), :] for j in range(ROW_CHUNKS)], axis=1)


def _store_rows(ref, val, n_rows):
    for j in range(ROW_CHUNKS):
        ref[pl.ds(j, n_rows, stride=SUBLANES), :] = val[:, j * LANES:(j + 1) * LANES]


def _mixer_body(*refs, nb, C, aliased):
    (x_ref, cos_ref, sin_ref, decay_ref, qdec_ref, kdec_ref, sdec_ref, win_ref, wout_ref,
     convw_ref, convb_ref, bret_ref, bconv_ref, g1_ref, b1_ref, wr_ref, br_ref,
     sret_ref, sconv_ref) = refs[:19]
    h_ref, logit_ref, sret_out, sconv_out, mix_ref = refs[19 + (2 if aliased else 0):]
    c = pl.program_id(1)

    @pl.when(c == 0)
    def _():
        sret_out[...] = sret_ref[...]
        sconv_out[...] = sconv_ref[...]

    x = x_ref[...].reshape(nb * C, D_MODEL)
    xb = x.astype(BF16)

    def proj(col0, width):
        return jnp.dot(xb, win_ref[:, col0:col0 + width], preferred_element_type=F32)

    q = proj(0, D_RET)
    k = proj(D_RET, D_RET)
    v = proj(2 * D_RET, D_RET)
    g = proj(3 * D_RET, D_RET)
    bg = proj(4 * D_RET, D_CONV)
    cg = proj(4 * D_RET + D_CONV, D_CONV)
    hc = proj(4 * D_RET + 2 * D_CONV, D_CONV)

    cos = cos_ref[...]
    sin = sin_ref[...]
    row = lax.broadcasted_iota(jnp.int32, (C, D_CONV), 0)
    k_scale = HEAD_DIM ** -0.5

    for b in range(nb):
        r0 = b * C
        for h in range(N_RET_HEADS):
            c0 = h * HEAD_DIM
            qh = q[r0:r0 + C, c0:c0 + HEAD_DIM]
            kh = k[r0:r0 + C, c0:c0 + HEAD_DIM]
            vh = v[r0:r0 + C, c0:c0 + HEAD_DIM]
            qh = qh * cos + pltpu.roll(qh, HEAD_DIM // 2, axis=1) * sin
            kh = (kh * cos + pltpu.roll(kh, HEAD_DIM // 2, axis=1) * sin) * k_scale
            qb = qh.astype(BF16)
            kb = kh.astype(BF16)
            vb = vh.astype(BF16)
            s_old = sret_out[b, h]
            scores = lax.dot_general(qb, kb, (((1,), (1,)), ((), ())), preferred_element_type=F32)
            scores = scores * decay_ref[h]
            intra = jnp.dot(scores.astype(BF16), vb, preferred_element_type=F32)
            cross = jnp.dot(qb, s_old.astype(BF16), preferred_element_type=F32) * qdec_ref[h]
            o = intra + cross
            kd = (kh * kdec_ref[h]).astype(BF16)
            s_new = sdec_ref[h] * s_old + lax.dot_general(
                kd, vb, (((0,), (0,)), ((), ())), preferred_element_type=F32)
            sret_out[b, h] = s_new
            mu = jnp.mean(o, axis=-1, keepdims=True)
            oc = o - mu
            var = jnp.mean(oc * oc, axis=-1, keepdims=True)
            on = oc * lax.rsqrt(var + GN_EPS)
            gh = g[r0:r0 + C, c0:c0 + HEAD_DIM]
            ret = on * (gh * jax.nn.sigmoid(gh)) * bret_ref[:, c0:c0 + HEAD_DIM]
            mix_ref[r0:r0 + C, c0:c0 + HEAD_DIM] = ret.astype(BF16)

        u = cg[r0:r0 + C] * hc[r0:r0 + C]
        prev = sconv_out[b]
        u1 = jnp.where(row == 0, prev[1:2], pltpu.roll(u, 1, axis=0))
        u2 = jnp.where(row == 0, prev[0:1], jnp.where(row == 1, prev[1:2], pltpu.roll(u, 2, axis=0)))
        z = convb_ref[...] + convw_ref[0:1] * u2 + convw_ref[1:2] * u1 + convw_ref[2:3] * u
        conv_out = bg[r0:r0 + C] * z * bconv_ref[...]
        mix_ref[r0:r0 + C, D_RET:D_RET + D_CONV] = conv_out.astype(BF16)
        sconv_out[b] = u[C - 2:C]

    m = jnp.dot(mix_ref[...], wout_ref[...], preferred_element_type=F32)
    hval = _layer_norm(DEEPNORM_ALPHA * x + m, g1_ref[...], b1_ref[...])
    _store_rows(h_ref, hval, nb * C)
    logit_ref[...] = jnp.dot(hval, wr_ref[...], preferred_element_type=F32,
                             precision=lax.Precision.HIGHEST) + br_ref[...]


def _retention_tables(C, pos0, L):
    lg = jnp.log1p(-jnp.power(2.0, -5.0 - jnp.arange(N_RET_HEADS, dtype=F32)))
    i = jnp.arange(C, dtype=F32)
    diff = i[:, None] - i[None, :]
    decay = jnp.where(diff[None] >= 0, jnp.exp(lg[:, None, None] * jnp.maximum(diff, 0.0)[None]), 0.0)
    qdec = jnp.exp(lg[:, None] * (i + 1.0)[None, :])
    kdec = jnp.exp(lg[:, None] * (C - 1.0 - i)[None, :])
    sdec = jnp.exp(lg * C)
    qdec = jnp.broadcast_to(qdec[:, :, None], (N_RET_HEADS, C, HEAD_DIM))
    kdec = jnp.broadcast_to(kdec[:, :, None], (N_RET_HEADS, C, HEAD_DIM))
    sdec = jnp.broadcast_to(sdec[:, None, None], (N_RET_HEADS, 1, HEAD_DIM))
    pos = pos0 + jnp.arange(L, dtype=jnp.int32)
    inv_freq = ROPE_BASE ** (-jnp.arange(0, HEAD_DIM, 2, dtype=F32) / HEAD_DIM)
    ang = pos.astype(F32)[:, None] * inv_freq[None, :]
    cos = jnp.cos(ang)
    sin = jnp.sin(ang)
    cos_full = jnp.concatenate([cos, cos], axis=-1)
    sin_signed = jnp.concatenate([-sin, sin], axis=-1)
    return cos_full, sin_signed, decay, qdec, kdec, sdec


def _mixer(x, s_ret, s_conv, pos0, nb, C, w, n_tokens_total, token0, shared=None):
    B, L, _ = x.shape
    nc = L // C
    rows = nb * C
    blk0 = token0 // rows
    assert token0 % rows == 0
    cos, sin, decay, qdec, kdec, sdec = _retention_tables(C, pos0, L)
    const2 = lambda b, c: (0, 0)
    const3 = lambda b, c: (0, 0, 0)
    full = lambda a: pl.BlockSpec(a.shape, const2 if a.ndim == 2 else const3)
    in_arrays = [x, cos, sin, decay, qdec, kdec, sdec, w["w_in"], w["w_out"], w["conv_w"], w["conv_b"],
                 w["beta_ret"], w["beta_conv"], w["ln1_g"], w["ln1_b"], w["w_router"], w["b_router"],
                 s_ret, s_conv]
    in_specs = [pl.BlockSpec((nb, C, D_MODEL), lambda b, c: (b, c, 0)),
                pl.BlockSpec((C, HEAD_DIM), lambda b, c: (c, 0)),
                pl.BlockSpec((C, HEAD_DIM), lambda b, c: (c, 0))]
    in_specs += [full(a) for a in in_arrays[3:17]]
    in_specs += [pl.BlockSpec((nb, N_RET_HEADS, HEAD_DIM, HEAD_DIM), lambda b, c: (b, 0, 0, 0)),
                 pl.BlockSpec((nb, CONV_W - 1, D_CONV), lambda b, c: (b, 0, 0))]
    aliases = {}
    if shared is not None:
        aliases = {len(in_arrays): 0, len(in_arrays) + 1: 1}
        in_arrays += list(shared)
        in_specs += [pl.BlockSpec(memory_space=pl.ANY), pl.BlockSpec(memory_space=pl.ANY)]
    out_shape = [jax.ShapeDtypeStruct((n_tokens_total * SUBLANES, LANES), F32),
                 jax.ShapeDtypeStruct((n_tokens_total, N_EXPERTS), F32),
                 jax.ShapeDtypeStruct((B, N_RET_HEADS, HEAD_DIM, HEAD_DIM), F32),
                 jax.ShapeDtypeStruct((B, CONV_W - 1, D_CONV), F32)]
    out_specs = [pl.BlockSpec((rows * SUBLANES, LANES), lambda b, c: (blk0 + b * nc + c, 0)),
                 pl.BlockSpec((rows, N_EXPERTS), lambda b, c: (blk0 + b * nc + c, 0)),
                 pl.BlockSpec((nb, N_RET_HEADS, HEAD_DIM, HEAD_DIM), lambda b, c: (b, 0, 0, 0)),
                 pl.BlockSpec((nb, CONV_W - 1, D_CONV), lambda b, c: (b, 0, 0))]
    return pl.pallas_call(
        functools.partial(_mixer_body, nb=nb, C=C, aliased=shared is not None),
        grid=(B // nb, nc), in_specs=in_specs, out_specs=out_specs, out_shape=out_shape,
        scratch_shapes=[pltpu.VMEM((rows, D_MODEL), BF16)],
        input_output_aliases=aliases,
        compiler_params=pltpu.CompilerParams(
            dimension_semantics=("arbitrary", "arbitrary"), vmem_limit_bytes=VMEM_LIMIT_BYTES),
        name="mixer",
    )(*in_arrays)


def _routing_plan(logits, blk):
    T = logits.shape[0]
    A = T * TOP_K
    assert A < (1 << ORDER_BITS) and A % blk == 0
    top_vals, top_idx = lax.top_k(logits, TOP_K)
    gates = jax.nn.softmax(top_vals, axis=-1)
    flat_e = top_idx.reshape(A).astype(jnp.int32)
    keys = (flat_e << ORDER_BITS) | jnp.arange(A, dtype=jnp.int32)
    order = jnp.sort(keys) & ((1 << ORDER_BITS) - 1)
    experts = jnp.arange(N_EXPERTS, dtype=jnp.int32)
    counts = jnp.sum((flat_e[:, None] == experts[None, :]).astype(jnp.int32), axis=0)
    nblk_e = (counts + blk - 1) // blk
    bend = jnp.cumsum(nblk_e)
    bstart = bend - nblk_e
    start = jnp.cumsum(counts) - counts
    n_blocks = A // blk + N_EXPERTS
    bi = jnp.arange(n_blocks, dtype=jnp.int32)
    n_valid = bend[-1]
    block_e = jnp.minimum(jnp.sum((bi[:, None] >= bend[None, :]).astype(jnp.int32), axis=1), N_EXPERTS - 1)
    j0 = (bi - bstart[block_e]) * blk
    cnt = jnp.clip(counts[block_e] - j0, 0, blk)
    off = jnp.clip(start[block_e] + j0, 0, A)
    order_pad = jnp.concatenate([order, jnp.zeros((blk,), jnp.int32)])
    win = jax.vmap(lambda o: lax.dynamic_slice(order_pad, (o,), (blk,)))(off)
    r = jnp.arange(blk, dtype=jnp.int32)[None, :]
    valid = r < cnt[:, None]
    tok = jnp.where(valid, win >> 2, 0)
    dst = jnp.where(valid, (win & (TOP_K - 1)) * T + (win >> 2), A + block_e[:, None] * blk + (r - cnt[:, None]))
    block_e = jnp.where(bi < n_valid, block_e, block_e[n_valid - 1])
    return gates, tok, dst, block_e, n_valid.reshape(1).astype(jnp.int32)


def _expert_body(be_ref, nv_ref, tokc_ref, tokn_ref, dst_ref, h_hbm, wgu_ref, bgu_ref, wd_ref, bd_ref,
                 y_hbm, xbuf, ybuf, isem, osem, wgu_bf, wd_bf, *, blk, n_blocks):
    i = pl.program_id(0)
    n_valid = nv_ref[0]
    slot = i % 2

    def row_tile(r):
        return pl.ds(pl.multiple_of(r * SUBLANES, SUBLANES), SUBLANES)

    def issue_gather(tok_ref, s):
        def body(gi, carry):
            for u in range(DMA_UNROLL):
                r = gi * DMA_UNROLL + u
                pltpu.make_async_copy(h_hbm.at[row_tile(tok_ref[0, 0, r])], xbuf.at[s, row_tile(r)],
                                      isem.at[s]).start()
            return carry
        lax.fori_loop(0, blk // DMA_UNROLL, body, 0)

    def issue_scatter(s):
        def body(gi, carry):
            for u in range(DMA_UNROLL):
                r = gi * DMA_UNROLL + u
                pltpu.make_async_copy(ybuf.at[s, row_tile(r)], y_hbm.at[row_tile(dst_ref[0, 0, r])],
                                      osem.at[s]).start()
            return carry
        lax.fori_loop(0, blk // DMA_UNROLL, body, 0)

    def wait_gather(s):
        pltpu.make_async_copy(h_hbm.at[pl.ds(0, blk * SUBLANES)], xbuf.at[s], isem.at[s]).wait()

    def wait_scatter(s):
        pltpu.make_async_copy(ybuf.at[s], y_hbm.at[pl.ds(0, blk * SUBLANES)], osem.at[s]).wait()

    @pl.when(i == 0)
    def _():
        issue_gather(tokc_ref, 0)

    @pl.when(i + 1 < n_valid)
    def _():
        issue_gather(tokn_ref, 1 - slot)

    @pl.when((i >= 2) & (i - 2 < n_valid))
    def _():
        wait_scatter(slot)

    @pl.when(i < n_valid)
    def _():
        wait_gather(slot)
        prev_e = be_ref[jnp.maximum(i - 1, 0)]

        @pl.when((i == 0) | (be_ref[i] != prev_e))
        def _():
            wgu_bf[...] = wgu_ref[0].astype(BF16)
            wd_bf[...] = wd_ref[0].astype(BF16)

        xb = _load_rows(xbuf.at[slot], blk).astype(BF16)
        gu = jnp.dot(xb, wgu_bf[...], preferred_element_type=F32) + bgu_ref[0]
        gate = jnp.minimum(gu[:, :D_FF], SWIGLU_LIMIT)
        up = jnp.clip(gu[:, D_FF:], -SWIGLU_LIMIT, SWIGLU_LIMIT)
        act = (up + 1.0) * gate * jax.nn.sigmoid(SWIGLU_ALPHA * gate)
        y = jnp.dot(act.astype(BF16), wd_bf[...], preferred_element_type=F32) + bd_ref[0]
        _store_rows(ybuf.at[slot], y, blk)
        issue_scatter(slot)

    @pl.when(i == n_blocks - 1)
    def _():
        @pl.when((i >= 1) & (i - 1 < n_valid))
        def _():
            wait_scatter(1 - slot)

        @pl.when(i < n_valid)
        def _():
            wait_scatter(slot)


def _experts(h_all, tok, dst, block_e, n_valid, w_gu, b_gu, w_down, b_down, blk, n_out_rows):
    n_blocks = block_e.shape[0]
    tok3 = tok.reshape(n_blocks, 1, blk)
    dst3 = dst.reshape(n_blocks, 1, blk)
    smem_blk = lambda imap: pl.BlockSpec((1, 1, blk), imap, memory_space=pltpu.SMEM)
    grid_spec = pltpu.PrefetchScalarGridSpec(
        num_scalar_prefetch=2,
        grid=(n_blocks,),
        in_specs=[
            smem_blk(lambda i, be, nv: (i, 0, 0)),
            smem_blk(lambda i, be, nv: (jnp.minimum(i + 1, n_blocks - 1), 0, 0)),
            smem_blk(lambda i, be, nv: (i, 0, 0)),
            pl.BlockSpec(memory_space=pl.ANY),
            pl.BlockSpec((1, D_MODEL, 2 * D_FF), lambda i, be, nv: (be[i], 0, 0)),
            pl.BlockSpec((1, 1, 2 * D_FF), lambda i, be, nv: (be[i], 0, 0)),
            pl.BlockSpec((1, D_FF, D_MODEL), lambda i, be, nv: (be[i], 0, 0)),
            pl.BlockSpec((1, 1, D_MODEL), lambda i, be, nv: (be[i], 0, 0)),
        ],
        out_specs=pl.BlockSpec(memory_space=pl.ANY),
        scratch_shapes=[pltpu.VMEM((2, blk * SUBLANES, LANES), F32),
                        pltpu.VMEM((2, blk * SUBLANES, LANES), F32),
                        pltpu.SemaphoreType.DMA((2,)),
                        pltpu.SemaphoreType.DMA((2,)),
                        pltpu.VMEM((D_MODEL, 2 * D_FF), BF16),
                        pltpu.VMEM((D_FF, D_MODEL), BF16)],
    )
    return pl.pallas_call(
        functools.partial(_expert_body, blk=blk, n_blocks=n_blocks),
        grid_spec=grid_spec,
        out_shape=jax.ShapeDtypeStruct((n_out_rows * SUBLANES, LANES), F32),
        compiler_params=pltpu.CompilerParams(
            dimension_semantics=("arbitrary",), vmem_limit_bytes=VMEM_LIMIT_BYTES),
        name="experts",
    )(block_e, n_valid, tok3, tok3, dst3, h_all, w_gu, b_gu.reshape(N_EXPERTS, 1, 2 * D_FF),
      w_down, b_down.reshape(N_EXPERTS, 1, D_MODEL))


def _combine_body(*refs, tt, n_first):
    y_refs = refs[:TOP_K]
    h_ref, gate_ref, g2_ref, b2_ref, out_first, out_second = refs[TOP_K:]
    i = pl.program_id(0)
    gates = gate_ref[...]
    f = gates[:, 0:1] * _load_rows(y_refs[0], tt)
    for kk in range(1, TOP_K):
        f = f + gates[:, kk:kk + 1] * _load_rows(y_refs[kk], tt)
    out = _layer_norm(DEEPNORM_ALPHA * _load_rows(h_ref, tt) + f, g2_ref[...], b2_ref[...])

    @pl.when(i < n_first)
    def _():
        out_first[...] = out

    @pl.when(i >= n_first)
    def _():
        out_second[...] = out


def _combine(y_rows, gates, h_all, ln2_g, ln2_b, tt, n_tokens_first):
    T = gates.shape[0]
    n_tiles = T // tt
    n_first = n_tokens_first // tt
    assert T % tt == 0 and n_tokens_first % tt == 0 and 0 < n_first < n_tiles
    row_blk = lambda imap: pl.BlockSpec((tt * SUBLANES, LANES), imap)
    y_specs = [row_blk(functools.partial(lambda i, kk: (kk * n_tiles + i, 0), kk=kk)) for kk in range(TOP_K)]
    return pl.pallas_call(
        functools.partial(_combine_body, tt=tt, n_first=n_first),
        grid=(n_tiles,),
        in_specs=y_specs + [
            row_blk(lambda i: (i, 0)),
            pl.BlockSpec((tt, TOP_K), lambda i: (i, 0)),
            pl.BlockSpec((1, D_MODEL), lambda i: (0, 0)),
            pl.BlockSpec((1, D_MODEL), lambda i: (0, 0)),
        ],
        out_specs=[pl.BlockSpec((tt, D_MODEL), lambda i: (jnp.minimum(i, n_first - 1), 0)),
                   pl.BlockSpec((tt, D_MODEL), lambda i: (jnp.maximum(i - n_first, 0), 0))],
        out_shape=[jax.ShapeDtypeStruct((n_tokens_first, D_MODEL), F32),
                   jax.ShapeDtypeStruct((T - n_tokens_first, D_MODEL), F32)],
        compiler_params=pltpu.CompilerParams(
            dimension_semantics=("arbitrary",), vmem_limit_bytes=VMEM_LIMIT_BYTES),
        name="combine",
    )(*([y_rows] * TOP_K), h_all, gates, ln2_g, ln2_b)


def kernel(x_prompt, x_sample, state_retention, state_conv, w_in, beta_ret, beta_conv, conv_w, conv_b,
           w_out, ln1_g, ln1_b, w_router, b_router, w_gu, b_gu, w_down, b_down, ln2_g, ln2_b):
    assert w_in.shape[0] == DEPTH == 1
    B, S, _ = x_prompt.shape
    Bd, Ld, _ = x_sample.shape
    Tp, Ts = B * S, Bd * Ld
    T = Tp + Ts
    w = dict(w_in=w_in[0].astype(BF16), w_out=w_out[0].astype(BF16), conv_w=conv_w[0],
             conv_b=conv_b[0][None], beta_ret=beta_ret[0][None], beta_conv=beta_conv[0][None],
             ln1_g=ln1_g[0][None], ln1_b=ln1_b[0][None], w_router=w_router[0], b_router=b_router[0][None])

    s_ret0 = jnp.zeros((B, N_RET_HEADS, HEAD_DIM, HEAD_DIM), F32)
    s_conv0 = jnp.zeros((B, CONV_W - 1, D_CONV), F32)
    h_all, logits, sret_p, sconv_p = _mixer(x_prompt, s_ret0, s_conv0, 0, 1, PROMPT_CHUNK, w, T, 0)
    h_all, logits, sret_s, sconv_s = _mixer(x_sample, state_retention[0], state_conv[0], PAST_LEN,
                                            SAMPLE_BATCH_BLOCK, Ld, w, T, Tp, shared=(h_all, logits))

    gates, tok, dst, block_e, n_valid = _routing_plan(logits, EXPERT_BLOCK)
    n_out_rows = T * TOP_K + N_EXPERTS * EXPERT_BLOCK
    y_rows = _experts(h_all, tok, dst, block_e, n_valid, w_gu[0], b_gu[0], w_down[0], b_down[0],
                      EXPERT_BLOCK, n_out_rows)
    y_p, y_s = _combine(y_rows, gates, h_all, ln2_g[0][None], ln2_b[0][None], COMBINE_TILE, Tp)

    return (y_p.reshape(B, S, D_MODEL), y_s.reshape(Bd, Ld, D_MODEL),
            sret_p[None], sconv_p[None], sret_s[None], sconv_s[None])
```

```python
import functools

import jax
import jax.numpy as jnp
from jax import lax
from jax.experimental import pallas as pl
from jax.experimental.pallas import tpu as pltpu

D_MODEL = 1024
N_RET_HEADS = 4
HEAD_DIM = 128
D_RET = N_RET_HEADS * HEAD_DIM
D_CONV = D_MODEL - D_RET
CONV_W = 3
N_EXPERTS = 32
TOP_K = 4
D_FF = D_MODEL
SWIGLU_LIMIT = 7.0
SWIGLU_ALPHA = 1.702
ROPE_BASE = 10000.0
LN_EPS = 1e-5
GN_EPS = 1e-5
DEPTH = 1
PAST_LEN = 1024
DEEPNORM_ALPHA = (2.0 * DEPTH) ** 0.25

LANES = 128
SUBLANES = 8
ROW_CHUNKS = D_MODEL // LANES
assert ROW_CHUNKS == SUBLANES

PROMPT_CHUNK = 256
SAMPLE_BATCH_BLOCK = 8
EXPERT_BLOCK = 256
COMBINE_TILE = 256
DMA_UNROLL = 8
PAD_BIT = 18
UNUSED_KEY = 1 << 30
VMEM_LIMIT_BYTES = 56 * 1024 * 1024

F32 = jnp.float32
BF16 = jnp.bfloat16


def _layer_norm(x, g, b):
    mu = jnp.mean(x, axis=-1, keepdims=True)
    xc = x - mu
    var = jnp.mean(xc * xc, axis=-1, keepdims=True)
    return xc * lax.rsqrt(var + LN_EPS) * g + b


def _load_rows(ref, n_rows):
    return jnp.concatenate(
        [ref[pl.ds(j, n_rows, stride=SUBLANES), :] for j in range(ROW_CHUNKS)], axis=1)


def _store_rows(ref, val, n_rows):
    for j in range(ROW_CHUNKS):
        ref[pl.ds(j, n_rows, stride=SUBLANES), :] = val[:, j * LANES:(j + 1) * LANES]


N_MIXER_INPUTS = 19


def _mixer_body(*refs, nb, C, n_batch_steps, n_tail):
    if not n_tail:
        _mixer_compute(*refs, nb=nb, C=C)
        return
    hs_ref, lgs_ref, h_ref, logit_ref = refs[N_MIXER_INPUTS:N_MIXER_INPUTS + 4]
    bstep = pl.program_id(0)

    @pl.when(bstep < n_batch_steps)
    def _():
        _mixer_compute(*refs[:N_MIXER_INPUTS], *refs[N_MIXER_INPUTS + 2:], nb=nb, C=C)

    @pl.when((bstep == n_batch_steps) & (pl.program_id(1) < n_tail))
    def _():
        h_ref[...] = hs_ref[...]
        logit_ref[...] = lgs_ref[...]


def _mixer_compute(x_ref, cos_ref, sin_ref, decay_ref, qdec_ref, kdec_ref, sdec_ref, win_ref, wout_ref,
                   convw_ref, convb_ref, bret_ref, bconv_ref, g1_ref, b1_ref, wr_ref, br_ref,
                   sret_ref, sconv_ref, h_ref, logit_ref, sret_out, sconv_out, mix_ref, *, nb, C):
    c = pl.program_id(1)

    @pl.when(c == 0)
    def _():
        sret_out[...] = sret_ref[...]
        sconv_out[...] = sconv_ref[...]

    x = x_ref[...].reshape(nb * C, D_MODEL)
    xb = x.astype(BF16)

    def proj(col0, width):
        return jnp.dot(xb, win_ref[:, col0:col0 + width], preferred_element_type=F32)

    q = proj(0, D_RET)
    k = proj(D_RET, D_RET)
    v = proj(2 * D_RET, D_RET)
    g = proj(3 * D_RET, D_RET)
    bg = proj(4 * D_RET, D_CONV)
    cg = proj(4 * D_RET + D_CONV, D_CONV)
    hc = proj(4 * D_RET + 2 * D_CONV, D_CONV)

    cos = cos_ref[...]
    sin = sin_ref[...]
    row = lax.broadcasted_iota(jnp.int32, (C, D_CONV), 0)
    k_scale = HEAD_DIM ** -0.5

    for b in range(nb):
        r0 = b * C
        for h in range(N_RET_HEADS):
            c0 = h * HEAD_DIM
            qh = q[r0:r0 + C, c0:c0 + HEAD_DIM]
            kh = k[r0:r0 + C, c0:c0 + HEAD_DIM]
            vh = v[r0:r0 + C, c0:c0 + HEAD_DIM]
            qh = qh * cos + pltpu.roll(qh, HEAD_DIM // 2, axis=1) * sin
            kh = (kh * cos + pltpu.roll(kh, HEAD_DIM // 2, axis=1) * sin) * k_scale
            qb = qh.astype(BF16)
            kb = kh.astype(BF16)
            vb = vh.astype(BF16)
            s_old = sret_out[b, h]
            scores = lax.dot_general(qb, kb, (((1,), (1,)), ((), ())), preferred_element_type=F32)
            scores = scores * decay_ref[h]
            intra = jnp.dot(scores.astype(BF16), vb, preferred_element_type=F32)
            cross = jnp.dot(qb, s_old.astype(BF16), preferred_element_type=F32) * qdec_ref[h]
            o = intra + cross
            kd = (kh * kdec_ref[h]).astype(BF16)
            s_new = sdec_ref[h] * s_old + lax.dot_general(
                kd, vb, (((0,), (0,)), ((), ())), preferred_element_type=F32)
            sret_out[b, h] = s_new
            mu = jnp.mean(o, axis=-1, keepdims=True)
            oc = o - mu
            var = jnp.mean(oc * oc, axis=-1, keepdims=True)
            on = oc * lax.rsqrt(var + GN_EPS)
            gh = g[r0:r0 + C, c0:c0 + HEAD_DIM]
            ret = on * (gh * jax.nn.sigmoid(gh)) * bret_ref[:, c0:c0 + HEAD_DIM]
            mix_ref[r0:r0 + C, c0:c0 + HEAD_DIM] = ret.astype(BF16)

        u = cg[r0:r0 + C] * hc[r0:r0 + C]
        prev = sconv_out[b]
        u1 = jnp.where(row == 0, prev[1:2], pltpu.roll(u, 1, axis=0))
        u2 = jnp.where(row == 0, prev[0:1], jnp.where(row == 1, prev[1:2], pltpu.roll(u, 2, axis=0)))
        z = convb_ref[...] + convw_ref[0:1] * u2 + convw_ref[1:2] * u1 + convw_ref[2:3] * u
        conv_out = bg[r0:r0 + C] * z * bconv_ref[...]
        mix_ref[r0:r0 + C, D_RET:D_RET + D_CONV] = conv_out.astype(BF16)
        sconv_out[b] = u[C - 2:C]

    m = jnp.dot(mix_ref[...], wout_ref[...], preferred_element_type=F32)
    hval = _layer_norm(DEEPNORM_ALPHA * x + m, g1_ref[...], b1_ref[...])
    _store_rows(h_ref, hval, nb * C)
    logit_ref[...] = jnp.dot(hval, wr_ref[...], preferred_element_type=F32,
                             precision=lax.Precision.HIGHEST) + br_ref[...]


def _retention_tables(C, pos0, L):
    lg = jnp.log1p(-jnp.power(2.0, -5.0 - jnp.arange(N_RET_HEADS, dtype=F32)))
    i = jnp.arange(C, dtype=F32)
    diff = i[:, None] - i[None, :]
    decay = jnp.where(diff[None] >= 0, jnp.exp(lg[:, None, None] * jnp.maximum(diff, 0.0)[None]), 0.0)
    qdec = jnp.exp(lg[:, None] * (i + 1.0)[None, :])
    kdec = jnp.exp(lg[:, None] * (C - 1.0 - i)[None, :])
    sdec = jnp.exp(lg * C)
    qdec = jnp.broadcast_to(qdec[:, :, None], (N_RET_HEADS, C, HEAD_DIM))
    kdec = jnp.broadcast_to(kdec[:, :, None], (N_RET_HEADS, C, HEAD_DIM))
    sdec = jnp.broadcast_to(sdec[:, None, None], (N_RET_HEADS, 1, HEAD_DIM))
    pos = pos0 + jnp.arange(L, dtype=jnp.int32)
    inv_freq = ROPE_BASE ** (-jnp.arange(0, HEAD_DIM, 2, dtype=F32) / HEAD_DIM)
    ang = pos.astype(F32)[:, None] * inv_freq[None, :]
    cos = jnp.cos(ang)
    sin = jnp.sin(ang)
    cos_full = jnp.concatenate([cos, cos], axis=-1)
    sin_signed = jnp.concatenate([-sin, sin], axis=-1)
    return cos_full, sin_signed, decay, qdec, kdec, sdec


def _mixer(x, s_ret, s_conv, pos0, nb, C, w, tail=None):
    B, L, _ = x.shape
    nc = L // C
    nbs = B // nb
    rows = nb * C
    n_tokens = B * L
    n_tail = 0
    if tail is not None:
        n_tokens += tail[1].shape[0]
        n_tail = tail[1].shape[0] // rows
        assert tail[1].shape[0] % rows == 0 and 0 < n_tail <= nc
    cos, sin, decay, qdec, kdec, sdec = _retention_tables(C, pos0, L)

    bb = lambda b: jnp.minimum(b, nbs - 1)
    cc = lambda b, c: jnp.where(b < nbs, c, nc - 1)
    out_blk = lambda b, c: jnp.where(b < nbs, b * nc + c, nbs * nc + jnp.minimum(c, n_tail - 1))
    const2 = lambda b, c: (0, 0)
    const3 = lambda b, c: (0, 0, 0)
    full = lambda a: pl.BlockSpec(a.shape, const2 if a.ndim == 2 else const3)
    in_arrays = [x, cos, sin, decay, qdec, kdec, sdec, w["w_in"], w["w_out"], w["conv_w"], w["conv_b"],
                 w["beta_ret"], w["beta_conv"], w["ln1_g"], w["ln1_b"], w["w_router"], w["b_router"],
                 s_ret, s_conv]
    assert len(in_arrays) == N_MIXER_INPUTS
    in_specs = [pl.BlockSpec((nb, C, D_MODEL), lambda b, c: (bb(b), cc(b, c), 0)),
                pl.BlockSpec((C, HEAD_DIM), lambda b, c: (cc(b, c), 0)),
                pl.BlockSpec((C, HEAD_DIM), lambda b, c: (cc(b, c), 0))]
    in_specs += [full(a) for a in in_arrays[3:17]]
    in_specs += [pl.BlockSpec((nb, N_RET_HEADS, HEAD_DIM, HEAD_DIM), lambda b, c: (bb(b), 0, 0, 0)),
                 pl.BlockSpec((nb, CONV_W - 1, D_CONV), lambda b, c: (bb(b), 0, 0))]
    if n_tail:
        tail_blk = lambda b, c: (jnp.where(b < nbs, 0, jnp.minimum(c, n_tail - 1)), 0)
        in_arrays += list(tail)
        in_specs += [pl.BlockSpec((rows * SUBLANES, LANES), tail_blk),
                     pl.BlockSpec((rows, N_EXPERTS), tail_blk)]
    out_shape = [jax.ShapeDtypeStruct((n_tokens * SUBLANES, LANES), F32),
                 jax.ShapeDtypeStruct((n_tokens, N_EXPERTS), F32),
                 jax.ShapeDtypeStruct((B, N_RET_HEADS, HEAD_DIM, HEAD_DIM), F32),
                 jax.ShapeDtypeStruct((B, CONV_W - 1, D_CONV), F32)]
    out_specs = [pl.BlockSpec((rows * SUBLANES, LANES), lambda b, c: (out_blk(b, c), 0)),
                 pl.BlockSpec((rows, N_EXPERTS), lambda b, c: (out_blk(b, c), 0)),
                 pl.BlockSpec((nb, N_RET_HEADS, HEAD_DIM, HEAD_DIM), lambda b, c: (bb(b), 0, 0, 0)),
                 pl.BlockSpec((nb, CONV_W - 1, D_CONV), lambda b, c: (bb(b), 0, 0))]
    return pl.pallas_call(
        functools.partial(_mixer_body, nb=nb, C=C, n_batch_steps=nbs, n_tail=n_tail),
        grid=(nbs + (1 if n_tail else 0), nc), in_specs=in_specs, out_specs=out_specs, out_shape=out_shape,
        scratch_shapes=[pltpu.VMEM((rows, D_MODEL), BF16)],
        compiler_params=pltpu.CompilerParams(
            dimension_semantics=("arbitrary", "arbitrary"), vmem_limit_bytes=VMEM_LIMIT_BYTES),
        name="mixer",
    )(*in_arrays)


def _routing_plan(logits, blk):
    T = logits.shape[0]
    A = T * TOP_K
    assert A < (1 << PAD_BIT) and N_EXPERTS * blk < (1 << PAD_BIT) and A % blk == 0
    top_vals, top_idx = lax.top_k(logits, TOP_K)
    gates = jax.nn.softmax(top_vals, axis=-1)
    flat_e = top_idx.reshape(A).astype(jnp.int32)
    experts = jnp.arange(N_EXPERTS, dtype=jnp.int32)
    counts = jnp.sum((flat_e[:, None] == experts[None, :]).astype(jnp.int32), axis=0)
    nblk_e = (counts + blk - 1) // blk
    n_pad = nblk_e * blk - counts
    real_keys = (flat_e << (PAD_BIT + 1)) | jnp.arange(A, dtype=jnp.int32)
    j = jnp.arange(blk, dtype=jnp.int32)[None, :]
    pad_keys = jnp.where(j < n_pad[:, None],
                         (experts[:, None] << (PAD_BIT + 1)) | (1 << PAD_BIT) | (experts[:, None] * blk + j),
                         UNUSED_KEY)
    slots = jnp.sort(jnp.concatenate([real_keys, pad_keys.reshape(-1)]))
    n_blocks = A // blk + N_EXPERTS
    ident = slots & ((1 << PAD_BIT) - 1)
    is_real = (((slots >> PAD_BIT) & 1) == 0) & (slots < UNUSED_KEY)
    tok = jnp.where(is_real, ident >> 2, 0)
    dst = jnp.where(is_real, (ident & (TOP_K - 1)) * T + (ident >> 2),
                    jnp.where(slots < UNUSED_KEY, A + ident, A))
    n_valid = jnp.sum(nblk_e)
    bi = jnp.arange(n_blocks, dtype=jnp.int32)
    block_e = slots.reshape(n_blocks, blk)[:, 0] >> (PAD_BIT + 1)
    block_e = jnp.where(bi < n_valid, block_e, block_e[n_valid - 1])
    return (gates, tok.reshape(n_blocks, blk), dst.reshape(n_blocks, blk), block_e,
            n_valid.reshape(1).astype(jnp.int32))


def _expert_body(be_ref, nv_ref, tokc_ref, tokn_ref, dst_ref, h_hbm, wgu_ref, bgu_ref, wd_ref, bd_ref,
                 y_hbm, xbuf, ybuf, isem, osem, wgu_bf, wd_bf, *, blk, n_blocks, n_real_rows):
    i = pl.program_id(0)
    n_valid = nv_ref[0]
    slot = i % 2

    def row_tile(r):
        return pl.ds(pl.multiple_of(r * SUBLANES, SUBLANES), SUBLANES)

    def issue_gather(tok_ref, s):
        def body(gi, carry):
            for u in range(DMA_UNROLL):
                r = gi * DMA_UNROLL + u
                pltpu.make_async_copy(h_hbm.at[row_tile(tok_ref[0, 0, r])], xbuf.at[s, row_tile(r)],
                                      isem.at[s]).start()
            return carry
        lax.fori_loop(0, blk // DMA_UNROLL, body, 0)

    def issue_scatter(s):
        def body(gi, carry):
            for u in range(DMA_UNROLL):
                r = gi * DMA_UNROLL + u
                pltpu.make_async_copy(ybuf.at[s, row_tile(r)], y_hbm.at[row_tile(dst_ref[0, 0, r])],
                                      osem.at[s]).start()
            return carry
        lax.fori_loop(0, blk // DMA_UNROLL, body, 0)

    def wait_gather(s):
        pltpu.make_async_copy(h_hbm.at[pl.ds(0, blk * SUBLANES)], xbuf.at[s], isem.at[s]).wait()

    def wait_scatter(s):
        pltpu.make_async_copy(ybuf.at[s], y_hbm.at[pl.ds(0, blk * SUBLANES)], osem.at[s]).wait()

    @pl.when(i == 0)
    def _():
        issue_gather(tokc_ref, 0)
        ybuf[0] = jnp.zeros(ybuf.shape[1:], F32)
        for e in range(N_EXPERTS):
            pltpu.make_async_copy(
                ybuf.at[0], y_hbm.at[pl.ds((n_real_rows + e * blk) * SUBLANES, blk * SUBLANES)],
                osem.at[0]).start()
        for e in range(N_EXPERTS):
            wait_scatter(0)

    @pl.when(i + 1 < n_valid)
    def _():
        issue_gather(tokn_ref, 1 - slot)

    @pl.when((i >= 2) & (i - 2 < n_valid))
    def _():
        wait_scatter(slot)

    @pl.when(i < n_valid)
    def _():
        wait_gather(slot)
        prev_e = be_ref[jnp.maximum(i - 1, 0)]

        @pl.when((i == 0) | (be_ref[i] != prev_e))
        def _():
            wgu_bf[...] = wgu_ref[0].astype(BF16)
            wd_bf[...] = wd_ref[0].astype(BF16)

        xb = _load_rows(xbuf.at[slot], blk).astype(BF16)
        gu = jnp.dot(xb, wgu_bf[...], preferred_element_type=F32) + bgu_ref[0]
        gate = jnp.minimum(gu[:, :D_FF], SWIGLU_LIMIT)
        up = jnp.clip(gu[:, D_FF:], -SWIGLU_LIMIT, SWIGLU_LIMIT)
        act = (up + 1.0) * gate * jax.nn.sigmoid(SWIGLU_ALPHA * gate)
        y = jnp.dot(act.astype(BF16), wd_bf[...], preferred_element_type=F32) + bd_ref[0]
        _store_rows(ybuf.at[slot], y, blk)
        issue_scatter(slot)

    @pl.when(i == n_blocks - 1)
    def _():
        @pl.when((i >= 1) & (i - 1 < n_valid))
        def _():
            wait_scatter(1 - slot)

        @pl.when(i < n_valid)
        def _():
            wait_scatter(slot)


def _experts(h_all, tok, dst, block_e, n_valid, w_gu, b_gu, w_down, b_down, blk, n_real_rows):
    n_blocks = block_e.shape[0]
    n_out_rows = n_real_rows + N_EXPERTS * blk
    tok3 = tok.reshape(n_blocks, 1, blk)
    dst3 = dst.reshape(n_blocks, 1, blk)
    smem_blk = lambda imap: pl.BlockSpec((1, 1, blk), imap, memory_space=pltpu.SMEM)
    grid_spec = pltpu.PrefetchScalarGridSpec(
        num_scalar_prefetch=2,
        grid=(n_blocks,),
        in_specs=[
            smem_blk(lambda i, be, nv: (i, 0, 0)),
            smem_blk(lambda i, be, nv: (jnp.minimum(i + 1, n_blocks - 1), 0, 0)),
            smem_blk(lambda i, be, nv: (i, 0, 0)),
            pl.BlockSpec(memory_space=pl.ANY),
            pl.BlockSpec((1, D_MODEL, 2 * D_FF), lambda i, be, nv: (be[i], 0, 0)),
            pl.BlockSpec((1, 1, 2 * D_FF), lambda i, be, nv: (be[i], 0, 0)),
            pl.BlockSpec((1, D_FF, D_MODEL), lambda i, be, nv: (be[i], 0, 0)),
            pl.BlockSpec((1, 1, D_MODEL), lambda i, be, nv: (be[i], 0, 0)),
        ],
        out_specs=pl.BlockSpec(memory_space=pl.ANY),
        scratch_shapes=[pltpu.VMEM((2, blk * SUBLANES, LANES), F32),
                        pltpu.VMEM((2, blk * SUBLANES, LANES), F32),
                        pltpu.SemaphoreType.DMA((2,)),
                        pltpu.SemaphoreType.DMA((2,)),
                        pltpu.VMEM((D_MODEL, 2 * D_FF), BF16),
                        pltpu.VMEM((D_FF, D_MODEL), BF16)],
    )
    return pl.pallas_call(
        functools.partial(_expert_body, blk=blk, n_blocks=n_blocks, n_real_rows=n_real_rows),
        grid_spec=grid_spec,
        out_shape=jax.ShapeDtypeStruct((n_out_rows * SUBLANES, LANES), F32),
        compiler_params=pltpu.CompilerParams(
            dimension_semantics=("arbitrary",), vmem_limit_bytes=VMEM_LIMIT_BYTES),
        name="experts",
    )(block_e, n_valid, tok3, tok3, dst3, h_all, w_gu, b_gu.reshape(N_EXPERTS, 1, 2 * D_FF),
      w_down, b_down.reshape(N_EXPERTS, 1, D_MODEL))


def _combine_body(*refs, tt, n_first):
    y_refs = refs[:TOP_K]
    h_ref, gate_ref, g2_ref, b2_ref, out_first, out_second = refs[TOP_K:]
    i = pl.program_id(0)
    gates = gate_ref[...]
    f = gates[:, 0:1] * _load_rows(y_refs[0], tt)
    for kk in range(1, TOP_K):
        f = f + gates[:, kk:kk + 1] * _load_rows(y_refs[kk], tt)
    out = _layer_norm(DEEPNORM_ALPHA * _load_rows(h_ref, tt) + f, g2_ref[...], b2_ref[...])

    @pl.when(i < n_first)
    def _():
        out_first[...] = out

    @pl.when(i >= n_first)
    def _():
        out_second[...] = out


def _combine(y_rows, gates, h_all, ln2_g, ln2_b, tt, n_tokens_first):
    T = gates.shape[0]
    n_tiles = T // tt
    n_first = n_tokens_first // tt
    assert T % tt == 0 and n_tokens_first % tt == 0 and 0 < n_first < n_tiles
    row_blk = lambda imap: pl.BlockSpec((tt * SUBLANES, LANES), imap)
    y_specs = [row_blk(functools.partial(lambda i, kk: (kk * n_tiles + i, 0), kk=kk)) for kk in range(TOP_K)]
    return pl.pallas_call(
        functools.partial(_combine_body, tt=tt, n_first=n_first),
        grid=(n_tiles,),
        in_specs=y_specs + [
            row_blk(lambda i: (i, 0)),
            pl.BlockSpec((tt, TOP_K), lambda i: (i, 0)),
            pl.BlockSpec((1, D_MODEL), lambda i: (0, 0)),
            pl.BlockSpec((1, D_MODEL), lambda i: (0, 0)),
        ],
        out_specs=[pl.BlockSpec((tt, D_MODEL), lambda i: (jnp.minimum(i, n_first - 1), 0)),
                   pl.BlockSpec((tt, D_MODEL), lambda i: (jnp.maximum(i - n_first, 0), 0))],
        out_shape=[jax.ShapeDtypeStruct((n_tokens_first, D_MODEL), F32),
                   jax.ShapeDtypeStruct((T - n_tokens_first, D_MODEL), F32)],
        compiler_params=pltpu.CompilerParams(
            dimension_semantics=("arbitrary",), vmem_limit_bytes=VMEM_LIMIT_BYTES),
        name="combine",
    )(*([y_rows] * TOP_K), h_all, gates, ln2_g, ln2_b)


def kernel(x_prompt, x_sample, state_retention, state_conv, w_in, beta_ret, beta_conv, conv_w, conv_b,
           w_out, ln1_g, ln1_b, w_router, b_router, w_gu, b_gu, w_down, b_down, ln2_g, ln2_b):
    assert w_in.shape[0] == DEPTH == 1
    B, S, _ = x_prompt.shape
    Bd, Ld, _ = x_sample.shape
    Tp, Ts = B * S, Bd * Ld
    T = Tp + Ts
    w = dict(w_in=w_in[0].astype(BF16), w_out=w_out[0].astype(BF16), conv_w=conv_w[0],
             conv_b=conv_b[0][None], beta_ret=beta_ret[0][None], beta_conv=beta_conv[0][None],
             ln1_g=ln1_g[0][None], ln1_b=ln1_b[0][None], w_router=w_router[0], b_router=b_router[0][None])

    s_ret0 = jnp.zeros((B, N_RET_HEADS, HEAD_DIM, HEAD_DIM), F32)
    s_conv0 = jnp.zeros((B, CONV_W - 1, D_CONV), F32)
    h_s, logits_s, sret_s, sconv_s = _mixer(x_sample, state_retention[0], state_conv[0], PAST_LEN,
                                            SAMPLE_BATCH_BLOCK, Ld, w)
    h_all, logits, sret_p, sconv_p = _mixer(x_prompt, s_ret0, s_conv0, 0, 1, PROMPT_CHUNK, w,
                                            tail=(h_s, logits_s))

    gates, tok, dst, block_e, n_valid = _routing_plan(logits, EXPERT_BLOCK)
    y_rows = _experts(h_all, tok, dst, block_e, n_valid, w_gu[0], b_gu[0], w_down[0], b_down[0],
                      EXPERT_BLOCK, T * TOP_K)
    y_p, y_s = _combine(y_rows, gates, h_all, ln2_g[0][None], ln2_b[0][None], COMBINE_TILE, Tp)

    return (y_p.reshape(B, S, D_MODEL), y_s.reshape(Bd, Ld, D_MODEL),
            sret_p[None], sconv_p[None], sret_s[None], sconv_s[None])
```

```python
import functools

import jax
import jax.numpy as jnp
from jax import lax
from jax.experimental import pallas as pl
from jax.experimental.pallas import tpu as pltpu

D_MODEL = 1024
N_RET_HEADS = 4
HEAD_DIM = 128
D_RET = N_RET_HEADS * HEAD_DIM
D_CONV = D_MODEL - D_RET
CONV_W = 3
N_EXPERTS = 32
TOP_K = 4
D_FF = D_MODEL
SWIGLU_LIMIT = 7.0
SWIGLU_ALPHA = 1.702
ROPE_BASE = 10000.0
LN_EPS = 1e-5
GN_EPS = 1e-5
DEPTH = 1
PAST_LEN = 1024
DEEPNORM_ALPHA = (2.0 * DEPTH) ** 0.25

LANES = 128
SUBLANES = 8
ROW_CHUNKS = D_MODEL // LANES
assert ROW_CHUNKS == SUBLANES

PROMPT_CHUNK = 256
SAMPLE_BATCH_BLOCK = 8
EXPERT_BLOCK = 256
COMBINE_TILE = 256
DMA_UNROLL = 8
PAD_BIT = 18
UNUSED_KEY = 1 << 30
VMEM_LIMIT_BYTES = 56 * 1024 * 1024

F32 = jnp.float32
BF16 = jnp.bfloat16


def _layer_norm(x, g, b):
    mu = jnp.mean(x, axis=-1, keepdims=True)
    xc = x - mu
    var = jnp.mean(xc * xc, axis=-1, keepdims=True)
    return xc * lax.rsqrt(var + LN_EPS) * g + b


def _load_rows(ref, n_rows):
    return jnp.concatenate(
        [ref[pl.ds(j, n_rows, stride=SUBLANES), :] for j in range(ROW_CHUNKS)], axis=1)


def _store_rows(ref, val, n_rows):
    for j in range(ROW_CHUNKS):
        ref[pl.ds(j, n_rows, stride=SUBLANES), :] = val[:, j * LANES:(j + 1) * LANES]


N_MIXER_INPUTS = 19


def _mixer_body(*refs, nb, C, n_batch_steps, n_tail):
    if not n_tail:
        _mixer_compute(*refs, nb=nb, C=C)
        return
    hs_ref, lgs_ref, h_ref, logit_ref = refs[N_MIXER_INPUTS:N_MIXER_INPUTS + 4]
    bstep = pl.program_id(0)

    @pl.when(bstep < n_batch_steps)
    def _():
        _mixer_compute(*refs[:N_MIXER_INPUTS], *refs[N_MIXER_INPUTS + 2:], nb=nb, C=C)

    @pl.when((bstep == n_batch_steps) & (pl.program_id(1) < n_tail))
    def _():
        h_ref[...] = hs_ref[...]
        logit_ref[...] = lgs_ref[...]


def _mixer_compute(x_ref, cos_ref, sin_ref, decay_ref, qdec_ref, kdec_ref, sdec_ref, win_ref, wout_ref,
                   convw_ref, convb_ref, bret_ref, bconv_ref, g1_ref, b1_ref, wr_ref, br_ref,
                   sret_ref, sconv_ref, h_ref, logit_ref, sret_out, sconv_out, mix_ref, *, nb, C):
    c = pl.program_id(1)

    @pl.when(c == 0)
    def _():
        sret_out[...] = sret_ref[...]
        sconv_out[...] = sconv_ref[...]

    x = x_ref[...].reshape(nb * C, D_MODEL)
    xb = x.astype(BF16)

    def proj(col0, width):
        return jnp.dot(xb, win_ref[:, col0:col0 + width], preferred_element_type=F32)

    q = proj(0, D_RET)
    k = proj(D_RET, D_RET)
    v = proj(2 * D_RET, D_RET)
    g = proj(3 * D_RET, D_RET)
    bg = proj(4 * D_RET, D_CONV)
    cg = proj(4 * D_RET + D_CONV, D_CONV)
    hc = proj(4 * D_RET + 2 * D_CONV, D_CONV)

    cos = cos_ref[...]
    sin = sin_ref[...]
    row = lax.broadcasted_iota(jnp.int32, (C, D_CONV), 0)
    k_scale = HEAD_DIM ** -0.5

    for b in range(nb):
        r0 = b * C
        for h in range(N_RET_HEADS):
            c0 = h * HEAD_DIM
            qh = q[r0:r0 + C, c0:c0 + HEAD_DIM]
            kh = k[r0:r0 + C, c0:c0 + HEAD_DIM]
            vh = v[r0:r0 + C, c0:c0 + HEAD_DIM]
            qh = qh * cos + pltpu.roll(qh, HEAD_DIM // 2, axis=1) * sin
            kh = (kh * cos + pltpu.roll(kh, HEAD_DIM // 2, axis=1) * sin) * k_scale
            qb = qh.astype(BF16)
            kb = kh.astype(BF16)
            vb = vh.astype(BF16)
            s_old = sret_out[b, h]
            scores = lax.dot_general(qb, kb, (((1,), (1,)), ((), ())), preferred_element_type=F32)
            scores = scores * decay_ref[h]
            intra = jnp.dot(scores.astype(BF16), vb, preferred_element_type=F32)
            cross = jnp.dot(qb, s_old.astype(BF16), preferred_element_type=F32) * qdec_ref[h]
            o = intra + cross
            kd = (kh * kdec_ref[h]).astype(BF16)
            s_new = sdec_ref[h] * s_old + lax.dot_general(
                kd, vb, (((0,), (0,)), ((), ())), preferred_element_type=F32)
            sret_out[b, h] = s_new
            mu = jnp.mean(o, axis=-1, keepdims=True)
            oc = o - mu
            var = jnp.mean(oc * oc, axis=-1, keepdims=True)
            on = oc * lax.rsqrt(var + GN_EPS)
            gh = g[r0:r0 + C, c0:c0 + HEAD_DIM]
            ret = on * (gh * jax.nn.sigmoid(gh)) * bret_ref[:, c0:c0 + HEAD_DIM]
            mix_ref[r0:r0 + C, c0:c0 + HEAD_DIM] = ret.astype(BF16)

        u = cg[r0:r0 + C] * hc[r0:r0 + C]
        prev = sconv_out[b]
        u1 = jnp.where(row == 0, prev[1:2], pltpu.roll(u, 1, axis=0))
        u2 = jnp.where(row == 0, prev[0:1], jnp.where(row == 1, prev[1:2], pltpu.roll(u, 2, axis=0)))
        z = convb_ref[...] + convw_ref[0:1] * u2 + convw_ref[1:2] * u1 + convw_ref[2:3] * u
        conv_out = bg[r0:r0 + C] * z * bconv_ref[...]
        mix_ref[r0:r0 + C, D_RET:D_RET + D_CONV] = conv_out.astype(BF16)
        sconv_out[b] = u[C - 2:C]

    m = jnp.dot(mix_ref[...], wout_ref[...], preferred_element_type=F32)
    hval = _layer_norm(DEEPNORM_ALPHA * x + m, g1_ref[...], b1_ref[...])
    _store_rows(h_ref, hval, nb * C)
    logit_ref[...] = jnp.dot(hval, wr_ref[...], preferred_element_type=F32,
                             precision=lax.Precision.HIGHEST) + br_ref[...]


def _retention_tables(C, pos0, L):
    lg = jnp.log1p(-jnp.power(2.0, -5.0 - jnp.arange(N_RET_HEADS, dtype=F32)))
    i = jnp.arange(C, dtype=F32)
    diff = i[:, None] - i[None, :]
    decay = jnp.where(diff[None] >= 0, jnp.exp(lg[:, None, None] * jnp.maximum(diff, 0.0)[None]), 0.0)
    qdec = jnp.exp(lg[:, None] * (i + 1.0)[None, :])
    kdec = jnp.exp(lg[:, None] * (C - 1.0 - i)[None, :])
    sdec = jnp.exp(lg * C)
    qdec = jnp.broadcast_to(qdec[:, :, None], (N_RET_HEADS, C, HEAD_DIM))
    kdec = jnp.broadcast_to(kdec[:, :, None], (N_RET_HEADS, C, HEAD_DIM))
    sdec = jnp.broadcast_to(sdec[:, None, None], (N_RET_HEADS, 1, HEAD_DIM))
    pos = pos0 + jnp.arange(L, dtype=jnp.int32)
    inv_freq = ROPE_BASE ** (-jnp.arange(0, HEAD_DIM, 2, dtype=F32) / HEAD_DIM)
    ang = pos.astype(F32)[:, None] * inv_freq[None, :]
    cos = jnp.cos(ang)
    sin = jnp.sin(ang)
    cos_full = jnp.concatenate([cos, cos], axis=-1)
    sin_signed = jnp.concatenate([-sin, sin], axis=-1)
    return cos_full, sin_signed, decay, qdec, kdec, sdec


def _mixer(x, s_ret, s_conv, pos0, nb, C, w, tail=None):
    B, L, _ = x.shape
    nc = L // C
    nbs = B // nb
    rows = nb * C
    n_tokens = B * L
    n_tail = 0
    if tail is not None:
        n_tokens += tail[1].shape[0]
        n_tail = tail[1].shape[0] // rows
        assert tail[1].shape[0] % rows == 0 and 0 < n_tail <= nc
    cos, sin, decay, qdec, kdec, sdec = _retention_tables(C, pos0, L)

    bb = lambda b: jnp.minimum(b, nbs - 1)
    cc = lambda b, c: jnp.where(b < nbs, c, nc - 1)
    out_blk = lambda b, c: jnp.where(b < nbs, b * nc + c, nbs * nc + jnp.minimum(c, n_tail - 1))
    const2 = lambda b, c: (0, 0)
    const3 = lambda b, c: (0, 0, 0)
    full = lambda a: pl.BlockSpec(a.shape, const2 if a.ndim == 2 else const3)
    in_arrays = [x, cos, sin, decay, qdec, kdec, sdec, w["w_in"], w["w_out"], w["conv_w"], w["conv_b"],
                 w["beta_ret"], w["beta_conv"], w["ln1_g"], w["ln1_b"], w["w_router"], w["b_router"],
                 s_ret, s_conv]
    assert len(in_arrays) == N_MIXER_INPUTS
    in_specs = [pl.BlockSpec((nb, C, D_MODEL), lambda b, c: (bb(b), cc(b, c), 0)),
                pl.BlockSpec((C, HEAD_DIM), lambda b, c: (cc(b, c), 0)),
                pl.BlockSpec((C, HEAD_DIM), lambda b, c: (cc(b, c), 0))]
    in_specs += [full(a) for a in in_arrays[3:17]]
    in_specs += [pl.BlockSpec((nb, N_RET_HEADS, HEAD_DIM, HEAD_DIM), lambda b, c: (bb(b), 0, 0, 0)),
                 pl.BlockSpec((nb, CONV_W - 1, D_CONV), lambda b, c: (bb(b), 0, 0))]
    if n_tail:
        tail_blk = lambda b, c: (jnp.where(b < nbs, 0, jnp.minimum(c, n_tail - 1)), 0)
        in_arrays += list(tail)
        in_specs += [pl.BlockSpec((rows * SUBLANES, LANES), tail_blk),
                     pl.BlockSpec((rows, N_EXPERTS), tail_blk)]
    out_shape = [jax.ShapeDtypeStruct((n_tokens * SUBLANES, LANES), F32),
                 jax.ShapeDtypeStruct((n_tokens, N_EXPERTS), F32),
                 jax.ShapeDtypeStruct((B, N_RET_HEADS, HEAD_DIM, HEAD_DIM), F32),
                 jax.ShapeDtypeStruct((B, CONV_W - 1, D_CONV), F32)]
    out_specs = [pl.BlockSpec((rows * SUBLANES, LANES), lambda b, c: (out_blk(b, c), 0)),
                 pl.BlockSpec((rows, N_EXPERTS), lambda b, c: (out_blk(b, c), 0)),
                 pl.BlockSpec((nb, N_RET_HEADS, HEAD_DIM, HEAD_DIM), lambda b, c: (bb(b), 0, 0, 0)),
                 pl.BlockSpec((nb, CONV_W - 1, D_CONV), lambda b, c: (bb(b), 0, 0))]
    return pl.pallas_call(
        functools.partial(_mixer_body, nb=nb, C=C, n_batch_steps=nbs, n_tail=n_tail),
        grid=(nbs + (1 if n_tail else 0), nc), in_specs=in_specs, out_specs=out_specs, out_shape=out_shape,
        scratch_shapes=[pltpu.VMEM((rows, D_MODEL), BF16)],
        compiler_params=pltpu.CompilerParams(
            dimension_semantics=("arbitrary", "arbitrary"), vmem_limit_bytes=VMEM_LIMIT_BYTES),
        name="mixer",
    )(*in_arrays)


def _routing_plan(logits, blk):
    T = logits.shape[0]
    A = T * TOP_K
    assert A < (1 << PAD_BIT) and N_EXPERTS * blk < (1 << PAD_BIT) and A % blk == 0
    top_vals, top_idx = lax.top_k(logits, TOP_K)
    gates = jax.nn.softmax(top_vals, axis=-1)
    flat_e = top_idx.reshape(A).astype(jnp.int32)
    experts = jnp.arange(N_EXPERTS, dtype=jnp.int32)
    counts = jnp.sum((flat_e[:, None] == experts[None, :]).astype(jnp.int32), axis=0)
    nblk_e = (counts + blk - 1) // blk
    n_pad = nblk_e * blk - counts
    real_keys = (flat_e << (PAD_BIT + 1)) | jnp.arange(A, dtype=jnp.int32)
    j = jnp.arange(blk, dtype=jnp.int32)[None, :]
    pad_keys = jnp.where(j < n_pad[:, None],
                         (experts[:, None] << (PAD_BIT + 1)) | (1 << PAD_BIT) | (experts[:, None] * blk + j),
                         UNUSED_KEY)
    slots = jnp.sort(jnp.concatenate([real_keys, pad_keys.reshape(-1)]))
    n_blocks = A // blk + N_EXPERTS
    ident = slots & ((1 << PAD_BIT) - 1)
    is_real = (((slots >> PAD_BIT) & 1) == 0) & (slots < UNUSED_KEY)
    tok = jnp.where(is_real, ident >> 2, 0)
    dst = jnp.where(is_real, (ident & (TOP_K - 1)) * T + (ident >> 2),
                    jnp.where(slots < UNUSED_KEY, A + ident, A))
    n_valid = jnp.sum(nblk_e)
    bi = jnp.arange(n_blocks, dtype=jnp.int32)
    block_e = slots.reshape(n_blocks, blk)[:, 0] >> (PAD_BIT + 1)
    block_e = jnp.where(bi < n_valid, block_e, block_e[n_valid - 1])
    return (gates, tok.reshape(n_blocks, blk), dst.reshape(n_blocks, blk), block_e,
            n_valid.reshape(1).astype(jnp.int32))


def _expert_body(be_ref, nv_ref, tok_ref, tokn_ref, dst_ref, h_hbm, wgu_ref, bgu_ref, wd_ref, bd_ref,
                 y_hbm, xbuf, ybuf, isem, osem, wgu_bf, wd_bf, *, blk, n_blocks, n_real_rows):
    i = pl.program_id(0)
    n_valid = nv_ref[0]
    slot = i % 2

    def row_tile(r):
        if isinstance(r, int):
            return pl.ds(r * SUBLANES, SUBLANES)
        return pl.ds(pl.multiple_of(r * SUBLANES, SUBLANES), SUBLANES)

    def for_each_row(fn, rolled):
        if rolled:
            def body(gi, carry):
                for u in range(DMA_UNROLL):
                    fn(gi * DMA_UNROLL + u)
                return carry
            lax.fori_loop(0, blk // DMA_UNROLL, body, 0)
        else:
            for r in range(blk):
                fn(r)

    def issue_gather(rows_ref, s, rolled):
        for_each_row(lambda r: pltpu.make_async_copy(
            h_hbm.at[row_tile(rows_ref[0, 0, r])], xbuf.at[s, row_tile(r)], isem.at[s]).start(), rolled)

    def issue_scatter(s, rolled):
        for_each_row(lambda r: pltpu.make_async_copy(
            ybuf.at[s, row_tile(r)], y_hbm.at[row_tile(dst_ref[0, 0, r])], osem.at[s]).start(), rolled)

    def wait_gather(s):
        pltpu.make_async_copy(h_hbm.at[pl.ds(0, blk * SUBLANES)], xbuf.at[s], isem.at[s]).wait()

    def wait_scatter(s):
        pltpu.make_async_copy(ybuf.at[s], y_hbm.at[pl.ds(0, blk * SUBLANES)], osem.at[s]).wait()

    @pl.when(i == 0)
    def _():
        issue_gather(tok_ref, 0, rolled=True)
        ybuf[...] = jnp.zeros(ybuf.shape, F32)
        for e in range(N_EXPERTS):
            pltpu.make_async_copy(
                ybuf.at[0], y_hbm.at[pl.ds((n_real_rows + e * blk) * SUBLANES, blk * SUBLANES)],
                osem.at[0]).start()
        for e in range(N_EXPERTS):
            wait_scatter(0)

    @pl.when(i < n_valid)
    def _():
        wait_gather(slot)
        prev_e = be_ref[jnp.maximum(i - 1, 0)]

        @pl.when((i == 0) | (be_ref[i] != prev_e))
        def _():
            wgu_bf[...] = wgu_ref[0].astype(BF16)
            wd_bf[...] = wd_ref[0].astype(BF16)

        xb = _load_rows(xbuf.at[slot], blk).astype(BF16)
        issue_gather(tokn_ref, 1 - slot, rolled=False)
        issue_scatter(1 - slot, rolled=False)
        gu = jnp.dot(xb, wgu_bf[...], preferred_element_type=F32) + bgu_ref[0]
        gate = jnp.minimum(gu[:, :D_FF], SWIGLU_LIMIT)
        up = jnp.clip(gu[:, D_FF:], -SWIGLU_LIMIT, SWIGLU_LIMIT)
        act = (up + 1.0) * gate * jax.nn.sigmoid(SWIGLU_ALPHA * gate)
        y = jnp.dot(act.astype(BF16), wd_bf[...], preferred_element_type=F32) + bd_ref[0]

        @pl.when(i >= 1)
        def _():
            wait_scatter(slot)

        _store_rows(ybuf.at[slot], y, blk)

    @pl.when(i == n_valid)
    def _():
        wait_gather(slot)
        wait_scatter(slot)
        issue_scatter(1 - slot, rolled=True)
        wait_scatter(1 - slot)


def _experts(h_all, tok, dst, block_e, n_valid, w_gu, b_gu, w_down, b_down, blk, n_real_rows):
    n_blocks = block_e.shape[0]
    assert n_blocks * blk == n_real_rows + N_EXPERTS * blk
    n_out_rows = n_real_rows + (N_EXPERTS + 1) * blk
    tok3 = tok.reshape(n_blocks, 1, blk)
    priming = n_real_rows + N_EXPERTS * blk + jnp.arange(blk, dtype=jnp.int32)
    dst3 = jnp.concatenate([priming[None], dst], axis=0).reshape(n_blocks + 1, 1, blk)
    smem_blk = lambda imap: pl.BlockSpec((1, 1, blk), imap, memory_space=pltpu.SMEM)
    grid_spec = pltpu.PrefetchScalarGridSpec(
        num_scalar_prefetch=2,
        grid=(n_blocks,),
        in_specs=[
            smem_blk(lambda i, be, nv: (i, 0, 0)),
            smem_blk(lambda i, be, nv: (jnp.minimum(i + 1, nv[0] - 1), 0, 0)),
            smem_blk(lambda i, be, nv: (i, 0, 0)),
            pl.BlockSpec(memory_space=pl.ANY),
            pl.BlockSpec((1, D_MODEL, 2 * D_FF), lambda i, be, nv: (be[i], 0, 0)),
            pl.BlockSpec((1, 1, 2 * D_FF), lambda i, be, nv: (be[i], 0, 0)),
            pl.BlockSpec((1, D_FF, D_MODEL), lambda i, be, nv: (be[i], 0, 0)),
            pl.BlockSpec((1, 1, D_MODEL), lambda i, be, nv: (be[i], 0, 0)),
        ],
        out_specs=pl.BlockSpec(memory_space=pl.ANY),
        scratch_shapes=[pltpu.VMEM((2, blk * SUBLANES, LANES), F32),
                        pltpu.VMEM((2, blk * SUBLANES, LANES), F32),
                        pltpu.SemaphoreType.DMA((2,)),
                        pltpu.SemaphoreType.DMA((2,)),
                        pltpu.VMEM((D_MODEL, 2 * D_FF), BF16),
                        pltpu.VMEM((D_FF, D_MODEL), BF16)],
    )
    return pl.pallas_call(
        functools.partial(_expert_body, blk=blk, n_blocks=n_blocks, n_real_rows=n_real_rows),
        grid_spec=grid_spec,
        out_shape=jax.ShapeDtypeStruct((n_out_rows * SUBLANES, LANES), F32),
        compiler_params=pltpu.CompilerParams(
            dimension_semantics=("arbitrary",), vmem_limit_bytes=VMEM_LIMIT_BYTES),
        name="experts",
    )(block_e, n_valid, tok3, tok3, dst3, h_all, w_gu, b_gu.reshape(N_EXPERTS, 1, 2 * D_FF),
      w_down, b_down.reshape(N_EXPERTS, 1, D_MODEL))


def _combine_body(*refs, tt, n_first):
    y_refs = refs[:TOP_K]
    h_ref, gate_ref, g2_ref, b2_ref, out_first, out_second = refs[TOP_K:]
    i = pl.program_id(0)
    gates = gate_ref[...]
    f = gates[:, 0:1] * _load_rows(y_refs[0], tt)
    for kk in range(1, TOP_K):
        f = f + gates[:, kk:kk + 1] * _load_rows(y_refs[kk], tt)
    out = _layer_norm(DEEPNORM_ALPHA * _load_rows(h_ref, tt) + f, g2_ref[...], b2_ref[...])

    @pl.when(i < n_first)
    def _():
        out_first[...] = out

    @pl.when(i >= n_first)
    def _():
        out_second[...] = out


def _combine(y_rows, gates, h_all, ln2_g, ln2_b, tt, n_tokens_first):
    T = gates.shape[0]
    n_tiles = T // tt
    n_first = n_tokens_first // tt
    assert T % tt == 0 and n_tokens_first % tt == 0 and 0 < n_first < n_tiles
    row_blk = lambda imap: pl.BlockSpec((tt * SUBLANES, LANES), imap)
    y_specs = [row_blk(functools.partial(lambda i, kk: (kk * n_tiles + i, 0), kk=kk)) for kk in range(TOP_K)]
    return pl.pallas_call(
        functools.partial(_combine_body, tt=tt, n_first=n_first),
        grid=(n_tiles,),
        in_specs=y_specs + [
            row_blk(lambda i: (i, 0)),
            pl.BlockSpec((tt, TOP_K), lambda i: (i, 0)),
            pl.BlockSpec((1, D_MODEL), lambda i: (0, 0)),
            pl.BlockSpec((1, D_MODEL), lambda i: (0, 0)),
        ],
        out_specs=[pl.BlockSpec((tt, D_MODEL), lambda i: (jnp.minimum(i, n_first - 1), 0)),
                   pl.BlockSpec((tt, D_MODEL), lambda i: (jnp.maximum(i - n_first, 0), 0))],
        out_shape=[jax.ShapeDtypeStruct((n_tokens_first, D_MODEL), F32),
                   jax.ShapeDtypeStruct((T - n_tokens_first, D_MODEL), F32)],
        compiler_params=pltpu.CompilerParams(
            dimension_semantics=("arbitrary",), vmem_limit_bytes=VMEM_LIMIT_BYTES),
        name="combine",
    )(*([y_rows] * TOP_K), h_all, gates, ln2_g, ln2_b)


def kernel(x_prompt, x_sample, state_retention, state_conv, w_in, beta_ret, beta_conv, conv_w, conv_b,
           w_out, ln1_g, ln1_b, w_router, b_router, w_gu, b_gu, w_down, b_down, ln2_g, ln2_b):
    assert w_in.shape[0] == DEPTH == 1
    B, S, _ = x_prompt.shape
    Bd, Ld, _ = x_sample.shape
    Tp, Ts = B * S, Bd * Ld
    T = Tp + Ts
    w = dict(w_in=w_in[0].astype(BF16), w_out=w_out[0].astype(BF16), conv_w=conv_w[0],
             conv_b=conv_b[0][None], beta_ret=beta_ret[0][None], beta_conv=beta_conv[0][None],
             ln1_g=ln1_g[0][None], ln1_b=ln1_b[0][None], w_router=w_router[0], b_router=b_router[0][None])

    s_ret0 = jnp.zeros((B, N_RET_HEADS, HEAD_DIM, HEAD_DIM), F32)
    s_conv0 = jnp.zeros((B, CONV_W - 1, D_CONV), F32)
    h_s, logits_s, sret_s, sconv_s = _mixer(x_sample, state_retention[0], state_conv[0], PAST_LEN,
                                            SAMPLE_BATCH_BLOCK, Ld, w)
    h_all, logits, sret_p, sconv_p = _mixer(x_prompt, s_ret0, s_conv0, 0, 1, PROMPT_CHUNK, w,
                                            tail=(h_s, logits_s))

    gates, tok, dst, block_e, n_valid = _routing_plan(logits, EXPERT_BLOCK)
    y_rows = _experts(h_all, tok, dst, block_e, n_valid, w_gu[0], b_gu[0], w_down[0], b_down[0],
                      EXPERT_BLOCK, T * TOP_K)
    y_p, y_s = _combine(y_rows, gates, h_all, ln2_g[0][None], ln2_b[0][None], COMBINE_TILE, Tp)

    return (y_p.reshape(B, S, D_MODEL), y_s.reshape(Bd, Ld, D_MODEL),
            sret_p[None], sconv_p[None], sret_s[None], sconv_s[None])
```

```python
import functools

import jax
import jax.numpy as jnp
from jax import lax
from jax.experimental import pallas as pl
from jax.experimental.pallas import tpu as pltpu

D_MODEL = 1024
N_RET_HEADS = 4
HEAD_DIM = 128
D_RET = N_RET_HEADS * HEAD_DIM
D_CONV = D_MODEL - D_RET
CONV_W = 3
N_EXPERTS = 32
TOP_K = 4
D_FF = D_MODEL
SWIGLU_LIMIT = 7.0
SWIGLU_ALPHA = 1.702
ROPE_BASE = 10000.0
LN_EPS = 1e-5
GN_EPS = 1e-5
DEPTH = 1
PAST_LEN = 1024
DEEPNORM_ALPHA = (2.0 * DEPTH) ** 0.25

LANES = 128
SUBLANES = 8
ROW_CHUNKS = D_MODEL // LANES
assert ROW_CHUNKS == SUBLANES

PROMPT_CHUNK = 256
SAMPLE_BATCH_BLOCK = 8
EXPERT_BLOCK = 256
COMBINE_TILE = 256
DMA_UNROLL = 8
PAD_BIT = 18
UNUSED_KEY = 1 << 30
VMEM_LIMIT_BYTES = 56 * 1024 * 1024

F32 = jnp.float32
BF16 = jnp.bfloat16


def _layer_norm(x, g, b):
    mu = jnp.mean(x, axis=-1, keepdims=True)
    xc = x - mu
    var = jnp.mean(xc * xc, axis=-1, keepdims=True)
    return xc * lax.rsqrt(var + LN_EPS) * g + b


def _load_rows(ref, n_rows):
    return jnp.concatenate(
        [ref[pl.ds(j, n_rows, stride=SUBLANES), :] for j in range(ROW_CHUNKS)], axis=1)


def _store_rows(ref, val, n_rows):
    for j in range(ROW_CHUNKS):
        ref[pl.ds(j, n_rows, stride=SUBLANES), :] = val[:, j * LANES:(j + 1) * LANES]


N_MIXER_INPUTS = 19


def _mixer_body(*refs, nb, C, n_batch_steps, n_tail):
    if not n_tail:
        _mixer_compute(*refs, nb=nb, C=C)
        return
    hs_ref, lgs_ref, h_ref, logit_ref = refs[N_MIXER_INPUTS:N_MIXER_INPUTS + 4]
    bstep = pl.program_id(0)

    @pl.when(bstep < n_batch_steps)
    def _():
        _mixer_compute(*refs[:N_MIXER_INPUTS], *refs[N_MIXER_INPUTS + 2:], nb=nb, C=C)

    @pl.when((bstep == n_batch_steps) & (pl.program_id(1) < n_tail))
    def _():
        h_ref[...] = hs_ref[...]
        logit_ref[...] = lgs_ref[...]


def _mixer_compute(x_ref, cos_ref, sin_ref, decay_ref, qdec_ref, kdec_ref, sdec_ref, win_ref, wout_ref,
                   convw_ref, convb_ref, bret_ref, bconv_ref, g1_ref, b1_ref, wr_ref, br_ref,
                   sret_ref, sconv_ref, h_ref, logit_ref, sret_out, sconv_out, mix_ref, *, nb, C):
    c = pl.program_id(1)

    @pl.when(c == 0)
    def _():
        sret_out[...] = sret_ref[...]
        sconv_out[...] = sconv_ref[...]

    x = x_ref[...].reshape(nb * C, D_MODEL)
    xb = x.astype(BF16)

    def proj(col0, width):
        return jnp.dot(xb, win_ref[:, col0:col0 + width], preferred_element_type=F32)

    q = proj(0, D_RET)
    k = proj(D_RET, D_RET)
    v = proj(2 * D_RET, D_RET)
    g = proj(3 * D_RET, D_RET)
    bg = proj(4 * D_RET, D_CONV)
    cg = proj(4 * D_RET + D_CONV, D_CONV)
    hc = proj(4 * D_RET + 2 * D_CONV, D_CONV)

    cos = cos_ref[...]
    sin = sin_ref[...]
    row = lax.broadcasted_iota(jnp.int32, (C, D_CONV), 0)
    k_scale = HEAD_DIM ** -0.5

    for b in range(nb):
        r0 = b * C
        for h in range(N_RET_HEADS):
            c0 = h * HEAD_DIM
            qh = q[r0:r0 + C, c0:c0 + HEAD_DIM]
            kh = k[r0:r0 + C, c0:c0 + HEAD_DIM]
            vh = v[r0:r0 + C, c0:c0 + HEAD_DIM]
            qh = qh * cos + pltpu.roll(qh, HEAD_DIM // 2, axis=1) * sin
            kh = (kh * cos + pltpu.roll(kh, HEAD_DIM // 2, axis=1) * sin) * k_scale
            qb = qh.astype(BF16)
            kb = kh.astype(BF16)
            vb = vh.astype(BF16)
            s_old = sret_out[b, h]
            scores = lax.dot_general(qb, kb, (((1,), (1,)), ((), ())), preferred_element_type=F32)
            scores = scores * decay_ref[h]
            intra = jnp.dot(scores.astype(BF16), vb, preferred_element_type=F32)
            cross = jnp.dot(qb, s_old.astype(BF16), preferred_element_type=F32) * qdec_ref[h]
            o = intra + cross
            kd = (kh * kdec_ref[h]).astype(BF16)
            s_new = sdec_ref[h] * s_old + lax.dot_general(
                kd, vb, (((0,), (0,)), ((), ())), preferred_element_type=F32)
            sret_out[b, h] = s_new
            mu = jnp.mean(o, axis=-1, keepdims=True)
            oc = o - mu
            var = jnp.mean(oc * oc, axis=-1, keepdims=True)
            on = oc * lax.rsqrt(var + GN_EPS)
            gh = g[r0:r0 + C, c0:c0 + HEAD_DIM]
            ret = on * (gh * jax.nn.sigmoid(gh)) * bret_ref[:, c0:c0 + HEAD_DIM]
            mix_ref[r0:r0 + C, c0:c0 + HEAD_DIM] = ret.astype(BF16)

        u = cg[r0:r0 + C] * hc[r0:r0 + C]
        prev = sconv_out[b]
        u1 = jnp.where(row == 0, prev[1:2], pltpu.roll(u, 1, axis=0))
        u2 = jnp.where(row == 0, prev[0:1], jnp.where(row == 1, prev[1:2], pltpu.roll(u, 2, axis=0)))
        z = convb_ref[...] + convw_ref[0:1] * u2 + convw_ref[1:2] * u1 + convw_ref[2:3] * u
        conv_out = bg[r0:r0 + C] * z * bconv_ref[...]
        mix_ref[r0:r0 + C, D_RET:D_RET + D_CONV] = conv_out.astype(BF16)
        sconv_out[b] = u[C - 2:C]

    m = jnp.dot(mix_ref[...], wout_ref[...], preferred_element_type=F32)
    hval = _layer_norm(DEEPNORM_ALPHA * x + m, g1_ref[...], b1_ref[...])
    _store_rows(h_ref, hval, nb * C)
    logit_ref[...] = jnp.dot(hval, wr_ref[...], preferred_element_type=F32,
                             precision=lax.Precision.HIGHEST) + br_ref[...]


def _retention_tables(C, pos0, L):
    lg = jnp.log1p(-jnp.power(2.0, -5.0 - jnp.arange(N_RET_HEADS, dtype=F32)))
    i = jnp.arange(C, dtype=F32)
    diff = i[:, None] - i[None, :]
    decay = jnp.where(diff[None] >= 0, jnp.exp(lg[:, None, None] * jnp.maximum(diff, 0.0)[None]), 0.0)
    qdec = jnp.exp(lg[:, None] * (i + 1.0)[None, :])
    kdec = jnp.exp(lg[:, None] * (C - 1.0 - i)[None, :])
    sdec = jnp.exp(lg * C)
    qdec = jnp.broadcast_to(qdec[:, :, None], (N_RET_HEADS, C, HEAD_DIM))
    kdec = jnp.broadcast_to(kdec[:, :, None], (N_RET_HEADS, C, HEAD_DIM))
    sdec = jnp.broadcast_to(sdec[:, None, None], (N_RET_HEADS, 1, HEAD_DIM))
    pos = pos0 + jnp.arange(L, dtype=jnp.int32)
    inv_freq = ROPE_BASE ** (-jnp.arange(0, HEAD_DIM, 2, dtype=F32) / HEAD_DIM)
    ang = pos.astype(F32)[:, None] * inv_freq[None, :]
    cos = jnp.cos(ang)
    sin = jnp.sin(ang)
    cos_full = jnp.concatenate([cos, cos], axis=-1)
    sin_signed = jnp.concatenate([-sin, sin], axis=-1)
    return cos_full, sin_signed, decay, qdec, kdec, sdec


def _mixer(x, s_ret, s_conv, pos0, nb, C, w, tail=None):
    B, L, _ = x.shape
    nc = L // C
    nbs = B // nb
    rows = nb * C
    n_tokens = B * L
    n_tail = 0
    if tail is not None:
        n_tokens += tail[1].shape[0]
        n_tail = tail[1].shape[0] // rows
        assert tail[1].shape[0] % rows == 0 and 0 < n_tail <= nc
    cos, sin, decay, qdec, kdec, sdec = _retention_tables(C, pos0, L)

    bb = lambda b: jnp.minimum(b, nbs - 1)
    cc = lambda b, c: jnp.where(b < nbs, c, nc - 1)
    out_blk = lambda b, c: jnp.where(b < nbs, b * nc + c, nbs * nc + jnp.minimum(c, n_tail - 1))
    const2 = lambda b, c: (0, 0)
    const3 = lambda b, c: (0, 0, 0)
    full = lambda a: pl.BlockSpec(a.shape, const2 if a.ndim == 2 else const3)
    in_arrays = [x, cos, sin, decay, qdec, kdec, sdec, w["w_in"], w["w_out"], w["conv_w"], w["conv_b"],
                 w["beta_ret"], w["beta_conv"], w["ln1_g"], w["ln1_b"], w["w_router"], w["b_router"],
                 s_ret, s_conv]
    assert len(in_arrays) == N_MIXER_INPUTS
    in_specs = [pl.BlockSpec((nb, C, D_MODEL), lambda b, c: (bb(b), cc(b, c), 0)),
                pl.BlockSpec((C, HEAD_DIM), lambda b, c: (cc(b, c), 0)),
                pl.BlockSpec((C, HEAD_DIM), lambda b, c: (cc(b, c), 0))]
    in_specs += [full(a) for a in in_arrays[3:17]]
    in_specs += [pl.BlockSpec((nb, N_RET_HEADS, HEAD_DIM, HEAD_DIM), lambda b, c: (bb(b), 0, 0, 0)),
                 pl.BlockSpec((nb, CONV_W - 1, D_CONV), lambda b, c: (bb(b), 0, 0))]
    if n_tail:
        tail_blk = lambda b, c: (jnp.where(b < nbs, 0, jnp.minimum(c, n_tail - 1)), 0)
        in_arrays += list(tail)
        in_specs += [pl.BlockSpec((rows * SUBLANES, LANES), tail_blk),
                     pl.BlockSpec((rows, N_EXPERTS), tail_blk)]
    out_shape = [jax.ShapeDtypeStruct((n_tokens * SUBLANES, LANES), F32),
                 jax.ShapeDtypeStruct((n_tokens, N_EXPERTS), F32),
                 jax.ShapeDtypeStruct((B, N_RET_HEADS, HEAD_DIM, HEAD_DIM), F32),
                 jax.ShapeDtypeStruct((B, CONV_W - 1, D_CONV), F32)]
    out_specs = [pl.BlockSpec((rows * SUBLANES, LANES), lambda b, c: (out_blk(b, c), 0)),
                 pl.BlockSpec((rows, N_EXPERTS), lambda b, c: (out_blk(b, c), 0)),
                 pl.BlockSpec((nb, N_RET_HEADS, HEAD_DIM, HEAD_DIM), lambda b, c: (bb(b), 0, 0, 0)),
                 pl.BlockSpec((nb, CONV_W - 1, D_CONV), lambda b, c: (bb(b), 0, 0))]
    return pl.pallas_call(
        functools.partial(_mixer_body, nb=nb, C=C, n_batch_steps=nbs, n_tail=n_tail),
        grid=(nbs + (1 if n_tail else 0), nc), in_specs=in_specs, out_specs=out_specs, out_shape=out_shape,
        scratch_shapes=[pltpu.VMEM((rows, D_MODEL), BF16)],
        compiler_params=pltpu.CompilerParams(
            dimension_semantics=("arbitrary", "arbitrary"), vmem_limit_bytes=VMEM_LIMIT_BYTES),
        name="mixer",
    )(*in_arrays)


def _routing_plan(logits, blk):
    T = logits.shape[0]
    A = T * TOP_K
    assert A < (1 << PAD_BIT) and N_EXPERTS * blk < (1 << PAD_BIT) and A % blk == 0
    top_vals, top_idx = lax.top_k(logits, TOP_K)
    gates = jax.nn.softmax(top_vals, axis=-1)
    flat_e = top_idx.reshape(A).astype(jnp.int32)
    experts = jnp.arange(N_EXPERTS, dtype=jnp.int32)
    counts = jnp.sum((flat_e[:, None] == experts[None, :]).astype(jnp.int32), axis=0)
    nblk_e = (counts + blk - 1) // blk
    n_pad = nblk_e * blk - counts
    real_keys = (flat_e << (PAD_BIT + 1)) | jnp.arange(A, dtype=jnp.int32)
    j = jnp.arange(blk, dtype=jnp.int32)[None, :]
    pad_keys = jnp.where(j < n_pad[:, None],
                         (experts[:, None] << (PAD_BIT + 1)) | (1 << PAD_BIT) | (experts[:, None] * blk + j),
                         UNUSED_KEY)
    slots = jnp.sort(jnp.concatenate([real_keys, pad_keys.reshape(-1)]))
    n_blocks = A // blk + N_EXPERTS
    ident = slots & ((1 << PAD_BIT) - 1)
    is_real = (((slots >> PAD_BIT) & 1) == 0) & (slots < UNUSED_KEY)
    tok = jnp.where(is_real, ident >> 2, 0)
    dst = jnp.where(is_real, (ident & (TOP_K - 1)) * T + (ident >> 2),
                    jnp.where(slots < UNUSED_KEY, A + ident, A))
    n_valid = jnp.sum(nblk_e)
    bi = jnp.arange(n_blocks, dtype=jnp.int32)
    block_e = slots.reshape(n_blocks, blk)[:, 0] >> (PAD_BIT + 1)
    block_e = jnp.where(bi < n_valid, block_e, block_e[n_valid - 1])
    return (gates, tok.reshape(n_blocks, blk), dst.reshape(n_blocks, blk), block_e,
            n_valid.reshape(1).astype(jnp.int32))


def _expert_body(be_ref, nv_ref, tok_ref, tokn_ref, dst_ref, h_hbm, wgu_ref, bgu_ref, wd_ref, bd_ref,
                 y_hbm, xbuf, ybuf, isem, osem, wgu_bf, wd_bf, *, blk, n_blocks, n_real_rows):
    i = pl.program_id(0)
    n_valid = nv_ref[0]

    def row_tile(r):
        if isinstance(r, int):
            return pl.ds(r * SUBLANES, SUBLANES)
        return pl.ds(pl.multiple_of(r * SUBLANES, SUBLANES), SUBLANES)

    def for_each_row(start_row_dma, rolled):
        if rolled:
            def body(gi, carry):
                for u in range(DMA_UNROLL):
                    start_row_dma(gi * DMA_UNROLL + u, 0)
                return carry
            lax.fori_loop(0, blk // DMA_UNROLL, body, 0)
        else:
            for r in range(blk):
                start_row_dma(r, r % 2)

    def issue_gather(rows_ref, s, rolled=False):
        for_each_row(lambda r, prio: pltpu.make_async_copy(
            h_hbm.at[row_tile(rows_ref[0, 0, r])], xbuf.at[s, row_tile(r)], isem.at[s]).start(prio), rolled)

    def issue_scatter(s):
        for_each_row(lambda r, prio: pltpu.make_async_copy(
            ybuf.at[s, row_tile(r)], y_hbm.at[row_tile(dst_ref[0, 0, r])], osem.at[s]).start(prio), False)

    def wait_gather(s):
        pltpu.make_async_copy(h_hbm.at[pl.ds(0, blk * SUBLANES)], xbuf.at[s], isem.at[s]).wait()

    def wait_scatter(s):
        pltpu.make_async_copy(ybuf.at[s], y_hbm.at[pl.ds(0, blk * SUBLANES)], osem.at[s]).wait()

    @pl.when(i == 0)
    def _():
        issue_gather(tok_ref, 0, rolled=True)
        ybuf[0] = jnp.zeros(ybuf.shape[1:], F32)
        for e in range(N_EXPERTS):
            pltpu.make_async_copy(
                ybuf.at[0], y_hbm.at[pl.ds((n_real_rows + e * blk) * SUBLANES, blk * SUBLANES)],
                osem.at[0]).start()
        for e in range(N_EXPERTS):
            wait_scatter(0)

    @pl.when((i < n_valid) & ((i == 0) | (be_ref[i] != be_ref[jnp.maximum(i - 1, 0)])))
    def _():
        wgu_bf[...] = wgu_ref[0].astype(BF16)
        wd_bf[...] = wd_ref[0].astype(BF16)

    def block_step(s):
        @pl.when(i + 1 < n_valid)
        def _():
            issue_gather(tokn_ref, 1 - s)

        @pl.when((i >= 2) & (i - 2 < n_valid))
        def _():
            wait_scatter(s)

        @pl.when(i < n_valid)
        def _():
            wait_gather(s)
            xb = _load_rows(xbuf.at[s], blk).astype(BF16)
            gu = jnp.dot(xb, wgu_bf[...], preferred_element_type=F32) + bgu_ref[0]
            gate = jnp.minimum(gu[:, :D_FF], SWIGLU_LIMIT)
            up = jnp.clip(gu[:, D_FF:], -SWIGLU_LIMIT, SWIGLU_LIMIT)
            act = (up + 1.0) * gate * jax.nn.sigmoid(SWIGLU_ALPHA * gate)
            y = jnp.dot(act.astype(BF16), wd_bf[...], preferred_element_type=F32) + bd_ref[0]
            _store_rows(ybuf.at[s], y, blk)
            issue_scatter(s)

    for s in range(2):
        pl.when(i % 2 == s)(functools.partial(block_step, s))

    @pl.when(i == n_blocks - 1)
    def _():
        slot = i % 2

        @pl.when((i >= 1) & (i - 1 < n_valid))
        def _():
            wait_scatter(1 - slot)

        @pl.when(i < n_valid)
        def _():
            wait_scatter(slot)


def _experts(h_all, tok, dst, block_e, n_valid, w_gu, b_gu, w_down, b_down, blk, n_real_rows):
    n_blocks = block_e.shape[0]
    n_out_rows = n_real_rows + N_EXPERTS * blk
    tok3 = tok.reshape(n_blocks, 1, blk)
    dst3 = dst.reshape(n_blocks, 1, blk)
    smem_blk = lambda imap: pl.BlockSpec((1, 1, blk), imap, memory_space=pltpu.SMEM)
    grid_spec = pltpu.PrefetchScalarGridSpec(
        num_scalar_prefetch=2,
        grid=(n_blocks,),
        in_specs=[
            smem_blk(lambda i, be, nv: (i, 0, 0)),
            smem_blk(lambda i, be, nv: (jnp.minimum(i + 1, n_blocks - 1), 0, 0)),
            smem_blk(lambda i, be, nv: (i, 0, 0)),
            pl.BlockSpec(memory_space=pl.ANY),
            pl.BlockSpec((1, D_MODEL, 2 * D_FF), lambda i, be, nv: (be[i], 0, 0)),
            pl.BlockSpec((1, 1, 2 * D_FF), lambda i, be, nv: (be[i], 0, 0)),
            pl.BlockSpec((1, D_FF, D_MODEL), lambda i, be, nv: (be[i], 0, 0)),
            pl.BlockSpec((1, 1, D_MODEL), lambda i, be, nv: (be[i], 0, 0)),
        ],
        out_specs=pl.BlockSpec(memory_space=pl.ANY),
        scratch_shapes=[pltpu.VMEM((2, blk * SUBLANES, LANES), F32),
                        pltpu.VMEM((2, blk * SUBLANES, LANES), F32),
                        pltpu.SemaphoreType.DMA((2,)),
                        pltpu.SemaphoreType.DMA((2,)),
                        pltpu.VMEM((D_MODEL, 2 * D_FF), BF16),
                        pltpu.VMEM((D_FF, D_MODEL), BF16)],
    )
    return pl.pallas_call(
        functools.partial(_expert_body, blk=blk, n_blocks=n_blocks, n_real_rows=n_real_rows),
        grid_spec=grid_spec,
        out_shape=jax.ShapeDtypeStruct((n_out_rows * SUBLANES, LANES), F32),
        compiler_params=pltpu.CompilerParams(
            dimension_semantics=("arbitrary",), vmem_limit_bytes=VMEM_LIMIT_BYTES),
        name="experts",
    )(block_e, n_valid, tok3, tok3, dst3, h_all, w_gu, b_gu.reshape(N_EXPERTS, 1, 2 * D_FF),
      w_down, b_down.reshape(N_EXPERTS, 1, D_MODEL))


def _combine_body(*refs, tt, n_first):
    y_refs = refs[:TOP_K]
    h_ref, gate_ref, g2_ref, b2_ref, out_first, out_second = refs[TOP_K:]
    i = pl.program_id(0)
    gates = gate_ref[...]
    f = gates[:, 0:1] * _load_rows(y_refs[0], tt)
    for kk in range(1, TOP_K):
        f = f + gates[:, kk:kk + 1] * _load_rows(y_refs[kk], tt)
    out = _layer_norm(DEEPNORM_ALPHA * _load_rows(h_ref, tt) + f, g2_ref[...], b2_ref[...])

    @pl.when(i < n_first)
    def _():
        out_first[...] = out

    @pl.when(i >= n_first)
    def _():
        out_second[...] = out


def _combine(y_rows, gates, h_all, ln2_g, ln2_b, tt, n_tokens_first):
    T = gates.shape[0]
    n_tiles = T // tt
    n_first = n_tokens_first // tt
    assert T % tt == 0 and n_tokens_first % tt == 0 and 0 < n_first < n_tiles
    row_blk = lambda imap: pl.BlockSpec((tt * SUBLANES, LANES), imap)
    y_specs = [row_blk(functools.partial(lambda i, kk: (kk * n_tiles + i, 0), kk=kk)) for kk in range(TOP_K)]
    return pl.pallas_call(
        functools.partial(_combine_body, tt=tt, n_first=n_first),
        grid=(n_tiles,),
        in_specs=y_specs + [
            row_blk(lambda i: (i, 0)),
            pl.BlockSpec((tt, TOP_K), lambda i: (i, 0)),
            pl.BlockSpec((1, D_MODEL), lambda i: (0, 0)),
            pl.BlockSpec((1, D_MODEL), lambda i: (0, 0)),
        ],
        out_specs=[pl.BlockSpec((tt, D_MODEL), lambda i: (jnp.minimum(i, n_first - 1), 0)),
                   pl.BlockSpec((tt, D_MODEL), lambda i: (jnp.maximum(i - n_first, 0), 0))],
        out_shape=[jax.ShapeDtypeStruct((n_tokens_first, D_MODEL), F32),
                   jax.ShapeDtypeStruct((T - n_tokens_first, D_MODEL), F32)],
        compiler_params=pltpu.CompilerParams(
            dimension_semantics=("arbitrary",), vmem_limit_bytes=VMEM_LIMIT_BYTES),
        name="combine",
    )(*([y_rows] * TOP_K), h_all, gates, ln2_g, ln2_b)


def kernel(x_prompt, x_sample, state_retention, state_conv, w_in, beta_ret, beta_conv, conv_w, conv_b,
           w_out, ln1_g, ln1_b, w_router, b_router, w_gu, b_gu, w_down, b_down, ln2_g, ln2_b):
    assert w_in.shape[0] == DEPTH == 1
    B, S, _ = x_prompt.shape
    Bd, Ld, _ = x_sample.shape
    Tp, Ts = B * S, Bd * Ld
    T = Tp + Ts
    w = dict(w_in=w_in[0].astype(BF16), w_out=w_out[0].astype(BF16), conv_w=conv_w[0],
             conv_b=conv_b[0][None], beta_ret=beta_ret[0][None], beta_conv=beta_conv[0][None],
             ln1_g=ln1_g[0][None], ln1_b=ln1_b[0][None], w_router=w_router[0], b_router=b_router[0][None])

    s_ret0 = jnp.zeros((B, N_RET_HEADS, HEAD_DIM, HEAD_DIM), F32)
    s_conv0 = jnp.zeros((B, CONV_W - 1, D_CONV), F32)
    h_s, logits_s, sret_s, sconv_s = _mixer(x_sample, state_retention[0], state_conv[0], PAST_LEN,
                                            SAMPLE_BATCH_BLOCK, Ld, w)
    h_all, logits, sret_p, sconv_p = _mixer(x_prompt, s_ret0, s_conv0, 0, 1, PROMPT_CHUNK, w,
                                            tail=(h_s, logits_s))

    gates, tok, dst, block_e, n_valid = _routing_plan(logits, EXPERT_BLOCK)
    y_rows = _experts(h_all, tok, dst, block_e, n_valid, w_gu[0], b_gu[0], w_down[0], b_down[0],
                      EXPERT_BLOCK, T * TOP_K)
    y_p, y_s = _combine(y_rows, gates, h_all, ln2_g[0][None], ln2_b[0][None], COMBINE_TILE, Tp)

    return (y_p.reshape(B, S, D_MODEL), y_s.reshape(Bd, Ld, D_MODEL),
            sret_p[None], sconv_p[None], sret_s[None], sconv_s[None])
```

```python
import functools

import jax
import jax.numpy as jnp
from jax import lax
from jax.experimental import pallas as pl
from jax.experimental.pallas import tpu as pltpu

D_MODEL = 1024
N_RET_HEADS = 4
HEAD_DIM = 128
D_RET = N_RET_HEADS * HEAD_DIM
D_CONV = D_MODEL - D_RET
CONV_W = 3
N_EXPERTS = 32
TOP_K = 4
D_FF = D_MODEL
SWIGLU_LIMIT = 7.0
SWIGLU_ALPHA = 1.702
ROPE_BASE = 10000.0
LN_EPS = 1e-5
GN_EPS = 1e-5
DEPTH = 1
PAST_LEN = 1024
DEEPNORM_ALPHA = (2.0 * DEPTH) ** 0.25

LANES = 128
SUBLANES = 8
ROW_CHUNKS = D_MODEL // LANES
assert ROW_CHUNKS == SUBLANES

PROMPT_CHUNK = 256
SAMPLE_BATCH_BLOCK = 8
EXPERT_BLOCK = 256
COMBINE_TILE = 256
DMA_UNROLL = 8
PAD_BIT = 18
UNUSED_KEY = 1 << 30
VMEM_LIMIT_BYTES = 56 * 1024 * 1024

F32 = jnp.float32
BF16 = jnp.bfloat16


def _layer_norm(x, g, b):
    mu = jnp.mean(x, axis=-1, keepdims=True)
    xc = x - mu
    var = jnp.mean(xc * xc, axis=-1, keepdims=True)
    return xc * lax.rsqrt(var + LN_EPS) * g + b


def _load_rows(ref, n_rows):
    return jnp.concatenate(
        [ref[pl.ds(j, n_rows, stride=SUBLANES), :] for j in range(ROW_CHUNKS)], axis=1)


def _store_rows(ref, val, n_rows):
    for j in range(ROW_CHUNKS):
        ref[pl.ds(j, n_rows, stride=SUBLANES), :] = val[:, j * LANES:(j + 1) * LANES]


N_MIXER_INPUTS = 19
N_TOKEN_OUTPUTS = 3


def _mixer_body(*refs, nb, C, n_batch_steps, n_tail):
    if not n_tail:
        _mixer_compute(*refs, nb=nb, C=C)
        return
    tails = refs[N_MIXER_INPUTS:N_MIXER_INPUTS + N_TOKEN_OUTPUTS]
    outs = refs[N_MIXER_INPUTS + N_TOKEN_OUTPUTS:N_MIXER_INPUTS + 2 * N_TOKEN_OUTPUTS]
    bstep = pl.program_id(0)

    @pl.when(bstep < n_batch_steps)
    def _():
        _mixer_compute(*refs[:N_MIXER_INPUTS], *refs[N_MIXER_INPUTS + N_TOKEN_OUTPUTS:], nb=nb, C=C)

    @pl.when((bstep == n_batch_steps) & (pl.program_id(1) < n_tail))
    def _():
        for tail_ref, out_ref in zip(tails, outs):
            out_ref[...] = tail_ref[...]


def _route(logits_t):
    expert = lax.broadcasted_iota(jnp.int32, logits_t.shape, 0)
    vals, idxs = [], []
    for _ in range(TOP_K):
        m = jnp.max(logits_t, axis=0, keepdims=True)
        idx = jnp.min(jnp.where(logits_t == m, expert, N_EXPERTS), axis=0, keepdims=True)
        vals.append(m)
        idxs.append(idx)
        logits_t = jnp.where(expert == idx, -jnp.inf, logits_t)
    ex = [jnp.exp(v - vals[0]) for v in vals]
    total = ex[0]
    for e in ex[1:]:
        total = total + e
    return jnp.concatenate(idxs, axis=0), jnp.concatenate([e / total for e in ex], axis=0)


def _mixer_compute(x_ref, cos_ref, sin_ref, decay_ref, qdec_ref, kdec_ref, sdec_ref, win_ref, wout_ref,
                   convw_ref, convb_ref, bret_ref, bconv_ref, g1_ref, b1_ref, wr_ref, br_ref,
                   sret_ref, sconv_ref, h_ref, choice_ref, gate_ref, sret_out, sconv_out, mix_ref,
                   *, nb, C):
    c = pl.program_id(1)

    @pl.when(c == 0)
    def _():
        sret_out[...] = sret_ref[...]
        sconv_out[...] = sconv_ref[...]

    x = x_ref[...].reshape(nb * C, D_MODEL)
    xb = x.astype(BF16)

    def proj(col0, width):
        return jnp.dot(xb, win_ref[:, col0:col0 + width], preferred_element_type=F32)

    q = proj(0, D_RET)
    k = proj(D_RET, D_RET)
    v = proj(2 * D_RET, D_RET)
    g = proj(3 * D_RET, D_RET)
    bg = proj(4 * D_RET, D_CONV)
    cg = proj(4 * D_RET + D_CONV, D_CONV)
    hc = proj(4 * D_RET + 2 * D_CONV, D_CONV)

    cos = cos_ref[...]
    sin = sin_ref[...]
    row = lax.broadcasted_iota(jnp.int32, (C, D_CONV), 0)
    k_scale = HEAD_DIM ** -0.5

    for b in range(nb):
        r0 = b * C
        for h in range(N_RET_HEADS):
            c0 = h * HEAD_DIM
            qh = q[r0:r0 + C, c0:c0 + HEAD_DIM]
            kh = k[r0:r0 + C, c0:c0 + HEAD_DIM]
            vh = v[r0:r0 + C, c0:c0 + HEAD_DIM]
            qh = qh * cos + pltpu.roll(qh, HEAD_DIM // 2, axis=1) * sin
            kh = (kh * cos + pltpu.roll(kh, HEAD_DIM // 2, axis=1) * sin) * k_scale
            qb = qh.astype(BF16)
            kb = kh.astype(BF16)
            vb = vh.astype(BF16)
            s_old = sret_out[b, h]
            scores = lax.dot_general(qb, kb, (((1,), (1,)), ((), ())), preferred_element_type=F32)
            scores = scores * decay_ref[h]
            intra = jnp.dot(scores.astype(BF16), vb, preferred_element_type=F32)
            cross = jnp.dot(qb, s_old.astype(BF16), preferred_element_type=F32) * qdec_ref[h]
            o = intra + cross
            kd = (kh * kdec_ref[h]).astype(BF16)
            s_new = sdec_ref[h] * s_old + lax.dot_general(
                kd, vb, (((0,), (0,)), ((), ())), preferred_element_type=F32)
            sret_out[b, h] = s_new
            mu = jnp.mean(o, axis=-1, keepdims=True)
            oc = o - mu
            var = jnp.mean(oc * oc, axis=-1, keepdims=True)
            on = oc * lax.rsqrt(var + GN_EPS)
            gh = g[r0:r0 + C, c0:c0 + HEAD_DIM]
            ret = on * (gh * jax.nn.sigmoid(gh)) * bret_ref[:, c0:c0 + HEAD_DIM]
            mix_ref[r0:r0 + C, c0:c0 + HEAD_DIM] = ret.astype(BF16)

        u = cg[r0:r0 + C] * hc[r0:r0 + C]
        prev = sconv_out[b]
        u1 = jnp.where(row == 0, prev[1:2], pltpu.roll(u, 1, axis=0))
        u2 = jnp.where(row == 0, prev[0:1], jnp.where(row == 1, prev[1:2], pltpu.roll(u, 2, axis=0)))
        z = convb_ref[...] + convw_ref[0:1] * u2 + convw_ref[1:2] * u1 + convw_ref[2:3] * u
        conv_out = bg[r0:r0 + C] * z * bconv_ref[...]
        mix_ref[r0:r0 + C, D_RET:D_RET + D_CONV] = conv_out.astype(BF16)
        sconv_out[b] = u[C - 2:C]

    m = jnp.dot(mix_ref[...], wout_ref[...], preferred_element_type=F32)
    hval = _layer_norm(DEEPNORM_ALPHA * x + m, g1_ref[...], b1_ref[...])
    _store_rows(h_ref, hval, nb * C)
    h_hi = hval.astype(BF16)
    h_lo = (hval - h_hi.astype(F32)).astype(BF16)
    nt_dot = lambda a, b: lax.dot_general(a, b, (((1,), (1,)), ((), ())), preferred_element_type=F32)
    logits_t = nt_dot(wr_ref[0], h_hi) + nt_dot(wr_ref[0], h_lo) + nt_dot(wr_ref[1], h_hi) + br_ref[...]
    choice_ref[...], gate_ref[...] = _route(logits_t)


def _retention_tables(C, pos0, L):
    lg = jnp.log1p(-jnp.power(2.0, -5.0 - jnp.arange(N_RET_HEADS, dtype=F32)))
    i = jnp.arange(C, dtype=F32)
    diff = i[:, None] - i[None, :]
    decay = jnp.where(diff[None] >= 0, jnp.exp(lg[:, None, None] * jnp.maximum(diff, 0.0)[None]), 0.0)
    qdec = jnp.exp(lg[:, None] * (i + 1.0)[None, :])
    kdec = jnp.exp(lg[:, None] * (C - 1.0 - i)[None, :])
    sdec = jnp.exp(lg * C)
    qdec = jnp.broadcast_to(qdec[:, :, None], (N_RET_HEADS, C, HEAD_DIM))
    kdec = jnp.broadcast_to(kdec[:, :, None], (N_RET_HEADS, C, HEAD_DIM))
    sdec = jnp.broadcast_to(sdec[:, None, None], (N_RET_HEADS, 1, HEAD_DIM))
    pos = pos0 + jnp.arange(L, dtype=jnp.int32)
    inv_freq = ROPE_BASE ** (-jnp.arange(0, HEAD_DIM, 2, dtype=F32) / HEAD_DIM)
    ang = pos.astype(F32)[:, None] * inv_freq[None, :]
    cos = jnp.cos(ang)
    sin = jnp.sin(ang)
    cos_full = jnp.concatenate([cos, cos], axis=-1)
    sin_signed = jnp.concatenate([-sin, sin], axis=-1)
    return cos_full, sin_signed, decay, qdec, kdec, sdec


def _mixer(x, s_ret, s_conv, pos0, nb, C, w, tail=None):
    B, L, _ = x.shape
    nc = L // C
    nbs = B // nb
    rows = nb * C
    n_tokens = B * L
    n_tail = 0
    if tail is not None:
        n_tokens += tail[1].shape[1]
        n_tail = tail[1].shape[1] // rows
        assert tail[1].shape[1] % rows == 0 and 0 < n_tail <= nc
    cos, sin, decay, qdec, kdec, sdec = _retention_tables(C, pos0, L)

    bb = lambda b: jnp.minimum(b, nbs - 1)
    cc = lambda b, c: jnp.where(b < nbs, c, nc - 1)
    out_blk = lambda b, c: jnp.where(b < nbs, b * nc + c, nbs * nc + jnp.minimum(c, n_tail - 1))
    const2 = lambda b, c: (0, 0)
    const3 = lambda b, c: (0, 0, 0)
    full = lambda a: pl.BlockSpec(a.shape, const2 if a.ndim == 2 else const3)
    in_arrays = [x, cos, sin, decay, qdec, kdec, sdec, w["w_in"], w["w_out"], w["conv_w"], w["conv_b"],
                 w["beta_ret"], w["beta_conv"], w["ln1_g"], w["ln1_b"], w["w_router"], w["b_router"],
                 s_ret, s_conv]
    assert len(in_arrays) == N_MIXER_INPUTS
    in_specs = [pl.BlockSpec((nb, C, D_MODEL), lambda b, c: (bb(b), cc(b, c), 0)),
                pl.BlockSpec((C, HEAD_DIM), lambda b, c: (cc(b, c), 0)),
                pl.BlockSpec((C, HEAD_DIM), lambda b, c: (cc(b, c), 0))]
    in_specs += [full(a) for a in in_arrays[3:17]]
    in_specs += [pl.BlockSpec((nb, N_RET_HEADS, HEAD_DIM, HEAD_DIM), lambda b, c: (bb(b), 0, 0, 0)),
                 pl.BlockSpec((nb, CONV_W - 1, D_CONV), lambda b, c: (bb(b), 0, 0))]
    if n_tail:
        tail_blk = lambda b, c: jnp.where(b < nbs, 0, jnp.minimum(c, n_tail - 1))
        in_arrays += list(tail)
        in_specs += [pl.BlockSpec((rows * SUBLANES, LANES), lambda b, c: (tail_blk(b, c), 0)),
                     pl.BlockSpec((TOP_K, rows), lambda b, c: (0, tail_blk(b, c))),
                     pl.BlockSpec((TOP_K, rows), lambda b, c: (0, tail_blk(b, c)))]
    out_shape = [jax.ShapeDtypeStruct((n_tokens * SUBLANES, LANES), F32),
                 jax.ShapeDtypeStruct((TOP_K, n_tokens), jnp.int32),
                 jax.ShapeDtypeStruct((TOP_K, n_tokens), F32),
                 jax.ShapeDtypeStruct((B, N_RET_HEADS, HEAD_DIM, HEAD_DIM), F32),
                 jax.ShapeDtypeStruct((B, CONV_W - 1, D_CONV), F32)]
    assert len(out_shape) == N_TOKEN_OUTPUTS + 2
    out_specs = [pl.BlockSpec((rows * SUBLANES, LANES), lambda b, c: (out_blk(b, c), 0)),
                 pl.BlockSpec((TOP_K, rows), lambda b, c: (0, out_blk(b, c))),
                 pl.BlockSpec((TOP_K, rows), lambda b, c: (0, out_blk(b, c))),
                 pl.BlockSpec((nb, N_RET_HEADS, HEAD_DIM, HEAD_DIM), lambda b, c: (bb(b), 0, 0, 0)),
                 pl.BlockSpec((nb, CONV_W - 1, D_CONV), lambda b, c: (bb(b), 0, 0))]
    return pl.pallas_call(
        functools.partial(_mixer_body, nb=nb, C=C, n_batch_steps=nbs, n_tail=n_tail),
        grid=(nbs + (1 if n_tail else 0), nc), in_specs=in_specs, out_specs=out_specs, out_shape=out_shape,
        scratch_shapes=[pltpu.VMEM((rows, D_MODEL), BF16)],
        compiler_params=pltpu.CompilerParams(
            dimension_semantics=("arbitrary", "arbitrary"), vmem_limit_bytes=VMEM_LIMIT_BYTES),
        name="mixer",
    )(*in_arrays)


def _routing_plan(choices_t, blk):
    T = choices_t.shape[1]
    A = T * TOP_K
    assert A < (1 << PAD_BIT) and N_EXPERTS * blk < (1 << PAD_BIT) and A % blk == 0
    assign_id = (jnp.arange(T, dtype=jnp.int32)[None, :] * TOP_K
                 + jnp.arange(TOP_K, dtype=jnp.int32)[:, None]).reshape(A)
    flat_e = choices_t.reshape(A)
    experts = jnp.arange(N_EXPERTS, dtype=jnp.int32)
    counts = jnp.sum((flat_e[:, None] == experts[None, :]).astype(jnp.int32), axis=0)
    nblk_e = (counts + blk - 1) // blk
    n_pad = nblk_e * blk - counts
    real_keys = (flat_e << (PAD_BIT + 1)) | assign_id
    j = jnp.arange(blk, dtype=jnp.int32)[None, :]
    pad_keys = jnp.where(j < n_pad[:, None],
                         (experts[:, None] << (PAD_BIT + 1)) | (1 << PAD_BIT) | (experts[:, None] * blk + j),
                         UNUSED_KEY)
    slots = jnp.sort(jnp.concatenate([real_keys, pad_keys.reshape(-1)]))
    n_blocks = A // blk + N_EXPERTS
    ident = slots & ((1 << PAD_BIT) - 1)
    is_real = (((slots >> PAD_BIT) & 1) == 0) & (slots < UNUSED_KEY)
    tok = jnp.where(is_real, ident >> 2, 0)
    dst = jnp.where(is_real, (ident & (TOP_K - 1)) * T + (ident >> 2),
                    jnp.where(slots < UNUSED_KEY, A + ident, A))
    n_valid = jnp.sum(nblk_e)
    bi = jnp.arange(n_blocks, dtype=jnp.int32)
    block_e = slots.reshape(n_blocks, blk)[:, 0] >> (PAD_BIT + 1)
    block_e = jnp.where(bi < n_valid, block_e, block_e[n_valid - 1])
    return tok.reshape(n_blocks, blk), dst.reshape(n_blocks, blk), block_e, n_valid.reshape(1).astype(jnp.int32)


def _expert_body(be_ref, nv_ref, tok_ref, tokn_ref, dst_ref, h_hbm, wgu_ref, bgu_ref, wd_ref, bd_ref,
                 y_hbm, xbuf, ybuf, isem, osem, wgu_bf, wd_bf, *, blk, n_blocks, n_real_rows):
    i = pl.program_id(0)
    n_valid = nv_ref[0]

    def row_tile(r):
        if isinstance(r, int):
            return pl.ds(r * SUBLANES, SUBLANES)
        return pl.ds(pl.multiple_of(r * SUBLANES, SUBLANES), SUBLANES)

    def for_each_row(start_row_dma, rolled):
        if rolled:
            def body(gi, carry):
                for u in range(DMA_UNROLL):
                    start_row_dma(gi * DMA_UNROLL + u, 0)
                return carry
            lax.fori_loop(0, blk // DMA_UNROLL, body, 0)
        else:
            for r in range(blk):
                start_row_dma(r, r % 2)

    def issue_gather(rows_ref, s, rolled=False):
        for_each_row(lambda r, prio: pltpu.make_async_copy(
            h_hbm.at[row_tile(rows_ref[0, 0, r])], xbuf.at[s, row_tile(r)], isem.at[s]).start(prio), rolled)

    def issue_scatter(s):
        for_each_row(lambda r, prio: pltpu.make_async_copy(
            ybuf.at[s, row_tile(r)], y_hbm.at[row_tile(dst_ref[0, 0, r])], osem.at[s]).start(prio), False)

    def wait_gather(s):
        pltpu.make_async_copy(h_hbm.at[pl.ds(0, blk * SUBLANES)], xbuf.at[s], isem.at[s]).wait()

    def wait_scatter(s):
        pltpu.make_async_copy(ybuf.at[s], y_hbm.at[pl.ds(0, blk * SUBLANES)], osem.at[s]).wait()

    @pl.when(i == 0)
    def _():
        issue_gather(tok_ref, 0, rolled=True)
        ybuf[0] = jnp.zeros(ybuf.shape[1:], F32)
        for e in range(N_EXPERTS):
            pltpu.make_async_copy(
                ybuf.at[0], y_hbm.at[pl.ds((n_real_rows + e * blk) * SUBLANES, blk * SUBLANES)],
                osem.at[0]).start()
        for e in range(N_EXPERTS):
            wait_scatter(0)

    @pl.when((i < n_valid) & ((i == 0) | (be_ref[i] != be_ref[jnp.maximum(i - 1, 0)])))
    def _():
        wgu_bf[...] = wgu_ref[0].astype(BF16)
        wd_bf[...] = wd_ref[0].astype(BF16)

    def block_step(s):
        @pl.when(i + 1 < n_valid)
        def _():
            issue_gather(tokn_ref, 1 - s)

        @pl.when((i >= 2) & (i - 2 < n_valid))
        def _():
            wait_scatter(s)

        @pl.when(i < n_valid)
        def _():
            wait_gather(s)
            xb = _load_rows(xbuf.at[s], blk).astype(BF16)
            gu = jnp.dot(xb, wgu_bf[...], preferred_element_type=F32) + bgu_ref[0]
            gate = jnp.minimum(gu[:, :D_FF], SWIGLU_LIMIT)
            up = jnp.clip(gu[:, D_FF:], -SWIGLU_LIMIT, SWIGLU_LIMIT)
            act = (up + 1.0) * gate * jax.nn.sigmoid(SWIGLU_ALPHA * gate)
            y = jnp.dot(act.astype(BF16), wd_bf[...], preferred_element_type=F32) + bd_ref[0]
            _store_rows(ybuf.at[s], y, blk)
            issue_scatter(s)

    for s in range(2):
        pl.when(i % 2 == s)(functools.partial(block_step, s))

    @pl.when(i == n_blocks - 1)
    def _():
        slot = i % 2

        @pl.when((i >= 1) & (i - 1 < n_valid))
        def _():
            wait_scatter(1 - slot)

        @pl.when(i < n_valid)
        def _():
            wait_scatter(slot)


def _experts(h_all, tok, dst, block_e, n_valid, w_gu, b_gu, w_down, b_down, blk, n_real_rows):
    n_blocks = block_e.shape[0]
    n_out_rows = n_real_rows + N_EXPERTS * blk
    tok3 = tok.reshape(n_blocks, 1, blk)
    dst3 = dst.reshape(n_blocks, 1, blk)
    smem_blk = lambda imap: pl.BlockSpec((1, 1, blk), imap, memory_space=pltpu.SMEM)
    grid_spec = pltpu.PrefetchScalarGridSpec(
        num_scalar_prefetch=2,
        grid=(n_blocks,),
        in_specs=[
            smem_blk(lambda i, be, nv: (i, 0, 0)),
            smem_blk(lambda i, be, nv: (jnp.minimum(i + 1, n_blocks - 1), 0, 0)),
            smem_blk(lambda i, be, nv: (i, 0, 0)),
            pl.BlockSpec(memory_space=pl.ANY),
            pl.BlockSpec((1, D_MODEL, 2 * D_FF), lambda i, be, nv: (be[i], 0, 0)),
            pl.BlockSpec((1, 1, 2 * D_FF), lambda i, be, nv: (be[i], 0, 0)),
            pl.BlockSpec((1, D_FF, D_MODEL), lambda i, be, nv: (be[i], 0, 0)),
            pl.BlockSpec((1, 1, D_MODEL), lambda i, be, nv: (be[i], 0, 0)),
        ],
        out_specs=pl.BlockSpec(memory_space=pl.ANY),
        scratch_shapes=[pltpu.VMEM((2, blk * SUBLANES, LANES), F32),
                        pltpu.VMEM((2, blk * SUBLANES, LANES), F32),
                        pltpu.SemaphoreType.DMA((2,)),
                        pltpu.SemaphoreType.DMA((2,)),
                        pltpu.VMEM((D_MODEL, 2 * D_FF), BF16),
                        pltpu.VMEM((D_FF, D_MODEL), BF16)],
    )
    return pl.pallas_call(
        functools.partial(_expert_body, blk=blk, n_blocks=n_blocks, n_real_rows=n_real_rows),
        grid_spec=grid_spec,
        out_shape=jax.ShapeDtypeStruct((n_out_rows * SUBLANES, LANES), F32),
        compiler_params=pltpu.CompilerParams(
            dimension_semantics=("arbitrary",), vmem_limit_bytes=VMEM_LIMIT_BYTES),
        name="experts",
    )(block_e, n_valid, tok3, tok3, dst3, h_all, w_gu, b_gu.reshape(N_EXPERTS, 1, 2 * D_FF),
      w_down, b_down.reshape(N_EXPERTS, 1, D_MODEL))


def _combine_body(*refs, tt, n_first):
    y_refs = refs[:TOP_K]
    h_ref, gate_ref, g2_ref, b2_ref, out_first, out_second = refs[TOP_K:]
    i = pl.program_id(0)
    gates = gate_ref[...]
    f = gates[:, 0:1] * _load_rows(y_refs[0], tt)
    for kk in range(1, TOP_K):
        f = f + gates[:, kk:kk + 1] * _load_rows(y_refs[kk], tt)
    out = _layer_norm(DEEPNORM_ALPHA * _load_rows(h_ref, tt) + f, g2_ref[...], b2_ref[...])

    @pl.when(i < n_first)
    def _():
        out_first[...] = out

    @pl.when(i >= n_first)
    def _():
        out_second[...] = out


def _combine(y_rows, gates, h_all, ln2_g, ln2_b, tt, n_tokens_first):
    T = gates.shape[0]
    n_tiles = T // tt
    n_first = n_tokens_first // tt
    assert T % tt == 0 and n_tokens_first % tt == 0 and 0 < n_first < n_tiles
    row_blk = lambda imap: pl.BlockSpec((tt * SUBLANES, LANES), imap)
    y_specs = [row_blk(functools.partial(lambda i, kk: (kk * n_tiles + i, 0), kk=kk)) for kk in range(TOP_K)]
    return pl.pallas_call(
        functools.partial(_combine_body, tt=tt, n_first=n_first),
        grid=(n_tiles,),
        in_specs=y_specs + [
            row_blk(lambda i: (i, 0)),
            pl.BlockSpec((tt, TOP_K), lambda i: (i, 0)),
            pl.BlockSpec((1, D_MODEL), lambda i: (0, 0)),
            pl.BlockSpec((1, D_MODEL), lambda i: (0, 0)),
        ],
        out_specs=[pl.BlockSpec((tt, D_MODEL), lambda i: (jnp.minimum(i, n_first - 1), 0)),
                   pl.BlockSpec((tt, D_MODEL), lambda i: (jnp.maximum(i - n_first, 0), 0))],
        out_shape=[jax.ShapeDtypeStruct((n_tokens_first, D_MODEL), F32),
                   jax.ShapeDtypeStruct((T - n_tokens_first, D_MODEL), F32)],
        compiler_params=pltpu.CompilerParams(
            dimension_semantics=("arbitrary",), vmem_limit_bytes=VMEM_LIMIT_BYTES),
        name="combine",
    )(*([y_rows] * TOP_K), h_all, gates, ln2_g, ln2_b)


def kernel(x_prompt, x_sample, state_retention, state_conv, w_in, beta_ret, beta_conv, conv_w, conv_b,
           w_out, ln1_g, ln1_b, w_router, b_router, w_gu, b_gu, w_down, b_down, ln2_g, ln2_b):
    assert w_in.shape[0] == DEPTH == 1
    B, S, _ = x_prompt.shape
    Bd, Ld, _ = x_sample.shape
    Tp, Ts = B * S, Bd * Ld
    T = Tp + Ts
    wr_t = w_router[0].T
    wr_hi = wr_t.astype(BF16)
    wr_lo = (wr_t - wr_hi.astype(F32)).astype(BF16)
    w = dict(w_in=w_in[0].astype(BF16), w_out=w_out[0].astype(BF16), conv_w=conv_w[0],
             conv_b=conv_b[0][None], beta_ret=beta_ret[0][None], beta_conv=beta_conv[0][None],
             ln1_g=ln1_g[0][None], ln1_b=ln1_b[0][None], w_router=jnp.stack([wr_hi, wr_lo]),
             b_router=b_router[0][:, None])

    s_ret0 = jnp.zeros((B, N_RET_HEADS, HEAD_DIM, HEAD_DIM), F32)
    s_conv0 = jnp.zeros((B, CONV_W - 1, D_CONV), F32)
    h_s, choices_s, gates_s, sret_s, sconv_s = _mixer(
        x_sample, state_retention[0], state_conv[0], PAST_LEN, SAMPLE_BATCH_BLOCK, Ld, w)
    h_all, choices_t, gates_t, sret_p, sconv_p = _mixer(
        x_prompt, s_ret0, s_conv0, 0, 1, PROMPT_CHUNK, w, tail=(h_s, choices_s, gates_s))

    tok, dst, block_e, n_valid = _routing_plan(choices_t, EXPERT_BLOCK)
    y_rows = _experts(h_all, tok, dst, block_e, n_valid, w_gu[0], b_gu[0], w_down[0], b_down[0],
                      EXPERT_BLOCK, T * TOP_K)
    y_p, y_s = _combine(y_rows, gates_t.T, h_all, ln2_g[0][None], ln2_b[0][None], COMBINE_TILE, Tp)

    return (y_p.reshape(B, S, D_MODEL), y_s.reshape(Bd, Ld, D_MODEL),
            sret_p[None], sconv_p[None], sret_s[None], sconv_s[None])
```

```python
import functools

import jax
import jax.numpy as jnp
from jax import lax
from jax.experimental import pallas as pl
from jax.experimental.pallas import tpu as pltpu

D_MODEL = 1024
N_RET_HEADS = 4
HEAD_DIM = 128
D_RET = N_RET_HEADS * HEAD_DIM
D_CONV = D_MODEL - D_RET
CONV_W = 3
N_EXPERTS = 32
TOP_K = 4
D_FF = D_MODEL
SWIGLU_LIMIT = 7.0
SWIGLU_ALPHA = 1.702
ROPE_BASE = 10000.0
LN_EPS = 1e-5
GN_EPS = 1e-5
DEPTH = 1
PAST_LEN = 1024
DEEPNORM_ALPHA = (2.0 * DEPTH) ** 0.25

LANES = 128
SUBLANES = 8
ROW_CHUNKS = D_MODEL // LANES
assert ROW_CHUNKS == SUBLANES

PROMPT_CHUNK = 256
SAMPLE_BATCH_BLOCK = 8
EXPERT_BLOCK = 256
COMBINE_TILE = 256
DMA_UNROLL = 8
N_CHUNKS = 4
PAD_BIT = 18
UNUSED_KEY = 1 << 30
VMEM_LIMIT_BYTES = 56 * 1024 * 1024

F32 = jnp.float32
BF16 = jnp.bfloat16


def _layer_norm(x, g, b):
    mu = jnp.mean(x, axis=-1, keepdims=True)
    xc = x - mu
    var = jnp.mean(xc * xc, axis=-1, keepdims=True)
    return xc * lax.rsqrt(var + LN_EPS) * g + b


def _load_rows(ref, n_rows):
    return jnp.concatenate(
        [ref[pl.ds(j, n_rows, stride=SUBLANES), :] for j in range(ROW_CHUNKS)], axis=1)


def _store_rows(ref, val, n_rows):
    for j in range(ROW_CHUNKS):
        ref[pl.ds(j, n_rows, stride=SUBLANES), :] = val[:, j * LANES:(j + 1) * LANES]


N_MIXER_INPUTS = 19
N_TOKEN_OUTPUTS = 3


def _mixer_body(*refs, nb, C, n_batch_steps, n_tail):
    if not n_tail:
        _mixer_compute(*refs, nb=nb, C=C)
        return
    tails = refs[N_MIXER_INPUTS:N_MIXER_INPUTS + N_TOKEN_OUTPUTS]
    outs = refs[N_MIXER_INPUTS + N_TOKEN_OUTPUTS:N_MIXER_INPUTS + 2 * N_TOKEN_OUTPUTS]
    bstep = pl.program_id(0)

    @pl.when(bstep < n_batch_steps)
    def _():
        _mixer_compute(*refs[:N_MIXER_INPUTS], *refs[N_MIXER_INPUTS + N_TOKEN_OUTPUTS:], nb=nb, C=C)

    @pl.when((bstep == n_batch_steps) & (pl.program_id(1) < n_tail))
    def _():
        for tail_ref, out_ref in zip(tails, outs):
            out_ref[...] = tail_ref[...]


def _route(logits_t):
    expert = lax.broadcasted_iota(jnp.int32, logits_t.shape, 0)
    vals, idxs = [], []
    for _ in range(TOP_K):
        m = jnp.max(logits_t, axis=0, keepdims=True)
        idx = jnp.min(jnp.where(logits_t == m, expert, N_EXPERTS), axis=0, keepdims=True)
        vals.append(m)
        idxs.append(idx)
        logits_t = jnp.where(expert == idx, -jnp.inf, logits_t)
    ex = [jnp.exp(v - vals[0]) for v in vals]
    total = ex[0]
    for e in ex[1:]:
        total = total + e
    return jnp.concatenate(idxs, axis=0), jnp.concatenate([e / total for e in ex], axis=0)


def _mixer_compute(x_ref, cos_ref, sin_ref, decay_ref, qdec_ref, kdec_ref, sdec_ref, win_ref, wout_ref,
                   convw_ref, convb_ref, bret_ref, bconv_ref, g1_ref, b1_ref, wr_ref, br_ref,
                   sret_ref, sconv_ref, h_ref, choice_ref, gate_ref, sret_out, sconv_out, mix_ref,
                   *, nb, C):
    c = pl.program_id(1)

    @pl.when(c == 0)
    def _():
        sret_out[...] = sret_ref[...]
        sconv_out[...] = sconv_ref[...]

    x = x_ref[...].reshape(nb * C, D_MODEL)
    xb = x.astype(BF16)

    def proj(col0, width):
        return jnp.dot(xb, win_ref[:, col0:col0 + width], preferred_element_type=F32)

    q = proj(0, D_RET)
    k = proj(D_RET, D_RET)
    v = proj(2 * D_RET, D_RET)
    g = proj(3 * D_RET, D_RET)
    bg = proj(4 * D_RET, D_CONV)
    cg = proj(4 * D_RET + D_CONV, D_CONV)
    hc = proj(4 * D_RET + 2 * D_CONV, D_CONV)

    cos = cos_ref[...]
    sin = sin_ref[...]
    row = lax.broadcasted_iota(jnp.int32, (C, D_CONV), 0)
    k_scale = HEAD_DIM ** -0.5

    for b in range(nb):
        r0 = b * C
        for h in range(N_RET_HEADS):
            c0 = h * HEAD_DIM
            qh = q[r0:r0 + C, c0:c0 + HEAD_DIM]
            kh = k[r0:r0 + C, c0:c0 + HEAD_DIM]
            vh = v[r0:r0 + C, c0:c0 + HEAD_DIM]
            qh = qh * cos + pltpu.roll(qh, HEAD_DIM // 2, axis=1) * sin
            kh = (kh * cos + pltpu.roll(kh, HEAD_DIM // 2, axis=1) * sin) * k_scale
            qb = qh.astype(BF16)
            kb = kh.astype(BF16)
            vb = vh.astype(BF16)
            s_old = sret_out[b, h]
            scores = lax.dot_general(qb, kb, (((1,), (1,)), ((), ())), preferred_element_type=F32)
            scores = scores * decay_ref[h]
            intra = jnp.dot(scores.astype(BF16), vb, preferred_element_type=F32)
            cross = jnp.dot(qb, s_old.astype(BF16), preferred_element_type=F32) * qdec_ref[h]
            o = intra + cross
            kd = (kh * kdec_ref[h]).astype(BF16)
            s_new = sdec_ref[h] * s_old + lax.dot_general(
                kd, vb, (((0,), (0,)), ((), ())), preferred_element_type=F32)
            sret_out[b, h] = s_new
            mu = jnp.mean(o, axis=-1, keepdims=True)
            oc = o - mu
            var = jnp.mean(oc * oc, axis=-1, keepdims=True)
            on = oc * lax.rsqrt(var + GN_EPS)
            gh = g[r0:r0 + C, c0:c0 + HEAD_DIM]
            ret = on * (gh * jax.nn.sigmoid(gh)) * bret_ref[:, c0:c0 + HEAD_DIM]
            mix_ref[r0:r0 + C, c0:c0 + HEAD_DIM] = ret.astype(BF16)

        u = cg[r0:r0 + C] * hc[r0:r0 + C]
        prev = sconv_out[b]
        u1 = jnp.where(row == 0, prev[1:2], pltpu.roll(u, 1, axis=0))
        u2 = jnp.where(row == 0, prev[0:1], jnp.where(row == 1, prev[1:2], pltpu.roll(u, 2, axis=0)))
        z = convb_ref[...] + convw_ref[0:1] * u2 + convw_ref[1:2] * u1 + convw_ref[2:3] * u
        conv_out = bg[r0:r0 + C] * z * bconv_ref[...]
        mix_ref[r0:r0 + C, D_RET:D_RET + D_CONV] = conv_out.astype(BF16)
        sconv_out[b] = u[C - 2:C]

    m = jnp.dot(mix_ref[...], wout_ref[...], preferred_element_type=F32)
    hval = _layer_norm(DEEPNORM_ALPHA * x + m, g1_ref[...], b1_ref[...])
    _store_rows(h_ref, hval, nb * C)
    h_hi = hval.astype(BF16)
    h_lo = (hval - h_hi.astype(F32)).astype(BF16)
    nt_dot = lambda a, b: lax.dot_general(a, b, (((1,), (1,)), ((), ())), preferred_element_type=F32)
    logits_t = nt_dot(wr_ref[0], h_hi) + nt_dot(wr_ref[0], h_lo) + nt_dot(wr_ref[1], h_hi) + br_ref[...]
    choice_ref[...], gate_ref[...] = _route(logits_t)


def _retention_tables(C, pos0, L):
    lg = jnp.log1p(-jnp.power(2.0, -5.0 - jnp.arange(N_RET_HEADS, dtype=F32)))
    i = jnp.arange(C, dtype=F32)
    diff = i[:, None] - i[None, :]
    decay = jnp.where(diff[None] >= 0, jnp.exp(lg[:, None, None] * jnp.maximum(diff, 0.0)[None]), 0.0)
    qdec = jnp.exp(lg[:, None] * (i + 1.0)[None, :])
    kdec = jnp.exp(lg[:, None] * (C - 1.0 - i)[None, :])
    sdec = jnp.exp(lg * C)
    qdec = jnp.broadcast_to(qdec[:, :, None], (N_RET_HEADS, C, HEAD_DIM))
    kdec = jnp.broadcast_to(kdec[:, :, None], (N_RET_HEADS, C, HEAD_DIM))
    sdec = jnp.broadcast_to(sdec[:, None, None], (N_RET_HEADS, 1, HEAD_DIM))
    pos = pos0 + jnp.arange(L, dtype=jnp.int32)
    inv_freq = ROPE_BASE ** (-jnp.arange(0, HEAD_DIM, 2, dtype=F32) / HEAD_DIM)
    ang = pos.astype(F32)[:, None] * inv_freq[None, :]
    cos = jnp.cos(ang)
    sin = jnp.sin(ang)
    cos_full = jnp.concatenate([cos, cos], axis=-1)
    sin_signed = jnp.concatenate([-sin, sin], axis=-1)
    return cos_full, sin_signed, decay, qdec, kdec, sdec


def _mixer(x, s_ret, s_conv, pos0, nb, C, w, tail=None):
    B, L, _ = x.shape
    nc = L // C
    nbs = B // nb
    rows = nb * C
    n_tokens = B * L
    n_tail = 0
    if tail is not None:
        n_tokens += tail[1].shape[1]
        n_tail = tail[1].shape[1] // rows
        assert tail[1].shape[1] % rows == 0 and 0 < n_tail <= nc
    cos, sin, decay, qdec, kdec, sdec = _retention_tables(C, pos0, L)

    bb = lambda b: jnp.minimum(b, nbs - 1)
    cc = lambda b, c: jnp.where(b < nbs, c, nc - 1)
    out_blk = lambda b, c: jnp.where(b < nbs, b * nc + c, nbs * nc + jnp.minimum(c, n_tail - 1))
    const2 = lambda b, c: (0, 0)
    const3 = lambda b, c: (0, 0, 0)
    full = lambda a: pl.BlockSpec(a.shape, const2 if a.ndim == 2 else const3)
    in_arrays = [x, cos, sin, decay, qdec, kdec, sdec, w["w_in"], w["w_out"], w["conv_w"], w["conv_b"],
                 w["beta_ret"], w["beta_conv"], w["ln1_g"], w["ln1_b"], w["w_router"], w["b_router"],
                 s_ret, s_conv]
    assert len(in_arrays) == N_MIXER_INPUTS
    in_specs = [pl.BlockSpec((nb, C, D_MODEL), lambda b, c: (bb(b), cc(b, c), 0)),
                pl.BlockSpec((C, HEAD_DIM), lambda b, c: (cc(b, c), 0)),
                pl.BlockSpec((C, HEAD_DIM), lambda b, c: (cc(b, c), 0))]
    in_specs += [full(a) for a in in_arrays[3:17]]
    in_specs += [pl.BlockSpec((nb, N_RET_HEADS, HEAD_DIM, HEAD_DIM), lambda b, c: (bb(b), 0, 0, 0)),
                 pl.BlockSpec((nb, CONV_W - 1, D_CONV), lambda b, c: (bb(b), 0, 0))]
    if n_tail:
        tail_blk = lambda b, c: jnp.where(b < nbs, 0, jnp.minimum(c, n_tail - 1))
        in_arrays += list(tail)
        in_specs += [pl.BlockSpec((rows * SUBLANES, LANES), lambda b, c: (tail_blk(b, c), 0)),
                     pl.BlockSpec((TOP_K, rows), lambda b, c: (0, tail_blk(b, c))),
                     pl.BlockSpec((TOP_K, rows), lambda b, c: (0, tail_blk(b, c)))]
    out_shape = [jax.ShapeDtypeStruct((n_tokens * SUBLANES, LANES), F32),
                 jax.ShapeDtypeStruct((TOP_K, n_tokens), jnp.int32),
                 jax.ShapeDtypeStruct((TOP_K, n_tokens), F32),
                 jax.ShapeDtypeStruct((B, N_RET_HEADS, HEAD_DIM, HEAD_DIM), F32),
                 jax.ShapeDtypeStruct((B, CONV_W - 1, D_CONV), F32)]
    assert len(out_shape) == N_TOKEN_OUTPUTS + 2
    out_specs = [pl.BlockSpec((rows * SUBLANES, LANES), lambda b, c: (out_blk(b, c), 0)),
                 pl.BlockSpec((TOP_K, rows), lambda b, c: (0, out_blk(b, c))),
                 pl.BlockSpec((TOP_K, rows), lambda b, c: (0, out_blk(b, c))),
                 pl.BlockSpec((nb, N_RET_HEADS, HEAD_DIM, HEAD_DIM), lambda b, c: (bb(b), 0, 0, 0)),
                 pl.BlockSpec((nb, CONV_W - 1, D_CONV), lambda b, c: (bb(b), 0, 0))]
    return pl.pallas_call(
        functools.partial(_mixer_body, nb=nb, C=C, n_batch_steps=nbs, n_tail=n_tail),
        grid=(nbs + (1 if n_tail else 0), nc), in_specs=in_specs, out_specs=out_specs, out_shape=out_shape,
        scratch_shapes=[pltpu.VMEM((rows, D_MODEL), BF16)],
        compiler_params=pltpu.CompilerParams(
            dimension_semantics=("arbitrary", "arbitrary"), vmem_limit_bytes=VMEM_LIMIT_BYTES),
        name="mixer",
    )(*in_arrays)


def _routing_plan(choices_t, blk):
    T = choices_t.shape[1]
    A = T * TOP_K
    assert A < (1 << PAD_BIT) and N_EXPERTS * blk < (1 << PAD_BIT) and A % blk == 0
    assign_id = (jnp.arange(T, dtype=jnp.int32)[None, :] * TOP_K
                 + jnp.arange(TOP_K, dtype=jnp.int32)[:, None]).reshape(A)
    flat_e = choices_t.reshape(A)
    experts = jnp.arange(N_EXPERTS, dtype=jnp.int32)
    counts = jnp.sum((flat_e[:, None] == experts[None, :]).astype(jnp.int32), axis=0)
    nblk_e = (counts + blk - 1) // blk
    n_pad = nblk_e * blk - counts
    real_keys = (flat_e << (PAD_BIT + 1)) | assign_id
    j = jnp.arange(blk, dtype=jnp.int32)[None, :]
    pad_keys = jnp.where(j < n_pad[:, None],
                         (experts[:, None] << (PAD_BIT + 1)) | (1 << PAD_BIT) | (experts[:, None] * blk + j),
                         UNUSED_KEY)
    slots = jnp.sort(jnp.concatenate([real_keys, pad_keys.reshape(-1)]))
    n_blocks = A // blk + N_EXPERTS
    ident = slots & ((1 << PAD_BIT) - 1)
    is_real = (((slots >> PAD_BIT) & 1) == 0) & (slots < UNUSED_KEY)
    tok = jnp.where(is_real, ident >> 2, 0)
    dst = jnp.where(is_real, (ident & (TOP_K - 1)) * T + (ident >> 2),
                    jnp.where(slots < UNUSED_KEY, A + ident, A))
    n_valid = jnp.sum(nblk_e)
    bi = jnp.arange(n_blocks, dtype=jnp.int32)
    block_e = slots.reshape(n_blocks, blk)[:, 0] >> (PAD_BIT + 1)
    block_e = jnp.where(bi < n_valid, block_e, block_e[n_valid - 1])
    return tok.reshape(n_blocks, blk), dst.reshape(n_blocks, blk), block_e, n_valid.reshape(1).astype(jnp.int32)


def _expert_body(be_ref, nv_ref, tok_ref, tokn_ref, dst_ref, h_hbm, wgu_ref, bgu_ref, wd_ref, bd_ref,
                 y_hbm, xbuf, ybuf, isem, osem, wgu_bf, wd_bf, xb_ref, act_ref, *, blk, n_real_rows):
    i = pl.program_id(0)
    n_valid = nv_ref[0]

    def row_tile(r):
        if isinstance(r, int):
            return pl.ds(r * SUBLANES, SUBLANES)
        return pl.ds(pl.multiple_of(r * SUBLANES, SUBLANES), SUBLANES)

    def gather_row(rows_ref, s, r, prio=0):
        pltpu.make_async_copy(
            h_hbm.at[row_tile(rows_ref[0, 0, r])], xbuf.at[s, row_tile(r)], isem.at[s]).start(prio)

    def scatter_row(s, r, prio=0):
        pltpu.make_async_copy(
            ybuf.at[s, row_tile(r)], y_hbm.at[row_tile(dst_ref[0, 0, r])], osem.at[s]).start(prio)

    def rolled_rows(start_row):
        def body(gi, carry):
            for u in range(DMA_UNROLL):
                start_row(gi * DMA_UNROLL + u)
            return carry
        lax.fori_loop(0, blk // DMA_UNROLL, body, 0)

    def wait_gather(s):
        pltpu.make_async_copy(h_hbm.at[pl.ds(0, blk * SUBLANES)], xbuf.at[s], isem.at[s]).wait()

    def wait_scatter(s):
        pltpu.make_async_copy(ybuf.at[s], y_hbm.at[pl.ds(0, blk * SUBLANES)], osem.at[s]).wait()

    @pl.when(i == 0)
    def _():
        rolled_rows(lambda r: gather_row(tok_ref, 0, r))
        ybuf[...] = jnp.zeros(ybuf.shape, F32)
        for e in range(N_EXPERTS):
            pltpu.make_async_copy(
                ybuf.at[0], y_hbm.at[pl.ds((n_real_rows + e * blk) * SUBLANES, blk * SUBLANES)],
                osem.at[0]).start()
        for e in range(N_EXPERTS):
            wait_scatter(0)

    @pl.when((i < n_valid) & ((i == 0) | (be_ref[i] != be_ref[jnp.maximum(i - 1, 0)])))
    def _():
        wgu_bf[...] = wgu_ref[0].astype(BF16)
        wd_bf[...] = wd_ref[0].astype(BF16)

    rows_per_chunk = blk // N_CHUNKS
    cols1 = D_FF // N_CHUNKS
    cols2 = D_MODEL // N_CHUNKS
    chunk_region = pl.when(be_ref[i] >= 0)

    def block_step(s):
        @pl.when(i < n_valid)
        def _():
            wait_gather(s)
            xb_ref[...] = _load_rows(xbuf.at[s], blk).astype(BF16)

            for c in range(N_CHUNKS):
                @chunk_region
                def _():
                    lo = c * cols1
                    xb = xb_ref[...]
                    gate = jnp.dot(xb, wgu_bf[:, lo:lo + cols1], preferred_element_type=F32)
                    up = jnp.dot(xb, wgu_bf[:, D_FF + lo:D_FF + lo + cols1], preferred_element_type=F32)
                    gate = jnp.minimum(gate + bgu_ref[0, :, lo:lo + cols1], SWIGLU_LIMIT)
                    up = jnp.clip(up + bgu_ref[0, :, D_FF + lo:D_FF + lo + cols1], -SWIGLU_LIMIT, SWIGLU_LIMIT)
                    act = (up + 1.0) * gate * jax.nn.sigmoid(SWIGLU_ALPHA * gate)
                    act_ref[:, lo:lo + cols1] = act.astype(BF16)
                    for r in range(c * rows_per_chunk, (c + 1) * rows_per_chunk):
                        gather_row(tokn_ref, 1 - s, r, r % 2)

            @pl.when(i >= 1)
            def _():
                wait_scatter(s)

            for c in range(N_CHUNKS):
                @chunk_region
                def _():
                    lo = c * cols2
                    y = jnp.dot(act_ref[...], wd_bf[:, lo:lo + cols2], preferred_element_type=F32)
                    y = y + bd_ref[0, :, lo:lo + cols2]
                    for j in range(cols2 // LANES):
                        ybuf[s, pl.ds(lo // LANES + j, blk, stride=SUBLANES), :] = y[:, j * LANES:(j + 1) * LANES]
                    for r in range(c * rows_per_chunk, (c + 1) * rows_per_chunk):
                        scatter_row(1 - s, r, r % 2)

        @pl.when(i == n_valid)
        def _():
            wait_gather(s)
            wait_scatter(s)
            rolled_rows(lambda r: scatter_row(1 - s, r))
            wait_scatter(1 - s)

    for s in range(2):
        pl.when(i % 2 == s)(functools.partial(block_step, s))


def _experts(h_all, tok, dst, block_e, n_valid, w_gu, b_gu, w_down, b_down, blk, n_real_rows):
    n_blocks = block_e.shape[0]
    assert n_blocks * blk == n_real_rows + N_EXPERTS * blk and blk % (2 * N_CHUNKS) == 0
    n_out_rows = n_real_rows + (N_EXPERTS + 1) * blk
    tok3 = tok.reshape(n_blocks, 1, blk)
    priming = n_real_rows + N_EXPERTS * blk + jnp.arange(blk, dtype=jnp.int32)
    dst3 = jnp.concatenate([priming[None], dst], axis=0).reshape(n_blocks + 1, 1, blk)
    smem_blk = lambda imap: pl.BlockSpec((1, 1, blk), imap, memory_space=pltpu.SMEM)
    grid_spec = pltpu.PrefetchScalarGridSpec(
        num_scalar_prefetch=2,
        grid=(n_blocks,),
        in_specs=[
            smem_blk(lambda i, be, nv: (i, 0, 0)),
            smem_blk(lambda i, be, nv: (jnp.minimum(i + 1, nv[0] - 1), 0, 0)),
            smem_blk(lambda i, be, nv: (i, 0, 0)),
            pl.BlockSpec(memory_space=pl.ANY),
            pl.BlockSpec((1, D_MODEL, 2 * D_FF), lambda i, be, nv: (be[i], 0, 0)),
            pl.BlockSpec((1, 1, 2 * D_FF), lambda i, be, nv: (be[i], 0, 0)),
            pl.BlockSpec((1, D_FF, D_MODEL), lambda i, be, nv: (be[i], 0, 0)),
            pl.BlockSpec((1, 1, D_MODEL), lambda i, be, nv: (be[i], 0, 0)),
        ],
        out_specs=pl.BlockSpec(memory_space=pl.ANY),
        scratch_shapes=[pltpu.VMEM((2, blk * SUBLANES, LANES), F32),
                        pltpu.VMEM((2, blk * SUBLANES, LANES), F32),
                        pltpu.SemaphoreType.DMA((2,)),
                        pltpu.SemaphoreType.DMA((2,)),
                        pltpu.VMEM((D_MODEL, 2 * D_FF), BF16),
                        pltpu.VMEM((D_FF, D_MODEL), BF16),
                        pltpu.VMEM((blk, D_MODEL), BF16),
                        pltpu.VMEM((blk, D_FF), BF16)],
    )
    return pl.pallas_call(
        functools.partial(_expert_body, blk=blk, n_real_rows=n_real_rows),
        grid_spec=grid_spec,
        out_shape=jax.ShapeDtypeStruct((n_out_rows * SUBLANES, LANES), F32),
        compiler_params=pltpu.CompilerParams(
            dimension_semantics=("arbitrary",), vmem_limit_bytes=VMEM_LIMIT_BYTES),
        name="experts",
    )(block_e, n_valid, tok3, tok3, dst3, h_all, w_gu, b_gu.reshape(N_EXPERTS, 1, 2 * D_FF),
      w_down, b_down.reshape(N_EXPERTS, 1, D_MODEL))


def _combine_body(*refs, tt, n_first):
    y_refs = refs[:TOP_K]
    h_ref, gate_ref, g2_ref, b2_ref, out_first, out_second = refs[TOP_K:]
    i = pl.program_id(0)
    gates = gate_ref[...]
    f = gates[:, 0:1] * _load_rows(y_refs[0], tt)
    for kk in range(1, TOP_K):
        f = f + gates[:, kk:kk + 1] * _load_rows(y_refs[kk], tt)
    out = _layer_norm(DEEPNORM_ALPHA * _load_rows(h_ref, tt) + f, g2_ref[...], b2_ref[...])

    @pl.when(i < n_first)
    def _():
        out_first[...] = out

    @pl.when(i >= n_first)
    def _():
        out_second[...] = out


def _combine(y_rows, gates, h_all, ln2_g, ln2_b, tt, n_tokens_first):
    T = gates.shape[0]
    n_tiles = T // tt
    n_first = n_tokens_first // tt
    assert T % tt == 0 and n_tokens_first % tt == 0 and 0 < n_first < n_tiles
    row_blk = lambda imap: pl.BlockSpec((tt * SUBLANES, LANES), imap)
    y_specs = [row_blk(functools.partial(lambda i, kk: (kk * n_tiles + i, 0), kk=kk)) for kk in range(TOP_K)]
    return pl.pallas_call(
        functools.partial(_combine_body, tt=tt, n_first=n_first),
        grid=(n_tiles,),
        in_specs=y_specs + [
            row_blk(lambda i: (i, 0)),
            pl.BlockSpec((tt, TOP_K), lambda i: (i, 0)),
            pl.BlockSpec((1, D_MODEL), lambda i: (0, 0)),
            pl.BlockSpec((1, D_MODEL), lambda i: (0, 0)),
        ],
        out_specs=[pl.BlockSpec((tt, D_MODEL), lambda i: (jnp.minimum(i, n_first - 1), 0)),
                   pl.BlockSpec((tt, D_MODEL), lambda i: (jnp.maximum(i - n_first, 0), 0))],
        out_shape=[jax.ShapeDtypeStruct((n_tokens_first, D_MODEL), F32),
                   jax.ShapeDtypeStruct((T - n_tokens_first, D_MODEL), F32)],
        compiler_params=pltpu.CompilerParams(
            dimension_semantics=("arbitrary",), vmem_limit_bytes=VMEM_LIMIT_BYTES),
        name="combine",
    )(*([y_rows] * TOP_K), h_all, gates, ln2_g, ln2_b)


def kernel(x_prompt, x_sample, state_retention, state_conv, w_in, beta_ret, beta_conv, conv_w, conv_b,
           w_out, ln1_g, ln1_b, w_router, b_router, w_gu, b_gu, w_down, b_down, ln2_g, ln2_b):
    assert w_in.shape[0] == DEPTH == 1
    B, S, _ = x_prompt.shape
    Bd, Ld, _ = x_sample.shape
    Tp, Ts = B * S, Bd * Ld
    T = Tp + Ts
    wr_t = w_router[0].T
    wr_hi = wr_t.astype(BF16)
    wr_lo = (wr_t - wr_hi.astype(F32)).astype(BF16)
    w = dict(w_in=w_in[0].astype(BF16), w_out=w_out[0].astype(BF16), conv_w=conv_w[0],
             conv_b=conv_b[0][None], beta_ret=beta_ret[0][None], beta_conv=beta_conv[0][None],
             ln1_g=ln1_g[0][None], ln1_b=ln1_b[0][None], w_router=jnp.stack([wr_hi, wr_lo]),
             b_router=b_router[0][:, None])

    s_ret0 = jnp.zeros((B, N_RET_HEADS, HEAD_DIM, HEAD_DIM), F32)
    s_conv0 = jnp.zeros((B, CONV_W - 1, D_CONV), F32)
    h_s, choices_s, gates_s, sret_s, sconv_s = _mixer(
        x_sample, state_retention[0], state_conv[0], PAST_LEN, SAMPLE_BATCH_BLOCK, Ld, w)
    h_all, choices_t, gates_t, sret_p, sconv_p = _mixer(
        x_prompt, s_ret0, s_conv0, 0, 1, PROMPT_CHUNK, w, tail=(h_s, choices_s, gates_s))

    tok, dst, block_e, n_valid = _routing_plan(choices_t, EXPERT_BLOCK)
    y_rows = _experts(h_all, tok, dst, block_e, n_valid, w_gu[0], b_gu[0], w_down[0], b_down[0],
                      EXPERT_BLOCK, T * TOP_K)
    y_p, y_s = _combine(y_rows, gates_t.T, h_all, ln2_g[0][None], ln2_b[0][None], COMBINE_TILE, Tp)

    return (y_p.reshape(B, S, D_MODEL), y_s.reshape(Bd, Ld, D_MODEL),
            sret_p[None], sconv_p[None], sret_s[None], sconv_s[None])
```

```python
import functools

import jax
import jax.numpy as jnp
from jax import lax
from jax.experimental import pallas as pl
from jax.experimental.pallas import tpu as pltpu

D_MODEL = 1024
N_RET_HEADS = 4
HEAD_DIM = 128
D_RET = N_RET_HEADS * HEAD_DIM
D_CONV = D_MODEL - D_RET
CONV_W = 3
N_EXPERTS = 32
TOP_K = 4
D_FF = D_MODEL
SWIGLU_LIMIT = 7.0
SWIGLU_ALPHA = 1.702
ROPE_BASE = 10000.0
LN_EPS = 1e-5
GN_EPS = 1e-5
DEPTH = 1
PAST_LEN = 1024
DEEPNORM_ALPHA = (2.0 * DEPTH) ** 0.25

LANES = 128
SUBLANES = 8
ROW_CHUNKS = D_MODEL // LANES
assert ROW_CHUNKS == SUBLANES

PROMPT_CHUNK = 256
SAMPLE_BATCH_BLOCK = 8
EXPERT_BLOCK = 256
TOKEN_TILE = 256
DMA_UNROLL = 8
PAD_BIT = 18
UNUSED_KEY = 1 << 30
VMEM_LIMIT_BYTES = 56 * 1024 * 1024

F32 = jnp.float32
BF16 = jnp.bfloat16


def _layer_norm(x, g, b):
    mu = jnp.mean(x, axis=-1, keepdims=True)
    xc = x - mu
    var = jnp.mean(xc * xc, axis=-1, keepdims=True)
    return xc * lax.rsqrt(var + LN_EPS) * g + b


def _load_rows(ref, n_rows):
    return jnp.concatenate(
        [ref[pl.ds(j, n_rows, stride=SUBLANES), :] for j in range(ROW_CHUNKS)], axis=1)


def _store_rows(ref, val, n_rows):
    for j in range(ROW_CHUNKS):
        ref[pl.ds(j, n_rows, stride=SUBLANES), :] = val[:, j * LANES:(j + 1) * LANES]


def _row_tile(r):
    if isinstance(r, int):
        return pl.ds(r * SUBLANES, SUBLANES)
    return pl.ds(pl.multiple_of(r * SUBLANES, SUBLANES), SUBLANES)


N_MIXER_INPUTS = 19
N_TOKEN_OUTPUTS = 4


def _mixer_body(*refs, nb, C, n_batch_steps, n_tail):
    if not n_tail:
        _mixer_compute(*refs, nb=nb, C=C)
        return
    tails = refs[N_MIXER_INPUTS:N_MIXER_INPUTS + N_TOKEN_OUTPUTS]
    outs = refs[N_MIXER_INPUTS + N_TOKEN_OUTPUTS:N_MIXER_INPUTS + 2 * N_TOKEN_OUTPUTS]
    bstep = pl.program_id(0)

    @pl.when(bstep < n_batch_steps)
    def _():
        _mixer_compute(*refs[:N_MIXER_INPUTS], *refs[N_MIXER_INPUTS + N_TOKEN_OUTPUTS:], nb=nb, C=C)

    @pl.when((bstep == n_batch_steps) & (pl.program_id(1) < n_tail))
    def _():
        for tail_ref, out_ref in zip(tails, outs):
            out_ref[...] = tail_ref[...]


def _route(logits_t, run_ref):
    n_tok = logits_t.shape[1]
    expert = lax.broadcasted_iota(jnp.int32, logits_t.shape, 0)
    vals, idxs, hots = [], [], []
    for _ in range(TOP_K):
        m = jnp.max(logits_t, axis=0, keepdims=True)
        idx = jnp.min(jnp.where(logits_t == m, expert, N_EXPERTS), axis=0, keepdims=True)
        hot = expert == idx
        vals.append(m)
        idxs.append(idx)
        hots.append(hot)
        logits_t = jnp.where(hot, -jnp.inf, logits_t)
    ex = [jnp.exp(v - vals[0]) for v in vals]
    total = ex[0]
    for e in ex[1:]:
        total = total + e
    gates = jnp.concatenate([e / total for e in ex], axis=0)

    chosen = hots[0].astype(F32)
    for hot in hots[1:]:
        chosen = chosen + hot.astype(F32)
    chosen = chosen.astype(BF16)
    t_row = lax.broadcasted_iota(jnp.int32, (n_tok, n_tok), 0)
    t_col = lax.broadcasted_iota(jnp.int32, (n_tok, n_tok), 1)
    earlier = jnp.where(t_row < t_col, 1.0, 0.0).astype(BF16)
    before = run_ref[...] + jnp.dot(chosen, earlier, preferred_element_type=F32)
    ranks = [jnp.sum(jnp.where(hot, before, 0.0), axis=0, keepdims=True) for hot in hots]
    run_ref[...] = run_ref[...] + jnp.dot(chosen, jnp.ones((n_tok, n_tok), BF16), preferred_element_type=F32)
    return jnp.concatenate(idxs, axis=0), gates, jnp.concatenate(ranks, axis=0).astype(jnp.int32)


def _mixer_compute(x_ref, cos_ref, sin_ref, decay_ref, qdec_ref, kdec_ref, sdec_ref, win_ref, wout_ref,
                   convw_ref, convb_ref, bret_ref, bconv_ref, g1_ref, b1_ref, wr_ref, br_ref,
                   sret_ref, sconv_ref, h_ref, choice_ref, gate_ref, rank_ref, count_ref, sret_out, sconv_out,
                   mix_ref, run_ref, *, nb, C):
    c = pl.program_id(1)

    @pl.when((pl.program_id(0) == 0) & (c == 0))
    def _():
        run_ref[...] = jnp.zeros(run_ref.shape, F32)

    @pl.when(c == 0)
    def _():
        sret_out[...] = sret_ref[...]
        sconv_out[...] = sconv_ref[...]

    x = x_ref[...].reshape(nb * C, D_MODEL)
    xb = x.astype(BF16)

    def proj(col0, width):
        return jnp.dot(xb, win_ref[:, col0:col0 + width], preferred_element_type=F32)

    q = proj(0, D_RET)
    k = proj(D_RET, D_RET)
    v = proj(2 * D_RET, D_RET)
    g = proj(3 * D_RET, D_RET)
    bg = proj(4 * D_RET, D_CONV)
    cg = proj(4 * D_RET + D_CONV, D_CONV)
    hc = proj(4 * D_RET + 2 * D_CONV, D_CONV)

    cos = cos_ref[...]
    sin = sin_ref[...]
    row = lax.broadcasted_iota(jnp.int32, (C, D_CONV), 0)
    k_scale = HEAD_DIM ** -0.5

    for b in range(nb):
        r0 = b * C
        for h in range(N_RET_HEADS):
            c0 = h * HEAD_DIM
            qh = q[r0:r0 + C, c0:c0 + HEAD_DIM]
            kh = k[r0:r0 + C, c0:c0 + HEAD_DIM]
            vh = v[r0:r0 + C, c0:c0 + HEAD_DIM]
            qh = qh * cos + pltpu.roll(qh, HEAD_DIM // 2, axis=1) * sin
            kh = (kh * cos + pltpu.roll(kh, HEAD_DIM // 2, axis=1) * sin) * k_scale
            qb = qh.astype(BF16)
            kb = kh.astype(BF16)
            vb = vh.astype(BF16)
            s_old = sret_out[b, h]
            scores = lax.dot_general(qb, kb, (((1,), (1,)), ((), ())), preferred_element_type=F32)
            scores = scores * decay_ref[h]
            intra = jnp.dot(scores.astype(BF16), vb, preferred_element_type=F32)
            cross = jnp.dot(qb, s_old.astype(BF16), preferred_element_type=F32) * qdec_ref[h]
            o = intra + cross
            kd = (kh * kdec_ref[h]).astype(BF16)
            s_new = sdec_ref[h] * s_old + lax.dot_general(
                kd, vb, (((0,), (0,)), ((), ())), preferred_element_type=F32)
            sret_out[b, h] = s_new
            mu = jnp.mean(o, axis=-1, keepdims=True)
            oc = o - mu
            var = jnp.mean(oc * oc, axis=-1, keepdims=True)
            on = oc * lax.rsqrt(var + GN_EPS)
            gh = g[r0:r0 + C, c0:c0 + HEAD_DIM]
            ret = on * (gh * jax.nn.sigmoid(gh)) * bret_ref[:, c0:c0 + HEAD_DIM]
            mix_ref[r0:r0 + C, c0:c0 + HEAD_DIM] = ret.astype(BF16)

        u = cg[r0:r0 + C] * hc[r0:r0 + C]
        prev = sconv_out[b]
        u1 = jnp.where(row == 0, prev[1:2], pltpu.roll(u, 1, axis=0))
        u2 = jnp.where(row == 0, prev[0:1], jnp.where(row == 1, prev[1:2], pltpu.roll(u, 2, axis=0)))
        z = convb_ref[...] + convw_ref[0:1] * u2 + convw_ref[1:2] * u1 + convw_ref[2:3] * u
        conv_out = bg[r0:r0 + C] * z * bconv_ref[...]
        mix_ref[r0:r0 + C, D_RET:D_RET + D_CONV] = conv_out.astype(BF16)
        sconv_out[b] = u[C - 2:C]

    m = jnp.dot(mix_ref[...], wout_ref[...], preferred_element_type=F32)
    hval = _layer_norm(DEEPNORM_ALPHA * x + m, g1_ref[...], b1_ref[...])
    _store_rows(h_ref, hval, nb * C)
    h_hi = hval.astype(BF16)
    h_lo = (hval - h_hi.astype(F32)).astype(BF16)
    nt_dot = lambda a, b: lax.dot_general(a, b, (((1,), (1,)), ((), ())), preferred_element_type=F32)
    logits_t = nt_dot(wr_ref[0], h_hi) + nt_dot(wr_ref[0], h_lo) + nt_dot(wr_ref[1], h_hi) + br_ref[...]
    choice_ref[...], gate_ref[...], rank_ref[...] = _route(logits_t, run_ref)
    count_ref[...] = run_ref[:, :LANES]


def _retention_tables(C, pos0, L):
    lg = jnp.log1p(-jnp.power(2.0, -5.0 - jnp.arange(N_RET_HEADS, dtype=F32)))
    i = jnp.arange(C, dtype=F32)
    diff = i[:, None] - i[None, :]
    decay = jnp.where(diff[None] >= 0, jnp.exp(lg[:, None, None] * jnp.maximum(diff, 0.0)[None]), 0.0)
    qdec = jnp.exp(lg[:, None] * (i + 1.0)[None, :])
    kdec = jnp.exp(lg[:, None] * (C - 1.0 - i)[None, :])
    sdec = jnp.exp(lg * C)
    qdec = jnp.broadcast_to(qdec[:, :, None], (N_RET_HEADS, C, HEAD_DIM))
    kdec = jnp.broadcast_to(kdec[:, :, None], (N_RET_HEADS, C, HEAD_DIM))
    sdec = jnp.broadcast_to(sdec[:, None, None], (N_RET_HEADS, 1, HEAD_DIM))
    pos = pos0 + jnp.arange(L, dtype=jnp.int32)
    inv_freq = ROPE_BASE ** (-jnp.arange(0, HEAD_DIM, 2, dtype=F32) / HEAD_DIM)
    ang = pos.astype(F32)[:, None] * inv_freq[None, :]
    cos = jnp.cos(ang)
    sin = jnp.sin(ang)
    cos_full = jnp.concatenate([cos, cos], axis=-1)
    sin_signed = jnp.concatenate([-sin, sin], axis=-1)
    return cos_full, sin_signed, decay, qdec, kdec, sdec


def _mixer(x, s_ret, s_conv, pos0, nb, C, w, tail=None):
    B, L, _ = x.shape
    nc = L // C
    nbs = B // nb
    rows = nb * C
    n_tokens = B * L
    n_tail = 0
    if tail is not None:
        n_tokens += tail[1].shape[1]
        n_tail = tail[1].shape[1] // rows
        assert tail[1].shape[1] % rows == 0 and 0 < n_tail <= nc
    cos, sin, decay, qdec, kdec, sdec = _retention_tables(C, pos0, L)

    bb = lambda b: jnp.minimum(b, nbs - 1)
    cc = lambda b, c: jnp.where(b < nbs, c, nc - 1)
    out_blk = lambda b, c: jnp.where(b < nbs, b * nc + c, nbs * nc + jnp.minimum(c, n_tail - 1))
    const2 = lambda b, c: (0, 0)
    const3 = lambda b, c: (0, 0, 0)
    full = lambda a: pl.BlockSpec(a.shape, const2 if a.ndim == 2 else const3)
    per_token = lambda imap: pl.BlockSpec((TOP_K, rows), lambda b, c: (0, imap(b, c)))
    in_arrays = [x, cos, sin, decay, qdec, kdec, sdec, w["w_in"], w["w_out"], w["conv_w"], w["conv_b"],
                 w["beta_ret"], w["beta_conv"], w["ln1_g"], w["ln1_b"], w["w_router"], w["b_router"],
                 s_ret, s_conv]
    assert len(in_arrays) == N_MIXER_INPUTS
    in_specs = [pl.BlockSpec((nb, C, D_MODEL), lambda b, c: (bb(b), cc(b, c), 0)),
                pl.BlockSpec((C, HEAD_DIM), lambda b, c: (cc(b, c), 0)),
                pl.BlockSpec((C, HEAD_DIM), lambda b, c: (cc(b, c), 0))]
    in_specs += [full(a) for a in in_arrays[3:17]]
    in_specs += [pl.BlockSpec((nb, N_RET_HEADS, HEAD_DIM, HEAD_DIM), lambda b, c: (bb(b), 0, 0, 0)),
                 pl.BlockSpec((nb, CONV_W - 1, D_CONV), lambda b, c: (bb(b), 0, 0))]
    if n_tail:
        tail_blk = lambda b, c: jnp.where(b < nbs, 0, jnp.minimum(c, n_tail - 1))
        in_arrays += list(tail)
        in_specs += [pl.BlockSpec((rows * SUBLANES, LANES), lambda b, c: (tail_blk(b, c), 0))]
        in_specs += [per_token(tail_blk)] * (N_TOKEN_OUTPUTS - 1)
    out_shape = [jax.ShapeDtypeStruct((n_tokens * SUBLANES, LANES), F32),
                 jax.ShapeDtypeStruct((TOP_K, n_tokens), jnp.int32),
                 jax.ShapeDtypeStruct((TOP_K, n_tokens), F32),
                 jax.ShapeDtypeStruct((TOP_K, n_tokens), jnp.int32),
                 jax.ShapeDtypeStruct((N_EXPERTS, LANES), F32),
                 jax.ShapeDtypeStruct((B, N_RET_HEADS, HEAD_DIM, HEAD_DIM), F32),
                 jax.ShapeDtypeStruct((B, CONV_W - 1, D_CONV), F32)]
    assert len(out_shape) == N_TOKEN_OUTPUTS + 3
    out_specs = [pl.BlockSpec((rows * SUBLANES, LANES), lambda b, c: (out_blk(b, c), 0))]
    out_specs += [per_token(out_blk)] * (N_TOKEN_OUTPUTS - 1)
    out_specs += [pl.BlockSpec((N_EXPERTS, LANES), const2),
                  pl.BlockSpec((nb, N_RET_HEADS, HEAD_DIM, HEAD_DIM), lambda b, c: (bb(b), 0, 0, 0)),
                  pl.BlockSpec((nb, CONV_W - 1, D_CONV), lambda b, c: (bb(b), 0, 0))]
    return pl.pallas_call(
        functools.partial(_mixer_body, nb=nb, C=C, n_batch_steps=nbs, n_tail=n_tail),
        grid=(nbs + (1 if n_tail else 0), nc), in_specs=in_specs, out_specs=out_specs, out_shape=out_shape,
        scratch_shapes=[pltpu.VMEM((rows, D_MODEL), BF16), pltpu.VMEM((N_EXPERTS, rows), F32)],
        compiler_params=pltpu.CompilerParams(
            dimension_semantics=("arbitrary", "arbitrary"), vmem_limit_bytes=VMEM_LIMIT_BYTES),
        name="mixer",
    )(*in_arrays)


def _lookup(table, choices):
    experts = jnp.arange(N_EXPERTS, dtype=jnp.int32)
    return jnp.sum(jnp.where(choices[..., None] == experts, table, 0), axis=-1)


def _routing_plan(choices_t, ranks_t, counts_first, counts_second, n_first, blk):
    T = choices_t.shape[1]
    A = T * TOP_K
    assert A < (1 << PAD_BIT) and N_EXPERTS * blk < (1 << PAD_BIT) and A % blk == 0
    n_blocks = A // blk + N_EXPERTS
    experts = jnp.arange(N_EXPERTS, dtype=jnp.int32)
    counts = counts_first + counts_second
    nblk_e = (counts + blk - 1) // blk
    bend = jnp.cumsum(nblk_e)
    n_valid = bend[-1]
    start = (bend - nblk_e) * blk
    token = jnp.arange(T, dtype=jnp.int32)[None, :]
    rank = ranks_t + jnp.where(token >= n_first, _lookup(counts_first, choices_t), 0)
    pos = _lookup(start, choices_t) + rank

    assign_id = (token * TOP_K + jnp.arange(TOP_K, dtype=jnp.int32)[:, None]).reshape(A)
    real_keys = (choices_t.reshape(A) << (PAD_BIT + 1)) | assign_id
    n_pad = nblk_e * blk - counts
    j = jnp.arange(blk, dtype=jnp.int32)[None, :]
    pad_keys = jnp.where(j < n_pad[:, None],
                         (experts[:, None] << (PAD_BIT + 1)) | (1 << PAD_BIT) | (experts[:, None] * blk + j),
                         UNUSED_KEY)
    slots = jnp.sort(jnp.concatenate([real_keys, pad_keys.reshape(-1)]))
    ident = slots & ((1 << PAD_BIT) - 1)
    is_real = (((slots >> PAD_BIT) & 1) == 0) & (slots < UNUSED_KEY)
    dst = jnp.where(is_real, (ident & (TOP_K - 1)) * T + (ident >> 2),
                    jnp.where(slots < UNUSED_KEY, A + ident, A))
    bi = jnp.arange(n_blocks, dtype=jnp.int32)
    block_e = slots.reshape(n_blocks, blk)[:, 0] >> (PAD_BIT + 1)
    block_e = jnp.where(bi < n_valid, block_e, block_e[n_valid - 1])
    last_block = jnp.where(nblk_e > 0, bend - 1, -1)
    unused = n_valid + experts
    unused = jnp.where(unused < n_blocks, unused, -1)
    zero_blocks = jnp.concatenate([last_block, unused]).astype(jnp.int32)
    return pos, dst.reshape(n_blocks, blk), block_e, n_valid.reshape(1).astype(jnp.int32), zero_blocks


def _dispatch_body(zero_ref, pos_ref, h_ref, xs_hbm, hbuf, sem, *, tt, blk):
    i = pl.program_id(0)
    n = pl.num_programs(0)
    slot = i % 2

    def wait_rows(s):
        for _ in range(TOP_K):
            pltpu.make_async_copy(hbuf.at[s], xs_hbm.at[pl.ds(0, tt * SUBLANES)], sem.at[s]).wait()

    @pl.when(i == 0)
    def _():
        hbuf[1] = jnp.zeros(hbuf.shape[1:], F32)
        zeros = hbuf.at[1, pl.ds(0, blk * SUBLANES)]
        for e in range(zero_ref.shape[0]):
            @pl.when(zero_ref[e] >= 0)
            def _():
                first_row = pl.multiple_of(zero_ref[e] * (blk * SUBLANES), blk * SUBLANES)
                pltpu.make_async_copy(zeros, xs_hbm.at[pl.ds(first_row, blk * SUBLANES)], sem.at[1]).start()
        for e in range(zero_ref.shape[0]):
            @pl.when(zero_ref[e] >= 0)
            def _():
                pltpu.make_async_copy(zeros, xs_hbm.at[pl.ds(0, blk * SUBLANES)], sem.at[1]).wait()

    @pl.when(i >= 2)
    def _():
        wait_rows(slot)

    hbuf[slot] = h_ref[...]
    for kk in range(TOP_K):
        def body(gi, carry):
            for u in range(DMA_UNROLL):
                t = gi * DMA_UNROLL + u
                pltpu.make_async_copy(hbuf.at[slot, _row_tile(t)],
                                      xs_hbm.at[_row_tile(pos_ref[0, 0, kk * tt + t])],
                                      sem.at[slot]).start(u % 2)
            return carry
        lax.fori_loop(0, tt // DMA_UNROLL, body, 0)

    @pl.when(i == n - 1)
    def _():
        wait_rows(slot)

        @pl.when(i >= 1)
        def _():
            wait_rows(1 - slot)


def _dispatch(h_all, pos, zero_blocks, n_slots, tt, blk):
    T = pos.shape[1]
    n_tiles = T // tt
    assert T % tt == 0 and blk <= tt
    pos3 = pos.reshape(TOP_K, n_tiles, tt).transpose(1, 0, 2).reshape(n_tiles, 1, TOP_K * tt)
    grid_spec = pltpu.PrefetchScalarGridSpec(
        num_scalar_prefetch=1,
        grid=(n_tiles,),
        in_specs=[pl.BlockSpec((1, 1, TOP_K * tt), lambda i, zb: (i, 0, 0), memory_space=pltpu.SMEM),
                  pl.BlockSpec((tt * SUBLANES, LANES), lambda i, zb: (i, 0))],
        out_specs=pl.BlockSpec(memory_space=pl.ANY),
        scratch_shapes=[pltpu.VMEM((2, tt * SUBLANES, LANES), F32), pltpu.SemaphoreType.DMA((2,))],
    )
    return pl.pallas_call(
        functools.partial(_dispatch_body, tt=tt, blk=blk),
        grid_spec=grid_spec,
        out_shape=jax.ShapeDtypeStruct((n_slots * SUBLANES, LANES), F32),
        compiler_params=pltpu.CompilerParams(
            dimension_semantics=("arbitrary",), vmem_limit_bytes=VMEM_LIMIT_BYTES),
        name="dispatch",
    )(zero_blocks, pos3, h_all)


def _expert_body(be_ref, nv_ref, dst_ref, x_ref, wgu_ref, bgu_ref, wd_ref, bd_ref,
                 y_hbm, ybuf, osem, wgu_bf, wd_bf, *, blk, n_blocks, n_real_rows):
    i = pl.program_id(0)
    n_valid = nv_ref[0]

    def wait_scatter(s):
        pltpu.make_async_copy(ybuf.at[s], y_hbm.at[pl.ds(0, blk * SUBLANES)], osem.at[s]).wait()

    @pl.when(i == 0)
    def _():
        ybuf[0] = jnp.zeros(ybuf.shape[1:], F32)
        for e in range(N_EXPERTS):
            pltpu.make_async_copy(
                ybuf.at[0], y_hbm.at[pl.ds((n_real_rows + e * blk) * SUBLANES, blk * SUBLANES)],
                osem.at[0]).start()
        for e in range(N_EXPERTS):
            wait_scatter(0)

    @pl.when((i < n_valid) & ((i == 0) | (be_ref[i] != be_ref[jnp.maximum(i - 1, 0)])))
    def _():
        wgu_bf[...] = wgu_ref[0].astype(BF16)
        wd_bf[...] = wd_ref[0].astype(BF16)

    def block_step(s):
        @pl.when((i >= 2) & (i - 2 < n_valid))
        def _():
            wait_scatter(s)

        @pl.when(i < n_valid)
        def _():
            xb = _load_rows(x_ref, blk).astype(BF16)
            gu = jnp.dot(xb, wgu_bf[...], preferred_element_type=F32) + bgu_ref[0]
            gate = jnp.minimum(gu[:, :D_FF], SWIGLU_LIMIT)
            up = jnp.clip(gu[:, D_FF:], -SWIGLU_LIMIT, SWIGLU_LIMIT)
            act = (up + 1.0) * gate * jax.nn.sigmoid(SWIGLU_ALPHA * gate)
            y = jnp.dot(act.astype(BF16), wd_bf[...], preferred_element_type=F32) + bd_ref[0]
            _store_rows(ybuf.at[s], y, blk)
            for r in range(blk):
                pltpu.make_async_copy(ybuf.at[s, _row_tile(r)], y_hbm.at[_row_tile(dst_ref[0, 0, r])],
                                      osem.at[s]).start(r % 2)

    for s in range(2):
        pl.when(i % 2 == s)(functools.partial(block_step, s))

    @pl.when(i == n_blocks - 1)
    def _():
        slot = i % 2

        @pl.when((i >= 1) & (i - 1 < n_valid))
        def _():
            wait_scatter(1 - slot)

        @pl.when(i < n_valid)
        def _():
            wait_scatter(slot)


def _experts(x_slots, dst, block_e, n_valid, w_gu, b_gu, w_down, b_down, blk, n_real_rows):
    n_blocks = block_e.shape[0]
    n_out_rows = n_real_rows + N_EXPERTS * blk
    dst3 = dst.reshape(n_blocks, 1, blk)
    used = lambda i, nv: jnp.minimum(i, nv[0] - 1)
    grid_spec = pltpu.PrefetchScalarGridSpec(
        num_scalar_prefetch=2,
        grid=(n_blocks,),
        in_specs=[
            pl.BlockSpec((1, 1, blk), lambda i, be, nv: (i, 0, 0), memory_space=pltpu.SMEM),
            pl.BlockSpec((blk * SUBLANES, LANES), lambda i, be, nv: (used(i, nv), 0)),
            pl.BlockSpec((1, D_MODEL, 2 * D_FF), lambda i, be, nv: (be[i], 0, 0)),
            pl.BlockSpec((1, 1, 2 * D_FF), lambda i, be, nv: (be[i], 0, 0)),
            pl.BlockSpec((1, D_FF, D_MODEL), lambda i, be, nv: (be[i], 0, 0)),
            pl.BlockSpec((1, 1, D_MODEL), lambda i, be, nv: (be[i], 0, 0)),
        ],
        out_specs=pl.BlockSpec(memory_space=pl.ANY),
        scratch_shapes=[pltpu.VMEM((2, blk * SUBLANES, LANES), F32),
                        pltpu.SemaphoreType.DMA((2,)),
                        pltpu.VMEM((D_MODEL, 2 * D_FF), BF16),
                        pltpu.VMEM((D_FF, D_MODEL), BF16)],
    )
    return pl.pallas_call(
        functools.partial(_expert_body, blk=blk, n_blocks=n_blocks, n_real_rows=n_real_rows),
        grid_spec=grid_spec,
        out_shape=jax.ShapeDtypeStruct((n_out_rows * SUBLANES, LANES), F32),
        compiler_params=pltpu.CompilerParams(
            dimension_semantics=("arbitrary",), vmem_limit_bytes=VMEM_LIMIT_BYTES),
        name="experts",
    )(block_e, n_valid, dst3, x_slots, w_gu, b_gu.reshape(N_EXPERTS, 1, 2 * D_FF),
      w_down, b_down.reshape(N_EXPERTS, 1, D_MODEL))


def _combine_body(*refs, tt, n_first):
    y_refs = refs[:TOP_K]
    h_ref, gate_ref, g2_ref, b2_ref, out_first, out_second = refs[TOP_K:]
    i = pl.program_id(0)
    gates = gate_ref[...]
    f = gates[:, 0:1] * _load_rows(y_refs[0], tt)
    for kk in range(1, TOP_K):
        f = f + gates[:, kk:kk + 1] * _load_rows(y_refs[kk], tt)
    out = _layer_norm(DEEPNORM_ALPHA * _load_rows(h_ref, tt) + f, g2_ref[...], b2_ref[...])

    @pl.when(i < n_first)
    def _():
        out_first[...] = out

    @pl.when(i >= n_first)
    def _():
        out_second[...] = out


def _combine(y_rows, gates, h_all, ln2_g, ln2_b, tt, n_tokens_first):
    T = gates.shape[0]
    n_tiles = T // tt
    n_first = n_tokens_first // tt
    assert T % tt == 0 and n_tokens_first % tt == 0 and 0 < n_first < n_tiles
    row_blk = lambda imap: pl.BlockSpec((tt * SUBLANES, LANES), imap)
    y_specs = [row_blk(functools.partial(lambda i, kk: (kk * n_tiles + i, 0), kk=kk)) for kk in range(TOP_K)]
    return pl.pallas_call(
        functools.partial(_combine_body, tt=tt, n_first=n_first),
        grid=(n_tiles,),
        in_specs=y_specs + [
            row_blk(lambda i: (i, 0)),
            pl.BlockSpec((tt, TOP_K), lambda i: (i, 0)),
            pl.BlockSpec((1, D_MODEL), lambda i: (0, 0)),
            pl.BlockSpec((1, D_MODEL), lambda i: (0, 0)),
        ],
        out_specs=[pl.BlockSpec((tt, D_MODEL), lambda i: (jnp.minimum(i, n_first - 1), 0)),
                   pl.BlockSpec((tt, D_MODEL), lambda i: (jnp.maximum(i - n_first, 0), 0))],
        out_shape=[jax.ShapeDtypeStruct((n_tokens_first, D_MODEL), F32),
                   jax.ShapeDtypeStruct((T - n_tokens_first, D_MODEL), F32)],
        compiler_params=pltpu.CompilerParams(
            dimension_semantics=("arbitrary",), vmem_limit_bytes=VMEM_LIMIT_BYTES),
        name="combine",
    )(*([y_rows] * TOP_K), h_all, gates, ln2_g, ln2_b)


def kernel(x_prompt, x_sample, state_retention, state_conv, w_in, beta_ret, beta_conv, conv_w, conv_b,
           w_out, ln1_g, ln1_b, w_router, b_router, w_gu, b_gu, w_down, b_down, ln2_g, ln2_b):
    assert w_in.shape[0] == DEPTH == 1
    B, S, _ = x_prompt.shape
    Bd, Ld, _ = x_sample.shape
    Tp, Ts = B * S, Bd * Ld
    T = Tp + Ts
    wr_t = w_router[0].T
    wr_hi = wr_t.astype(BF16)
    wr_lo = (wr_t - wr_hi.astype(F32)).astype(BF16)
    w = dict(w_in=w_in[0].astype(BF16), w_out=w_out[0].astype(BF16), conv_w=conv_w[0],
             conv_b=conv_b[0][None], beta_ret=beta_ret[0][None], beta_conv=beta_conv[0][None],
             ln1_g=ln1_g[0][None], ln1_b=ln1_b[0][None], w_router=jnp.stack([wr_hi, wr_lo]),
             b_router=b_router[0][:, None])

    s_ret0 = jnp.zeros((B, N_RET_HEADS, HEAD_DIM, HEAD_DIM), F32)
    s_conv0 = jnp.zeros((B, CONV_W - 1, D_CONV), F32)
    h_s, choices_s, gates_s, ranks_s, counts_s, sret_s, sconv_s = _mixer(
        x_sample, state_retention[0], state_conv[0], PAST_LEN, SAMPLE_BATCH_BLOCK, Ld, w)
    h_all, choices_t, gates_t, ranks_t, counts_p, sret_p, sconv_p = _mixer(
        x_prompt, s_ret0, s_conv0, 0, 1, PROMPT_CHUNK, w, tail=(h_s, choices_s, gates_s, ranks_s))

    pos, dst, block_e, n_valid, zero_blocks = _routing_plan(
        choices_t, ranks_t, counts_p[:, 0].astype(jnp.int32), counts_s[:, 0].astype(jnp.int32), Tp,
        EXPERT_BLOCK)
    n_slots = block_e.shape[0] * EXPERT_BLOCK
    x_slots = _dispatch(h_all, pos, zero_blocks, n_slots, TOKEN_TILE, EXPERT_BLOCK)
    y_rows = _experts(x_slots, dst, block_e, n_valid, w_gu[0], b_gu[0], w_down[0], b_down[0],
                      EXPERT_BLOCK, T * TOP_K)
    y_p, y_s = _combine(y_rows, gates_t.T, h_all, ln2_g[0][None], ln2_b[0][None], TOKEN_TILE, Tp)

    return (y_p.reshape(B, S, D_MODEL), y_s.reshape(Bd, Ld, D_MODEL),
            sret_p[None], sconv_p[None], sret_s[None], sconv_s[None])
```

```python
import functools

import jax
import jax.numpy as jnp
from jax import lax
from jax.experimental import pallas as pl
from jax.experimental.pallas import tpu as pltpu

D_MODEL = 1024
N_RET_HEADS = 4
HEAD_DIM = 128
D_RET = N_RET_HEADS * HEAD_DIM
D_CONV = D_MODEL - D_RET
CONV_W = 3
N_EXPERTS = 32
TOP_K = 4
D_FF = D_MODEL
SWIGLU_LIMIT = 7.0
SWIGLU_ALPHA = 1.702
ROPE_BASE = 10000.0
LN_EPS = 1e-5
GN_EPS = 1e-5
DEPTH = 1
PAST_LEN = 1024
DEEPNORM_ALPHA = (2.0 * DEPTH) ** 0.25

LANES = 128
SUBLANES = 8
ROW_CHUNKS = D_MODEL // LANES
assert ROW_CHUNKS == SUBLANES

PROMPT_CHUNK = 256
SAMPLE_BATCH_BLOCK = 8
EXPERT_BLOCK = 512
DISPATCH_TILE = 512
COMBINE_TILE = 256
DMA_UNROLL = 8
PAD_BIT = 18
UNUSED_KEY = 1 << 30
VMEM_LIMIT_BYTES = 56 * 1024 * 1024

F32 = jnp.float32
BF16 = jnp.bfloat16


def _layer_norm(x, g, b):
    mu = jnp.mean(x, axis=-1, keepdims=True)
    xc = x - mu
    var = jnp.mean(xc * xc, axis=-1, keepdims=True)
    return xc * lax.rsqrt(var + LN_EPS) * g + b


def _load_rows(ref, n_rows):
    return jnp.concatenate(
        [ref[pl.ds(j, n_rows, stride=SUBLANES), :] for j in range(ROW_CHUNKS)], axis=1)


def _store_rows(ref, val, n_rows):
    for j in range(ROW_CHUNKS):
        ref[pl.ds(j, n_rows, stride=SUBLANES), :] = val[:, j * LANES:(j + 1) * LANES]


def _row_tile(r):
    if isinstance(r, int):
        return pl.ds(r * SUBLANES, SUBLANES)
    return pl.ds(pl.multiple_of(r * SUBLANES, SUBLANES), SUBLANES)


N_MIXER_INPUTS = 19
N_TOKEN_OUTPUTS = 4


def _mixer_body(*refs, nb, C, n_batch_steps, n_tail):
    if not n_tail:
        _mixer_compute(*refs, nb=nb, C=C)
        return
    tails = refs[N_MIXER_INPUTS:N_MIXER_INPUTS + N_TOKEN_OUTPUTS]
    outs = refs[N_MIXER_INPUTS + N_TOKEN_OUTPUTS:N_MIXER_INPUTS + 2 * N_TOKEN_OUTPUTS]
    bstep = pl.program_id(0)

    @pl.when(bstep < n_batch_steps)
    def _():
        _mixer_compute(*refs[:N_MIXER_INPUTS], *refs[N_MIXER_INPUTS + N_TOKEN_OUTPUTS:], nb=nb, C=C)

    @pl.when((bstep == n_batch_steps) & (pl.program_id(1) < n_tail))
    def _():
        for tail_ref, out_ref in zip(tails, outs):
            out_ref[...] = tail_ref[...]


def _route(logits_t, run_ref):
    n_tok = logits_t.shape[1]
    expert = lax.broadcasted_iota(jnp.int32, logits_t.shape, 0)
    vals, idxs, hots = [], [], []
    for _ in range(TOP_K):
        m = jnp.max(logits_t, axis=0, keepdims=True)
        idx = jnp.min(jnp.where(logits_t == m, expert, N_EXPERTS), axis=0, keepdims=True)
        hot = expert == idx
        vals.append(m)
        idxs.append(idx)
        hots.append(hot)
        logits_t = jnp.where(hot, -jnp.inf, logits_t)
    ex = [jnp.exp(v - vals[0]) for v in vals]
    total = ex[0]
    for e in ex[1:]:
        total = total + e
    gates = jnp.concatenate([e / total for e in ex], axis=0)

    chosen = hots[0].astype(F32)
    for hot in hots[1:]:
        chosen = chosen + hot.astype(F32)
    chosen = chosen.astype(BF16)
    t_row = lax.broadcasted_iota(jnp.int32, (n_tok, n_tok), 0)
    t_col = lax.broadcasted_iota(jnp.int32, (n_tok, n_tok), 1)
    earlier = jnp.where(t_row < t_col, 1.0, 0.0).astype(BF16)
    before = run_ref[...] + jnp.dot(chosen, earlier, preferred_element_type=F32)
    ranks = [jnp.sum(jnp.where(hot, before, 0.0), axis=0, keepdims=True) for hot in hots]
    run_ref[...] = run_ref[...] + jnp.dot(chosen, jnp.ones((n_tok, n_tok), BF16), preferred_element_type=F32)
    return jnp.concatenate(idxs, axis=0), gates, jnp.concatenate(ranks, axis=0).astype(jnp.int32)


def _mixer_compute(x_ref, cos_ref, sin_ref, decay_ref, qdec_ref, kdec_ref, sdec_ref, win_ref, wout_ref,
                   convw_ref, convb_ref, bret_ref, bconv_ref, g1_ref, b1_ref, wr_ref, br_ref,
                   sret_ref, sconv_ref, h_ref, choice_ref, gate_ref, rank_ref, count_ref, sret_out, sconv_out,
                   mix_ref, run_ref, *, nb, C):
    c = pl.program_id(1)

    @pl.when((pl.program_id(0) == 0) & (c == 0))
    def _():
        run_ref[...] = jnp.zeros(run_ref.shape, F32)

    @pl.when(c == 0)
    def _():
        sret_out[...] = sret_ref[...]
        sconv_out[...] = sconv_ref[...]

    x = x_ref[...].reshape(nb * C, D_MODEL)
    xb = x.astype(BF16)

    def proj(col0, width):
        return jnp.dot(xb, win_ref[:, col0:col0 + width], preferred_element_type=F32)

    q = proj(0, D_RET)
    k = proj(D_RET, D_RET)
    v = proj(2 * D_RET, D_RET)
    g = proj(3 * D_RET, D_RET)
    bg = proj(4 * D_RET, D_CONV)
    cg = proj(4 * D_RET + D_CONV, D_CONV)
    hc = proj(4 * D_RET + 2 * D_CONV, D_CONV)

    cos = cos_ref[...]
    sin = sin_ref[...]
    row = lax.broadcasted_iota(jnp.int32, (C, D_CONV), 0)
    k_scale = HEAD_DIM ** -0.5

    for b in range(nb):
        r0 = b * C
        for h in range(N_RET_HEADS):
            c0 = h * HEAD_DIM
            qh = q[r0:r0 + C, c0:c0 + HEAD_DIM]
            kh = k[r0:r0 + C, c0:c0 + HEAD_DIM]
            vh = v[r0:r0 + C, c0:c0 + HEAD_DIM]
            qh = qh * cos + pltpu.roll(qh, HEAD_DIM // 2, axis=1) * sin
            kh = (kh * cos + pltpu.roll(kh, HEAD_DIM // 2, axis=1) * sin) * k_scale
            qb = qh.astype(BF16)
            kb = kh.astype(BF16)
            vb = vh.astype(BF16)
            s_old = sret_out[b, h]
            scores = lax.dot_general(qb, kb, (((1,), (1,)), ((), ())), preferred_element_type=F32)
            scores = scores * decay_ref[h]
            intra = jnp.dot(scores.astype(BF16), vb, preferred_element_type=F32)
            cross = jnp.dot(qb, s_old.astype(BF16), preferred_element_type=F32) * qdec_ref[h]
            o = intra + cross
            kd = (kh * kdec_ref[h]).astype(BF16)
            s_new = sdec_ref[h] * s_old + lax.dot_general(
                kd, vb, (((0,), (0,)), ((), ())), preferred_element_type=F32)
            sret_out[b, h] = s_new
            mu = jnp.mean(o, axis=-1, keepdims=True)
            oc = o - mu
            var = jnp.mean(oc * oc, axis=-1, keepdims=True)
            on = oc * lax.rsqrt(var + GN_EPS)
            gh = g[r0:r0 + C, c0:c0 + HEAD_DIM]
            ret = on * (gh * jax.nn.sigmoid(gh)) * bret_ref[:, c0:c0 + HEAD_DIM]
            mix_ref[r0:r0 + C, c0:c0 + HEAD_DIM] = ret.astype(BF16)

        u = cg[r0:r0 + C] * hc[r0:r0 + C]
        prev = sconv_out[b]
        u1 = jnp.where(row == 0, prev[1:2], pltpu.roll(u, 1, axis=0))
        u2 = jnp.where(row == 0, prev[0:1], jnp.where(row == 1, prev[1:2], pltpu.roll(u, 2, axis=0)))
        z = convb_ref[...] + convw_ref[0:1] * u2 + convw_ref[1:2] * u1 + convw_ref[2:3] * u
        conv_out = bg[r0:r0 + C] * z * bconv_ref[...]
        mix_ref[r0:r0 + C, D_RET:D_RET + D_CONV] = conv_out.astype(BF16)
        sconv_out[b] = u[C - 2:C]

    m = jnp.dot(mix_ref[...], wout_ref[...], preferred_element_type=F32)
    hval = _layer_norm(DEEPNORM_ALPHA * x + m, g1_ref[...], b1_ref[...])
    _store_rows(h_ref, hval, nb * C)
    h_hi = hval.astype(BF16)
    h_lo = (hval - h_hi.astype(F32)).astype(BF16)
    nt_dot = lambda a, b: lax.dot_general(a, b, (((1,), (1,)), ((), ())), preferred_element_type=F32)
    logits_t = nt_dot(wr_ref[0], h_hi) + nt_dot(wr_ref[0], h_lo) + nt_dot(wr_ref[1], h_hi) + br_ref[...]
    choice_ref[...], gate_ref[...], rank_ref[...] = _route(logits_t, run_ref)
    count_ref[...] = run_ref[:, :LANES]


def _retention_tables(C, pos0, L):
    lg = jnp.log1p(-jnp.power(2.0, -5.0 - jnp.arange(N_RET_HEADS, dtype=F32)))
    i = jnp.arange(C, dtype=F32)
    diff = i[:, None] - i[None, :]
    decay = jnp.where(diff[None] >= 0, jnp.exp(lg[:, None, None] * jnp.maximum(diff, 0.0)[None]), 0.0)
    qdec = jnp.exp(lg[:, None] * (i + 1.0)[None, :])
    kdec = jnp.exp(lg[:, None] * (C - 1.0 - i)[None, :])
    sdec = jnp.exp(lg * C)
    qdec = jnp.broadcast_to(qdec[:, :, None], (N_RET_HEADS, C, HEAD_DIM))
    kdec = jnp.broadcast_to(kdec[:, :, None], (N_RET_HEADS, C, HEAD_DIM))
    sdec = jnp.broadcast_to(sdec[:, None, None], (N_RET_HEADS, 1, HEAD_DIM))
    pos = pos0 + jnp.arange(L, dtype=jnp.int32)
    inv_freq = ROPE_BASE ** (-jnp.arange(0, HEAD_DIM, 2, dtype=F32) / HEAD_DIM)
    ang = pos.astype(F32)[:, None] * inv_freq[None, :]
    cos = jnp.cos(ang)
    sin = jnp.sin(ang)
    cos_full = jnp.concatenate([cos, cos], axis=-1)
    sin_signed = jnp.concatenate([-sin, sin], axis=-1)
    return cos_full, sin_signed, decay, qdec, kdec, sdec


def _mixer(x, s_ret, s_conv, pos0, nb, C, w, tail=None):
    B, L, _ = x.shape
    nc = L // C
    nbs = B // nb
    rows = nb * C
    n_tokens = B * L
    n_tail = 0
    if tail is not None:
        n_tokens += tail[1].shape[1]
        n_tail = tail[1].shape[1] // rows
        assert tail[1].shape[1] % rows == 0 and 0 < n_tail <= nc
    cos, sin, decay, qdec, kdec, sdec = _retention_tables(C, pos0, L)

    bb = lambda b: jnp.minimum(b, nbs - 1)
    cc = lambda b, c: jnp.where(b < nbs, c, nc - 1)
    out_blk = lambda b, c: jnp.where(b < nbs, b * nc + c, nbs * nc + jnp.minimum(c, n_tail - 1))
    const2 = lambda b, c: (0, 0)
    const3 = lambda b, c: (0, 0, 0)
    full = lambda a: pl.BlockSpec(a.shape, const2 if a.ndim == 2 else const3)
    per_token = lambda imap: pl.BlockSpec((TOP_K, rows), lambda b, c: (0, imap(b, c)))
    in_arrays = [x, cos, sin, decay, qdec, kdec, sdec, w["w_in"], w["w_out"], w["conv_w"], w["conv_b"],
                 w["beta_ret"], w["beta_conv"], w["ln1_g"], w["ln1_b"], w["w_router"], w["b_router"],
                 s_ret, s_conv]
    assert len(in_arrays) == N_MIXER_INPUTS
    in_specs = [pl.BlockSpec((nb, C, D_MODEL), lambda b, c: (bb(b), cc(b, c), 0)),
                pl.BlockSpec((C, HEAD_DIM), lambda b, c: (cc(b, c), 0)),
                pl.BlockSpec((C, HEAD_DIM), lambda b, c: (cc(b, c), 0))]
    in_specs += [full(a) for a in in_arrays[3:17]]
    in_specs += [pl.BlockSpec((nb, N_RET_HEADS, HEAD_DIM, HEAD_DIM), lambda b, c: (bb(b), 0, 0, 0)),
                 pl.BlockSpec((nb, CONV_W - 1, D_CONV), lambda b, c: (bb(b), 0, 0))]
    if n_tail:
        tail_blk = lambda b, c: jnp.where(b < nbs, 0, jnp.minimum(c, n_tail - 1))
        in_arrays += list(tail)
        in_specs += [pl.BlockSpec((rows * SUBLANES, LANES), lambda b, c: (tail_blk(b, c), 0))]
        in_specs += [per_token(tail_blk)] * (N_TOKEN_OUTPUTS - 1)
    out_shape = [jax.ShapeDtypeStruct((n_tokens * SUBLANES, LANES), F32),
                 jax.ShapeDtypeStruct((TOP_K, n_tokens), jnp.int32),
                 jax.ShapeDtypeStruct((TOP_K, n_tokens), F32),
                 jax.ShapeDtypeStruct((TOP_K, n_tokens), jnp.int32),
                 jax.ShapeDtypeStruct((N_EXPERTS, LANES), F32),
                 jax.ShapeDtypeStruct((B, N_RET_HEADS, HEAD_DIM, HEAD_DIM), F32),
                 jax.ShapeDtypeStruct((B, CONV_W - 1, D_CONV), F32)]
    assert len(out_shape) == N_TOKEN_OUTPUTS + 3
    out_specs = [pl.BlockSpec((rows * SUBLANES, LANES), lambda b, c: (out_blk(b, c), 0))]
    out_specs += [per_token(out_blk)] * (N_TOKEN_OUTPUTS - 1)
    out_specs += [pl.BlockSpec((N_EXPERTS, LANES), const2),
                  pl.BlockSpec((nb, N_RET_HEADS, HEAD_DIM, HEAD_DIM), lambda b, c: (bb(b), 0, 0, 0)),
                  pl.BlockSpec((nb, CONV_W - 1, D_CONV), lambda b, c: (bb(b), 0, 0))]
    return pl.pallas_call(
        functools.partial(_mixer_body, nb=nb, C=C, n_batch_steps=nbs, n_tail=n_tail),
        grid=(nbs + (1 if n_tail else 0), nc), in_specs=in_specs, out_specs=out_specs, out_shape=out_shape,
        scratch_shapes=[pltpu.VMEM((rows, D_MODEL), BF16), pltpu.VMEM((N_EXPERTS, rows), F32)],
        compiler_params=pltpu.CompilerParams(
            dimension_semantics=("arbitrary", "arbitrary"), vmem_limit_bytes=VMEM_LIMIT_BYTES),
        name="mixer",
    )(*in_arrays)


def _lookup(table, choices):
    experts = jnp.arange(N_EXPERTS, dtype=jnp.int32)
    return jnp.sum(jnp.where(choices[..., None] == experts, table, 0), axis=-1)


def _routing_plan(choices_t, ranks_t, counts_first, counts_second, n_first, blk):
    T = choices_t.shape[1]
    A = T * TOP_K
    assert A < (1 << PAD_BIT) and N_EXPERTS * blk < (1 << PAD_BIT) and A % blk == 0
    n_blocks = A // blk + N_EXPERTS
    experts = jnp.arange(N_EXPERTS, dtype=jnp.int32)
    counts = counts_first + counts_second
    nblk_e = (counts + blk - 1) // blk
    bend = jnp.cumsum(nblk_e)
    n_valid = bend[-1]
    start = (bend - nblk_e) * blk
    token = jnp.arange(T, dtype=jnp.int32)[None, :]
    rank = ranks_t + jnp.where(token >= n_first, _lookup(counts_first, choices_t), 0)
    pos = _lookup(start, choices_t) + rank

    assign_id = (token * TOP_K + jnp.arange(TOP_K, dtype=jnp.int32)[:, None]).reshape(A)
    real_keys = (choices_t.reshape(A) << (PAD_BIT + 1)) | assign_id
    n_pad = nblk_e * blk - counts
    j = jnp.arange(blk, dtype=jnp.int32)[None, :]
    pad_keys = jnp.where(j < n_pad[:, None],
                         (experts[:, None] << (PAD_BIT + 1)) | (1 << PAD_BIT) | (experts[:, None] * blk + j),
                         UNUSED_KEY)
    slots = jnp.sort(jnp.concatenate([real_keys, pad_keys.reshape(-1)]))
    ident = slots & ((1 << PAD_BIT) - 1)
    is_real = (((slots >> PAD_BIT) & 1) == 0) & (slots < UNUSED_KEY)
    dst = jnp.where(is_real, (ident & (TOP_K - 1)) * T + (ident >> 2),
                    jnp.where(slots < UNUSED_KEY, A + ident, A))
    bi = jnp.arange(n_blocks, dtype=jnp.int32)
    block_e = slots.reshape(n_blocks, blk)[:, 0] >> (PAD_BIT + 1)
    block_e = jnp.where(bi < n_valid, block_e, block_e[n_valid - 1])
    last_block = jnp.where(nblk_e > 0, bend - 1, -1)
    unused = n_valid + experts
    unused = jnp.where(unused < n_blocks, unused, -1)
    zero_blocks = jnp.concatenate([last_block, unused]).astype(jnp.int32)
    return pos, dst.reshape(n_blocks, blk), block_e, n_valid.reshape(1).astype(jnp.int32), zero_blocks


def _dispatch_body(zero_ref, pos_ref, h_ref, xs_hbm, hbuf, sem, *, tt, blk):
    i = pl.program_id(0)
    n = pl.num_programs(0)
    slot = i % 2

    def wait_rows(s):
        for _ in range(TOP_K):
            pltpu.make_async_copy(hbuf.at[s], xs_hbm.at[pl.ds(0, tt * SUBLANES)], sem.at[s]).wait()

    @pl.when(i == 0)
    def _():
        hbuf[1] = jnp.zeros(hbuf.shape[1:], F32)
        zeros = hbuf.at[1, pl.ds(0, blk * SUBLANES)]
        for e in range(zero_ref.shape[0]):
            @pl.when(zero_ref[e] >= 0)
            def _():
                first_row = pl.multiple_of(zero_ref[e] * (blk * SUBLANES), blk * SUBLANES)
                pltpu.make_async_copy(zeros, xs_hbm.at[pl.ds(first_row, blk * SUBLANES)], sem.at[1]).start()
        for e in range(zero_ref.shape[0]):
            @pl.when(zero_ref[e] >= 0)
            def _():
                pltpu.make_async_copy(zeros, xs_hbm.at[pl.ds(0, blk * SUBLANES)], sem.at[1]).wait()

    @pl.when(i >= 2)
    def _():
        wait_rows(slot)

    hbuf[slot] = h_ref[...]
    for kk in range(TOP_K):
        def body(gi, carry):
            for u in range(DMA_UNROLL):
                t = gi * DMA_UNROLL + u
                pltpu.make_async_copy(hbuf.at[slot, _row_tile(t)],
                                      xs_hbm.at[_row_tile(pos_ref[0, 0, kk * tt + t])],
                                      sem.at[slot]).start(u % 2)
            return carry
        lax.fori_loop(0, tt // DMA_UNROLL, body, 0)

    @pl.when(i == n - 1)
    def _():
        wait_rows(slot)

        @pl.when(i >= 1)
        def _():
            wait_rows(1 - slot)


def _dispatch(h_all, pos, zero_blocks, n_slots, tt, blk):
    T = pos.shape[1]
    n_tiles = T // tt
    assert T % tt == 0 and blk <= tt
    pos3 = pos.reshape(TOP_K, n_tiles, tt).transpose(1, 0, 2).reshape(n_tiles, 1, TOP_K * tt)
    grid_spec = pltpu.PrefetchScalarGridSpec(
        num_scalar_prefetch=1,
        grid=(n_tiles,),
        in_specs=[pl.BlockSpec((1, 1, TOP_K * tt), lambda i, zb: (i, 0, 0), memory_space=pltpu.SMEM),
                  pl.BlockSpec((tt * SUBLANES, LANES), lambda i, zb: (i, 0))],
        out_specs=pl.BlockSpec(memory_space=pl.ANY),
        scratch_shapes=[pltpu.VMEM((2, tt * SUBLANES, LANES), F32), pltpu.SemaphoreType.DMA((2,))],
    )
    return pl.pallas_call(
        functools.partial(_dispatch_body, tt=tt, blk=blk),
        grid_spec=grid_spec,
        out_shape=jax.ShapeDtypeStruct((n_slots * SUBLANES, LANES), F32),
        compiler_params=pltpu.CompilerParams(
            dimension_semantics=("arbitrary",), vmem_limit_bytes=VMEM_LIMIT_BYTES),
        name="dispatch",
    )(zero_blocks, pos3, h_all)


def _expert_body(be_ref, nv_ref, dst_ref, x_ref, wgu_ref, bgu_ref, wd_ref, bd_ref,
                 y_hbm, ybuf, osem, wgu_bf, wd_bf, *, blk, n_blocks, n_real_rows):
    i = pl.program_id(0)
    n_valid = nv_ref[0]

    def wait_scatter(s):
        pltpu.make_async_copy(ybuf.at[s], y_hbm.at[pl.ds(0, blk * SUBLANES)], osem.at[s]).wait()

    @pl.when(i == 0)
    def _():
        ybuf[0] = jnp.zeros(ybuf.shape[1:], F32)
        for e in range(N_EXPERTS):
            pltpu.make_async_copy(
                ybuf.at[0], y_hbm.at[pl.ds((n_real_rows + e * blk) * SUBLANES, blk * SUBLANES)],
                osem.at[0]).start()
        for e in range(N_EXPERTS):
            wait_scatter(0)

    @pl.when((i < n_valid) & ((i == 0) | (be_ref[i] != be_ref[jnp.maximum(i - 1, 0)])))
    def _():
        wgu_bf[...] = wgu_ref[0].astype(BF16)
        wd_bf[...] = wd_ref[0].astype(BF16)

    def block_step(s):
        @pl.when((i >= 2) & (i - 2 < n_valid))
        def _():
            wait_scatter(s)

        @pl.when(i < n_valid)
        def _():
            xb = _load_rows(x_ref, blk).astype(BF16)
            gu = jnp.dot(xb, wgu_bf[...], preferred_element_type=F32) + bgu_ref[0]
            gate = jnp.minimum(gu[:, :D_FF], SWIGLU_LIMIT)
            up = jnp.clip(gu[:, D_FF:], -SWIGLU_LIMIT, SWIGLU_LIMIT)
            act = (up + 1.0) * gate * jax.nn.sigmoid(SWIGLU_ALPHA * gate)
            y = jnp.dot(act.astype(BF16), wd_bf[...], preferred_element_type=F32) + bd_ref[0]
            _store_rows(ybuf.at[s], y, blk)
            for r in range(blk):
                pltpu.make_async_copy(ybuf.at[s, _row_tile(r)], y_hbm.at[_row_tile(dst_ref[0, 0, r])],
                                      osem.at[s]).start(r % 2)

    for s in range(2):
        pl.when(i % 2 == s)(functools.partial(block_step, s))

    @pl.when(i == n_blocks - 1)
    def _():
        slot = i % 2

        @pl.when((i >= 1) & (i - 1 < n_valid))
        def _():
            wait_scatter(1 - slot)

        @pl.when(i < n_valid)
        def _():
            wait_scatter(slot)


def _experts(x_slots, dst, block_e, n_valid, w_gu, b_gu, w_down, b_down, blk, n_real_rows):
    n_blocks = block_e.shape[0]
    n_out_rows = n_real_rows + N_EXPERTS * blk
    dst3 = dst.reshape(n_blocks, 1, blk)
    used = lambda i, nv: jnp.minimum(i, nv[0] - 1)
    grid_spec = pltpu.PrefetchScalarGridSpec(
        num_scalar_prefetch=2,
        grid=(n_blocks,),
        in_specs=[
            pl.BlockSpec((1, 1, blk), lambda i, be, nv: (i, 0, 0), memory_space=pltpu.SMEM),
            pl.BlockSpec((blk * SUBLANES, LANES), lambda i, be, nv: (used(i, nv), 0)),
            pl.BlockSpec((1, D_MODEL, 2 * D_FF), lambda i, be, nv: (be[i], 0, 0)),
            pl.BlockSpec((1, 1, 2 * D_FF), lambda i, be, nv: (be[i], 0, 0)),
            pl.BlockSpec((1, D_FF, D_MODEL), lambda i, be, nv: (be[i], 0, 0)),
            pl.BlockSpec((1, 1, D_MODEL), lambda i, be, nv: (be[i], 0, 0)),
        ],
        out_specs=pl.BlockSpec(memory_space=pl.ANY),
        scratch_shapes=[pltpu.VMEM((2, blk * SUBLANES, LANES), F32),
                        pltpu.SemaphoreType.DMA((2,)),
                        pltpu.VMEM((D_MODEL, 2 * D_FF), BF16),
                        pltpu.VMEM((D_FF, D_MODEL), BF16)],
    )
    return pl.pallas_call(
        functools.partial(_expert_body, blk=blk, n_blocks=n_blocks, n_real_rows=n_real_rows),
        grid_spec=grid_spec,
        out_shape=jax.ShapeDtypeStruct((n_out_rows * SUBLANES, LANES), F32),
        compiler_params=pltpu.CompilerParams(
            dimension_semantics=("arbitrary",), vmem_limit_bytes=VMEM_LIMIT_BYTES),
        name="experts",
    )(block_e, n_valid, dst3, x_slots, w_gu, b_gu.reshape(N_EXPERTS, 1, 2 * D_FF),
      w_down, b_down.reshape(N_EXPERTS, 1, D_MODEL))


def _combine_body(*refs, tt, n_first):
    y_refs = refs[:TOP_K]
    h_ref, gate_ref, g2_ref, b2_ref, out_first, out_second = refs[TOP_K:]
    i = pl.program_id(0)
    gates = gate_ref[...]
    f = gates[:, 0:1] * _load_rows(y_refs[0], tt)
    for kk in range(1, TOP_K):
        f = f + gates[:, kk:kk + 1] * _load_rows(y_refs[kk], tt)
    out = _layer_norm(DEEPNORM_ALPHA * _load_rows(h_ref, tt) + f, g2_ref[...], b2_ref[...])

    @pl.when(i < n_first)
    def _():
        out_first[...] = out

    @pl.when(i >= n_first)
    def _():
        out_second[...] = out


def _combine(y_rows, gates, h_all, ln2_g, ln2_b, tt, n_tokens_first):
    T = gates.shape[0]
    n_tiles = T // tt
    n_first = n_tokens_first // tt
    assert T % tt == 0 and n_tokens_first % tt == 0 and 0 < n_first < n_tiles
    row_blk = lambda imap: pl.BlockSpec((tt * SUBLANES, LANES), imap)
    y_specs = [row_blk(functools.partial(lambda i, kk: (kk * n_tiles + i, 0), kk=kk)) for kk in range(TOP_K)]
    return pl.pallas_call(
        functools.partial(_combine_body, tt=tt, n_first=n_first),
        grid=(n_tiles,),
        in_specs=y_specs + [
            row_blk(lambda i: (i, 0)),
            pl.BlockSpec((tt, TOP_K), lambda i: (i, 0)),
            pl.BlockSpec((1, D_MODEL), lambda i: (0, 0)),
            pl.BlockSpec((1, D_MODEL), lambda i: (0, 0)),
        ],
        out_specs=[pl.BlockSpec((tt, D_MODEL), lambda i: (jnp.minimum(i, n_first - 1), 0)),
                   pl.BlockSpec((tt, D_MODEL), lambda i: (jnp.maximum(i - n_first, 0), 0))],
        out_shape=[jax.ShapeDtypeStruct((n_tokens_first, D_MODEL), F32),
                   jax.ShapeDtypeStruct((T - n_tokens_first, D_MODEL), F32)],
        compiler_params=pltpu.CompilerParams(
            dimension_semantics=("arbitrary",), vmem_limit_bytes=VMEM_LIMIT_BYTES),
        name="combine",
    )(*([y_rows] * TOP_K), h_all, gates, ln2_g, ln2_b)


def kernel(x_prompt, x_sample, state_retention, state_conv, w_in, beta_ret, beta_conv, conv_w, conv_b,
           w_out, ln1_g, ln1_b, w_router, b_router, w_gu, b_gu, w_down, b_down, ln2_g, ln2_b):
    assert w_in.shape[0] == DEPTH == 1
    B, S, _ = x_prompt.shape
    Bd, Ld, _ = x_sample.shape
    Tp, Ts = B * S, Bd * Ld
    T = Tp + Ts
    wr_t = w_router[0].T
    wr_hi = wr_t.astype(BF16)
    wr_lo = (wr_t - wr_hi.astype(F32)).astype(BF16)
    w = dict(w_in=w_in[0].astype(BF16), w_out=w_out[0].astype(BF16), conv_w=conv_w[0],
             conv_b=conv_b[0][None], beta_ret=beta_ret[0][None], beta_conv=beta_conv[0][None],
             ln1_g=ln1_g[0][None], ln1_b=ln1_b[0][None], w_router=jnp.stack([wr_hi, wr_lo]),
             b_router=b_router[0][:, None])

    s_ret0 = jnp.zeros((B, N_RET_HEADS, HEAD_DIM, HEAD_DIM), F32)
    s_conv0 = jnp.zeros((B, CONV_W - 1, D_CONV), F32)
    h_s, choices_s, gates_s, ranks_s, counts_s, sret_s, sconv_s = _mixer(
        x_sample, state_retention[0], state_conv[0], PAST_LEN, SAMPLE_BATCH_BLOCK, Ld, w)
    h_all, choices_t, gates_t, ranks_t, counts_p, sret_p, sconv_p = _mixer(
        x_prompt, s_ret0, s_conv0, 0, 1, PROMPT_CHUNK, w, tail=(h_s, choices_s, gates_s, ranks_s))

    pos, dst, block_e, n_valid, zero_blocks = _routing_plan(
        choices_t, ranks_t, counts_p[:, 0].astype(jnp.int32), counts_s[:, 0].astype(jnp.int32), Tp,
        EXPERT_BLOCK)
    n_slots = block_e.shape[0] * EXPERT_BLOCK
    x_slots = _dispatch(h_all, pos, zero_blocks, n_slots, DISPATCH_TILE, EXPERT_BLOCK)
    y_rows = _experts(x_slots, dst, block_e, n_valid, w_gu[0], b_gu[0], w_down[0], b_down[0],
                      EXPERT_BLOCK, T * TOP_K)
    y_p, y_s = _combine(y_rows, gates_t.T, h_all, ln2_g[0][None], ln2_b[0][None], COMBINE_TILE, Tp)

    return (y_p.reshape(B, S, D_MODEL), y_s.reshape(Bd, Ld, D_MODEL),
            sret_p[None], sconv_p[None], sret_s[None], sconv_s[None])
```

```python
import functools

import jax
import jax.numpy as jnp
from jax import lax
from jax.experimental import pallas as pl
from jax.experimental.pallas import tpu as pltpu

D_MODEL = 1024
N_RET_HEADS = 4
HEAD_DIM = 128
D_RET = N_RET_HEADS * HEAD_DIM
D_CONV = D_MODEL - D_RET
CONV_W = 3
N_EXPERTS = 32
TOP_K = 4
D_FF = D_MODEL
SWIGLU_LIMIT = 7.0
SWIGLU_ALPHA = 1.702
ROPE_BASE = 10000.0
LN_EPS = 1e-5
GN_EPS = 1e-5
DEPTH = 1
PAST_LEN = 1024
DEEPNORM_ALPHA = (2.0 * DEPTH) ** 0.25

LANES = 128
SUBLANES = 8
ROW_CHUNKS = D_MODEL // LANES
assert ROW_CHUNKS == SUBLANES

PROMPT_CHUNK = 256
SAMPLE_BATCH_BLOCK = 8
EXPERT_BLOCK = 512
DISPATCH_TILE = 512
COMBINE_TILE = 256
DMA_UNROLL = 8
N_CHUNKS = 4
PAD_BIT = 18
UNUSED_KEY = 1 << 30
VMEM_LIMIT_BYTES = 56 * 1024 * 1024

F32 = jnp.float32
BF16 = jnp.bfloat16


def _layer_norm(x, g, b):
    mu = jnp.mean(x, axis=-1, keepdims=True)
    xc = x - mu
    var = jnp.mean(xc * xc, axis=-1, keepdims=True)
    return xc * lax.rsqrt(var + LN_EPS) * g + b


def _load_rows(ref, n_rows):
    return jnp.concatenate(
        [ref[pl.ds(j, n_rows, stride=SUBLANES), :] for j in range(ROW_CHUNKS)], axis=1)


def _store_rows(ref, val, n_rows):
    for j in range(ROW_CHUNKS):
        ref[pl.ds(j, n_rows, stride=SUBLANES), :] = val[:, j * LANES:(j + 1) * LANES]


def _row_tile(r):
    if isinstance(r, int):
        return pl.ds(r * SUBLANES, SUBLANES)
    return pl.ds(pl.multiple_of(r * SUBLANES, SUBLANES), SUBLANES)


N_MIXER_INPUTS = 19
N_TOKEN_OUTPUTS = 4


def _mixer_body(*refs, nb, C, n_batch_steps, n_tail):
    if not n_tail:
        _mixer_compute(*refs, nb=nb, C=C)
        return
    tails = refs[N_MIXER_INPUTS:N_MIXER_INPUTS + N_TOKEN_OUTPUTS]
    outs = refs[N_MIXER_INPUTS + N_TOKEN_OUTPUTS:N_MIXER_INPUTS + 2 * N_TOKEN_OUTPUTS]
    bstep = pl.program_id(0)

    @pl.when(bstep < n_batch_steps)
    def _():
        _mixer_compute(*refs[:N_MIXER_INPUTS], *refs[N_MIXER_INPUTS + N_TOKEN_OUTPUTS:], nb=nb, C=C)

    @pl.when((bstep == n_batch_steps) & (pl.program_id(1) < n_tail))
    def _():
        for tail_ref, out_ref in zip(tails, outs):
            out_ref[...] = tail_ref[...]


def _route(logits_t, run_ref):
    n_tok = logits_t.shape[1]
    expert = lax.broadcasted_iota(jnp.int32, logits_t.shape, 0)
    vals, idxs, hots = [], [], []
    for _ in range(TOP_K):
        m = jnp.max(logits_t, axis=0, keepdims=True)
        idx = jnp.min(jnp.where(logits_t == m, expert, N_EXPERTS), axis=0, keepdims=True)
        hot = expert == idx
        vals.append(m)
        idxs.append(idx)
        hots.append(hot)
        logits_t = jnp.where(hot, -jnp.inf, logits_t)
    ex = [jnp.exp(v - vals[0]) for v in vals]
    total = ex[0]
    for e in ex[1:]:
        total = total + e
    gates = jnp.concatenate([e / total for e in ex], axis=0)

    chosen = hots[0].astype(F32)
    for hot in hots[1:]:
        chosen = chosen + hot.astype(F32)
    chosen = chosen.astype(BF16)
    t_row = lax.broadcasted_iota(jnp.int32, (n_tok, n_tok), 0)
    t_col = lax.broadcasted_iota(jnp.int32, (n_tok, n_tok), 1)
    earlier = jnp.where(t_row < t_col, 1.0, 0.0).astype(BF16)
    before = run_ref[...] + jnp.dot(chosen, earlier, preferred_element_type=F32)
    ranks = [jnp.sum(jnp.where(hot, before, 0.0), axis=0, keepdims=True) for hot in hots]
    run_ref[...] = run_ref[...] + jnp.dot(chosen, jnp.ones((n_tok, n_tok), BF16), preferred_element_type=F32)
    return jnp.concatenate(idxs, axis=0), gates, jnp.concatenate(ranks, axis=0).astype(jnp.int32)


def _mixer_compute(x_ref, cos_ref, sin_ref, decay_ref, qdec_ref, kdec_ref, sdec_ref, win_ref, wout_ref,
                   convw_ref, convb_ref, bret_ref, bconv_ref, g1_ref, b1_ref, wr_ref, br_ref,
                   sret_ref, sconv_ref, h_ref, choice_ref, gate_ref, rank_ref, count_ref, sret_out, sconv_out,
                   mix_ref, run_ref, *, nb, C):
    c = pl.program_id(1)

    @pl.when((pl.program_id(0) == 0) & (c == 0))
    def _():
        run_ref[...] = jnp.zeros(run_ref.shape, F32)

    @pl.when(c == 0)
    def _():
        sret_out[...] = sret_ref[...]
        sconv_out[...] = sconv_ref[...]

    x = x_ref[...].reshape(nb * C, D_MODEL)
    xb = x.astype(BF16)

    def proj(col0, width):
        return jnp.dot(xb, win_ref[:, col0:col0 + width], preferred_element_type=F32)

    q = proj(0, D_RET)
    k = proj(D_RET, D_RET)
    v = proj(2 * D_RET, D_RET)
    g = proj(3 * D_RET, D_RET)
    bg = proj(4 * D_RET, D_CONV)
    cg = proj(4 * D_RET + D_CONV, D_CONV)
    hc = proj(4 * D_RET + 2 * D_CONV, D_CONV)

    cos = cos_ref[...]
    sin = sin_ref[...]
    row = lax.broadcasted_iota(jnp.int32, (C, D_CONV), 0)
    k_scale = HEAD_DIM ** -0.5

    for b in range(nb):
        r0 = b * C
        for h in range(N_RET_HEADS):
            c0 = h * HEAD_DIM
            qh = q[r0:r0 + C, c0:c0 + HEAD_DIM]
            kh = k[r0:r0 + C, c0:c0 + HEAD_DIM]
            vh = v[r0:r0 + C, c0:c0 + HEAD_DIM]
            qh = qh * cos + pltpu.roll(qh, HEAD_DIM // 2, axis=1) * sin
            kh = (kh * cos + pltpu.roll(kh, HEAD_DIM // 2, axis=1) * sin) * k_scale
            qb = qh.astype(BF16)
            kb = kh.astype(BF16)
            vb = vh.astype(BF16)
            s_old = sret_out[b, h]
            scores = lax.dot_general(qb, kb, (((1,), (1,)), ((), ())), preferred_element_type=F32)
            scores = scores * decay_ref[h]
            intra = jnp.dot(scores.astype(BF16), vb, preferred_element_type=F32)
            cross = jnp.dot(qb, s_old.astype(BF16), preferred_element_type=F32) * qdec_ref[h]
            o = intra + cross
            kd = (kh * kdec_ref[h]).astype(BF16)
            s_new = sdec_ref[h] * s_old + lax.dot_general(
                kd, vb, (((0,), (0,)), ((), ())), preferred_element_type=F32)
            sret_out[b, h] = s_new
            mu = jnp.mean(o, axis=-1, keepdims=True)
            oc = o - mu
            var = jnp.mean(oc * oc, axis=-1, keepdims=True)
            on = oc * lax.rsqrt(var + GN_EPS)
            gh = g[r0:r0 + C, c0:c0 + HEAD_DIM]
            ret = on * (gh * jax.nn.sigmoid(gh)) * bret_ref[:, c0:c0 + HEAD_DIM]
            mix_ref[r0:r0 + C, c0:c0 + HEAD_DIM] = ret.astype(BF16)

        u = cg[r0:r0 + C] * hc[r0:r0 + C]
        prev = sconv_out[b]
        u1 = jnp.where(row == 0, prev[1:2], pltpu.roll(u, 1, axis=0))
        u2 = jnp.where(row == 0, prev[0:1], jnp.where(row == 1, prev[1:2], pltpu.roll(u, 2, axis=0)))
        z = convb_ref[...] + convw_ref[0:1] * u2 + convw_ref[1:2] * u1 + convw_ref[2:3] * u
        conv_out = bg[r0:r0 + C] * z * bconv_ref[...]
        mix_ref[r0:r0 + C, D_RET:D_RET + D_CONV] = conv_out.astype(BF16)
        sconv_out[b] = u[C - 2:C]

    m = jnp.dot(mix_ref[...], wout_ref[...], preferred_element_type=F32)
    hval = _layer_norm(DEEPNORM_ALPHA * x + m, g1_ref[...], b1_ref[...])
    _store_rows(h_ref, hval, nb * C)
    h_hi = hval.astype(BF16)
    h_lo = (hval - h_hi.astype(F32)).astype(BF16)
    nt_dot = lambda a, b: lax.dot_general(a, b, (((1,), (1,)), ((), ())), preferred_element_type=F32)
    logits_t = nt_dot(wr_ref[0], h_hi) + nt_dot(wr_ref[0], h_lo) + nt_dot(wr_ref[1], h_hi) + br_ref[...]
    choice_ref[...], gate_ref[...], rank_ref[...] = _route(logits_t, run_ref)
    count_ref[...] = run_ref[:, :LANES]


def _retention_tables(C, pos0, L):
    lg = jnp.log1p(-jnp.power(2.0, -5.0 - jnp.arange(N_RET_HEADS, dtype=F32)))
    i = jnp.arange(C, dtype=F32)
    diff = i[:, None] - i[None, :]
    decay = jnp.where(diff[None] >= 0, jnp.exp(lg[:, None, None] * jnp.maximum(diff, 0.0)[None]), 0.0)
    qdec = jnp.exp(lg[:, None] * (i + 1.0)[None, :])
    kdec = jnp.exp(lg[:, None] * (C - 1.0 - i)[None, :])
    sdec = jnp.exp(lg * C)
    qdec = jnp.broadcast_to(qdec[:, :, None], (N_RET_HEADS, C, HEAD_DIM))
    kdec = jnp.broadcast_to(kdec[:, :, None], (N_RET_HEADS, C, HEAD_DIM))
    sdec = jnp.broadcast_to(sdec[:, None, None], (N_RET_HEADS, 1, HEAD_DIM))
    pos = pos0 + jnp.arange(L, dtype=jnp.int32)
    inv_freq = ROPE_BASE ** (-jnp.arange(0, HEAD_DIM, 2, dtype=F32) / HEAD_DIM)
    ang = pos.astype(F32)[:, None] * inv_freq[None, :]
    cos = jnp.cos(ang)
    sin = jnp.sin(ang)
    cos_full = jnp.concatenate([cos, cos], axis=-1)
    sin_signed = jnp.concatenate([-sin, sin], axis=-1)
    return cos_full, sin_signed, decay, qdec, kdec, sdec


def _mixer(x, s_ret, s_conv, pos0, nb, C, w, tail=None):
    B, L, _ = x.shape
    nc = L // C
    nbs = B // nb
    rows = nb * C
    n_tokens = B * L
    n_tail = 0
    if tail is not None:
        n_tokens += tail[1].shape[1]
        n_tail = tail[1].shape[1] // rows
        assert tail[1].shape[1] % rows == 0 and 0 < n_tail <= nc
    cos, sin, decay, qdec, kdec, sdec = _retention_tables(C, pos0, L)

    bb = lambda b: jnp.minimum(b, nbs - 1)
    cc = lambda b, c: jnp.where(b < nbs, c, nc - 1)
    out_blk = lambda b, c: jnp.where(b < nbs, b * nc + c, nbs * nc + jnp.minimum(c, n_tail - 1))
    const2 = lambda b, c: (0, 0)
    const3 = lambda b, c: (0, 0, 0)
    full = lambda a: pl.BlockSpec(a.shape, const2 if a.ndim == 2 else const3)
    per_token = lambda imap: pl.BlockSpec((TOP_K, rows), lambda b, c: (0, imap(b, c)))
    in_arrays = [x, cos, sin, decay, qdec, kdec, sdec, w["w_in"], w["w_out"], w["conv_w"], w["conv_b"],
                 w["beta_ret"], w["beta_conv"], w["ln1_g"], w["ln1_b"], w["w_router"], w["b_router"],
                 s_ret, s_conv]
    assert len(in_arrays) == N_MIXER_INPUTS
    in_specs = [pl.BlockSpec((nb, C, D_MODEL), lambda b, c: (bb(b), cc(b, c), 0)),
                pl.BlockSpec((C, HEAD_DIM), lambda b, c: (cc(b, c), 0)),
                pl.BlockSpec((C, HEAD_DIM), lambda b, c: (cc(b, c), 0))]
    in_specs += [full(a) for a in in_arrays[3:17]]
    in_specs += [pl.BlockSpec((nb, N_RET_HEADS, HEAD_DIM, HEAD_DIM), lambda b, c: (bb(b), 0, 0, 0)),
                 pl.BlockSpec((nb, CONV_W - 1, D_CONV), lambda b, c: (bb(b), 0, 0))]
    if n_tail:
        tail_blk = lambda b, c: jnp.where(b < nbs, 0, jnp.minimum(c, n_tail - 1))
        in_arrays += list(tail)
        in_specs += [pl.BlockSpec((rows * SUBLANES, LANES), lambda b, c: (tail_blk(b, c), 0))]
        in_specs += [per_token(tail_blk)] * (N_TOKEN_OUTPUTS - 1)
    out_shape = [jax.ShapeDtypeStruct((n_tokens * SUBLANES, LANES), F32),
                 jax.ShapeDtypeStruct((TOP_K, n_tokens), jnp.int32),
                 jax.ShapeDtypeStruct((TOP_K, n_tokens), F32),
                 jax.ShapeDtypeStruct((TOP_K, n_tokens), jnp.int32),
                 jax.ShapeDtypeStruct((N_EXPERTS, LANES), F32),
                 jax.ShapeDtypeStruct((B, N_RET_HEADS, HEAD_DIM, HEAD_DIM), F32),
                 jax.ShapeDtypeStruct((B, CONV_W - 1, D_CONV), F32)]
    assert len(out_shape) == N_TOKEN_OUTPUTS + 3
    out_specs = [pl.BlockSpec((rows * SUBLANES, LANES), lambda b, c: (out_blk(b, c), 0))]
    out_specs += [per_token(out_blk)] * (N_TOKEN_OUTPUTS - 1)
    out_specs += [pl.BlockSpec((N_EXPERTS, LANES), const2),
                  pl.BlockSpec((nb, N_RET_HEADS, HEAD_DIM, HEAD_DIM), lambda b, c: (bb(b), 0, 0, 0)),
                  pl.BlockSpec((nb, CONV_W - 1, D_CONV), lambda b, c: (bb(b), 0, 0))]
    return pl.pallas_call(
        functools.partial(_mixer_body, nb=nb, C=C, n_batch_steps=nbs, n_tail=n_tail),
        grid=(nbs + (1 if n_tail else 0), nc), in_specs=in_specs, out_specs=out_specs, out_shape=out_shape,
        scratch_shapes=[pltpu.VMEM((rows, D_MODEL), BF16), pltpu.VMEM((N_EXPERTS, rows), F32)],
        compiler_params=pltpu.CompilerParams(
            dimension_semantics=("arbitrary", "arbitrary"), vmem_limit_bytes=VMEM_LIMIT_BYTES),
        name="mixer",
    )(*in_arrays)


def _lookup(table, choices):
    experts = jnp.arange(N_EXPERTS, dtype=jnp.int32)
    return jnp.sum(jnp.where(choices[..., None] == experts, table, 0), axis=-1)


def _routing_plan(choices_t, ranks_t, counts_first, counts_second, n_first, blk):
    T = choices_t.shape[1]
    A = T * TOP_K
    assert A < (1 << PAD_BIT) and N_EXPERTS * blk < (1 << PAD_BIT) and A % blk == 0
    n_blocks = A // blk + N_EXPERTS
    experts = jnp.arange(N_EXPERTS, dtype=jnp.int32)
    counts = counts_first + counts_second
    nblk_e = (counts + blk - 1) // blk
    bend = jnp.cumsum(nblk_e)
    n_valid = bend[-1]
    start = (bend - nblk_e) * blk
    token = jnp.arange(T, dtype=jnp.int32)[None, :]
    rank = ranks_t + jnp.where(token >= n_first, _lookup(counts_first, choices_t), 0)
    pos = _lookup(start, choices_t) + rank

    assign_id = (token * TOP_K + jnp.arange(TOP_K, dtype=jnp.int32)[:, None]).reshape(A)
    real_keys = (choices_t.reshape(A) << (PAD_BIT + 1)) | assign_id
    n_pad = nblk_e * blk - counts
    j = jnp.arange(blk, dtype=jnp.int32)[None, :]
    pad_keys = jnp.where(j < n_pad[:, None],
                         (experts[:, None] << (PAD_BIT + 1)) | (1 << PAD_BIT) | (experts[:, None] * blk + j),
                         UNUSED_KEY)
    slots = jnp.sort(jnp.concatenate([real_keys, pad_keys.reshape(-1)]))
    ident = slots & ((1 << PAD_BIT) - 1)
    is_real = (((slots >> PAD_BIT) & 1) == 0) & (slots < UNUSED_KEY)
    dst = jnp.where(is_real, (ident & (TOP_K - 1)) * T + (ident >> 2),
                    jnp.where(slots < UNUSED_KEY, A + ident, A))
    bi = jnp.arange(n_blocks, dtype=jnp.int32)
    block_e = slots.reshape(n_blocks, blk)[:, 0] >> (PAD_BIT + 1)
    block_e = jnp.where(bi < n_valid, block_e, block_e[n_valid - 1])
    last_block = jnp.where(nblk_e > 0, bend - 1, -1)
    unused = n_valid + experts
    unused = jnp.where(unused < n_blocks, unused, -1)
    zero_blocks = jnp.concatenate([last_block, unused]).astype(jnp.int32)
    return pos, dst.reshape(n_blocks, blk), block_e, n_valid.reshape(1).astype(jnp.int32), zero_blocks


def _dispatch_body(zero_ref, pos_ref, h_ref, xs_hbm, hbuf, sem, *, tt, blk):
    i = pl.program_id(0)
    n = pl.num_programs(0)
    slot = i % 2

    def wait_rows(s):
        for _ in range(TOP_K):
            pltpu.make_async_copy(hbuf.at[s], xs_hbm.at[pl.ds(0, tt * SUBLANES)], sem.at[s]).wait()

    @pl.when(i == 0)
    def _():
        hbuf[1] = jnp.zeros(hbuf.shape[1:], F32)
        zeros = hbuf.at[1, pl.ds(0, blk * SUBLANES)]
        for e in range(zero_ref.shape[0]):
            @pl.when(zero_ref[e] >= 0)
            def _():
                first_row = pl.multiple_of(zero_ref[e] * (blk * SUBLANES), blk * SUBLANES)
                pltpu.make_async_copy(zeros, xs_hbm.at[pl.ds(first_row, blk * SUBLANES)], sem.at[1]).start()
        for e in range(zero_ref.shape[0]):
            @pl.when(zero_ref[e] >= 0)
            def _():
                pltpu.make_async_copy(zeros, xs_hbm.at[pl.ds(0, blk * SUBLANES)], sem.at[1]).wait()

    @pl.when(i >= 2)
    def _():
        wait_rows(slot)

    hbuf[slot] = h_ref[...]
    for kk in range(TOP_K):
        def body(gi, carry):
            for u in range(DMA_UNROLL):
                t = gi * DMA_UNROLL + u
                pltpu.make_async_copy(hbuf.at[slot, _row_tile(t)],
                                      xs_hbm.at[_row_tile(pos_ref[0, 0, kk * tt + t])],
                                      sem.at[slot]).start(u % 2)
            return carry
        lax.fori_loop(0, tt // DMA_UNROLL, body, 0)

    @pl.when(i == n - 1)
    def _():
        wait_rows(slot)

        @pl.when(i >= 1)
        def _():
            wait_rows(1 - slot)


def _dispatch(h_all, pos, zero_blocks, n_slots, tt, blk):
    T = pos.shape[1]
    n_tiles = T // tt
    assert T % tt == 0 and blk <= tt
    pos3 = pos.reshape(TOP_K, n_tiles, tt).transpose(1, 0, 2).reshape(n_tiles, 1, TOP_K * tt)
    grid_spec = pltpu.PrefetchScalarGridSpec(
        num_scalar_prefetch=1,
        grid=(n_tiles,),
        in_specs=[pl.BlockSpec((1, 1, TOP_K * tt), lambda i, zb: (i, 0, 0), memory_space=pltpu.SMEM),
                  pl.BlockSpec((tt * SUBLANES, LANES), lambda i, zb: (i, 0))],
        out_specs=pl.BlockSpec(memory_space=pl.ANY),
        scratch_shapes=[pltpu.VMEM((2, tt * SUBLANES, LANES), F32), pltpu.SemaphoreType.DMA((2,))],
    )
    return pl.pallas_call(
        functools.partial(_dispatch_body, tt=tt, blk=blk),
        grid_spec=grid_spec,
        out_shape=jax.ShapeDtypeStruct((n_slots * SUBLANES, LANES), F32),
        compiler_params=pltpu.CompilerParams(
            dimension_semantics=("arbitrary",), vmem_limit_bytes=VMEM_LIMIT_BYTES),
        name="dispatch",
    )(zero_blocks, pos3, h_all)


def _expert_body(be_ref, nv_ref, dst_ref, x_ref, wgu_ref, bgu_ref, wd_ref, bd_ref,
                 y_hbm, ybuf, osem, wgu_bf, wd_bf, xb_ref, act_ref, *, blk, n_real_rows):
    i = pl.program_id(0)
    n_valid = nv_ref[0]

    def scatter_row(s, r, prio=0):
        pltpu.make_async_copy(
            ybuf.at[s, _row_tile(r)], y_hbm.at[_row_tile(dst_ref[0, 0, r])], osem.at[s]).start(prio)

    def wait_scatter(s):
        pltpu.make_async_copy(ybuf.at[s], y_hbm.at[pl.ds(0, blk * SUBLANES)], osem.at[s]).wait()

    @pl.when(i == 0)
    def _():
        ybuf[...] = jnp.zeros(ybuf.shape, F32)
        for e in range(N_EXPERTS):
            pltpu.make_async_copy(
                ybuf.at[0], y_hbm.at[pl.ds((n_real_rows + e * blk) * SUBLANES, blk * SUBLANES)],
                osem.at[0]).start()
        for e in range(N_EXPERTS):
            wait_scatter(0)

    @pl.when((i < n_valid) & ((i == 0) | (be_ref[i] != be_ref[jnp.maximum(i - 1, 0)])))
    def _():
        wgu_bf[...] = wgu_ref[0].astype(BF16)
        wd_bf[...] = wd_ref[0].astype(BF16)

    rows_per_chunk = blk // N_CHUNKS
    cols = D_FF // N_CHUNKS

    def block_step(s):
        @pl.when(i < n_valid)
        def _():
            @pl.when(i >= 1)
            def _():
                wait_scatter(s)

            xb_ref[...] = _load_rows(x_ref, blk).astype(BF16)
            for c in range(N_CHUNKS):
                lo = c * cols
                xb = xb_ref[...]
                gate = jnp.dot(xb, wgu_bf[:, lo:lo + cols], preferred_element_type=F32)
                up = jnp.dot(xb, wgu_bf[:, D_FF + lo:D_FF + lo + cols], preferred_element_type=F32)
                gate = jnp.minimum(gate + bgu_ref[0, :, lo:lo + cols], SWIGLU_LIMIT)
                up = jnp.clip(up + bgu_ref[0, :, D_FF + lo:D_FF + lo + cols], -SWIGLU_LIMIT, SWIGLU_LIMIT)
                act = (up + 1.0) * gate * jax.nn.sigmoid(SWIGLU_ALPHA * gate)
                act_ref[:, lo:lo + cols] = act.astype(BF16)
                for r in range(c * rows_per_chunk, (c + 1) * rows_per_chunk):
                    scatter_row(1 - s, r, r % 2)
            y = jnp.dot(act_ref[...], wd_bf[...], preferred_element_type=F32) + bd_ref[0]
            _store_rows(ybuf.at[s], y, blk)

        @pl.when(i == n_valid)
        def _():
            wait_scatter(s)

            def body(gi, carry):
                for u in range(DMA_UNROLL):
                    scatter_row(1 - s, gi * DMA_UNROLL + u)
                return carry
            lax.fori_loop(0, blk // DMA_UNROLL, body, 0)
            wait_scatter(1 - s)

    for s in range(2):
        pl.when(i % 2 == s)(functools.partial(block_step, s))


def _experts(x_slots, dst, block_e, n_valid, w_gu, b_gu, w_down, b_down, blk, n_real_rows):
    n_blocks = block_e.shape[0]
    assert n_blocks * blk == n_real_rows + N_EXPERTS * blk and blk % (2 * N_CHUNKS) == 0
    n_out_rows = n_real_rows + (N_EXPERTS + 1) * blk
    priming = n_real_rows + N_EXPERTS * blk + jnp.arange(blk, dtype=jnp.int32)
    dst3 = jnp.concatenate([priming[None], dst], axis=0).reshape(n_blocks + 1, 1, blk)
    used = lambda i, nv: jnp.minimum(i, nv[0] - 1)
    grid_spec = pltpu.PrefetchScalarGridSpec(
        num_scalar_prefetch=2,
        grid=(n_blocks,),
        in_specs=[
            pl.BlockSpec((1, 1, blk), lambda i, be, nv: (i, 0, 0), memory_space=pltpu.SMEM),
            pl.BlockSpec((blk * SUBLANES, LANES), lambda i, be, nv: (used(i, nv), 0)),
            pl.BlockSpec((1, D_MODEL, 2 * D_FF), lambda i, be, nv: (be[i], 0, 0)),
            pl.BlockSpec((1, 1, 2 * D_FF), lambda i, be, nv: (be[i], 0, 0)),
            pl.BlockSpec((1, D_FF, D_MODEL), lambda i, be, nv: (be[i], 0, 0)),
            pl.BlockSpec((1, 1, D_MODEL), lambda i, be, nv: (be[i], 0, 0)),
        ],
        out_specs=pl.BlockSpec(memory_space=pl.ANY),
        scratch_shapes=[pltpu.VMEM((2, blk * SUBLANES, LANES), F32),
                        pltpu.SemaphoreType.DMA((2,)),
                        pltpu.VMEM((D_MODEL, 2 * D_FF), BF16),
                        pltpu.VMEM((D_FF, D_MODEL), BF16),
                        pltpu.VMEM((blk, D_MODEL), BF16),
                        pltpu.VMEM((blk, D_FF), BF16)],
    )
    return pl.pallas_call(
        functools.partial(_expert_body, blk=blk, n_real_rows=n_real_rows),
        grid_spec=grid_spec,
        out_shape=jax.ShapeDtypeStruct((n_out_rows * SUBLANES, LANES), F32),
        compiler_params=pltpu.CompilerParams(
            dimension_semantics=("arbitrary",), vmem_limit_bytes=VMEM_LIMIT_BYTES),
        name="experts",
    )(block_e, n_valid, dst3, x_slots, w_gu, b_gu.reshape(N_EXPERTS, 1, 2 * D_FF),
      w_down, b_down.reshape(N_EXPERTS, 1, D_MODEL))


def _combine_body(*refs, tt, n_first):
    y_refs = refs[:TOP_K]
    h_ref, gate_ref, g2_ref, b2_ref, out_first, out_second = refs[TOP_K:]
    i = pl.program_id(0)
    gates = gate_ref[...]
    f = gates[:, 0:1] * _load_rows(y_refs[0], tt)
    for kk in range(1, TOP_K):
        f = f + gates[:, kk:kk + 1] * _load_rows(y_refs[kk], tt)
    out = _layer_norm(DEEPNORM_ALPHA * _load_rows(h_ref, tt) + f, g2_ref[...], b2_ref[...])

    @pl.when(i < n_first)
    def _():
        out_first[...] = out

    @pl.when(i >= n_first)
    def _():
        out_second[...] = out


def _combine(y_rows, gates, h_all, ln2_g, ln2_b, tt, n_tokens_first):
    T = gates.shape[0]
    n_tiles = T // tt
    n_first = n_tokens_first // tt
    assert T % tt == 0 and n_tokens_first % tt == 0 and 0 < n_first < n_tiles
    row_blk = lambda imap: pl.BlockSpec((tt * SUBLANES, LANES), imap)
    y_specs = [row_blk(functools.partial(lambda i, kk: (kk * n_tiles + i, 0), kk=kk)) for kk in range(TOP_K)]
    return pl.pallas_call(
        functools.partial(_combine_body, tt=tt, n_first=n_first),
        grid=(n_tiles,),
        in_specs=y_specs + [
            row_blk(lambda i: (i, 0)),
            pl.BlockSpec((tt, TOP_K), lambda i: (i, 0)),
            pl.BlockSpec((1, D_MODEL), lambda i: (0, 0)),
            pl.BlockSpec((1, D_MODEL), lambda i: (0, 0)),
        ],
        out_specs=[pl.BlockSpec((tt, D_MODEL), lambda i: (jnp.minimum(i, n_first - 1), 0)),
                   pl.BlockSpec((tt, D_MODEL), lambda i: (jnp.maximum(i - n_first, 0), 0))],
        out_shape=[jax.ShapeDtypeStruct((n_tokens_first, D_MODEL), F32),
                   jax.ShapeDtypeStruct((T - n_tokens_first, D_MODEL), F32)],
        compiler_params=pltpu.CompilerParams(
            dimension_semantics=("arbitrary",), vmem_limit_bytes=VMEM_LIMIT_BYTES),
        name="combine",
    )(*([y_rows] * TOP_K), h_all, gates, ln2_g, ln2_b)


def kernel(x_prompt, x_sample, state_retention, state_conv, w_in, beta_ret, beta_conv, conv_w, conv_b,
           w_out, ln1_g, ln1_b, w_router, b_router, w_gu, b_gu, w_down, b_down, ln2_g, ln2_b):
    assert w_in.shape[0] == DEPTH == 1
    B, S, _ = x_prompt.shape
    Bd, Ld, _ = x_sample.shape
    Tp, Ts = B * S, Bd * Ld
    T = Tp + Ts
    wr_t = w_router[0].T
    wr_hi = wr_t.astype(BF16)
    wr_lo = (wr_t - wr_hi.astype(F32)).astype(BF16)
    w = dict(w_in=w_in[0].astype(BF16), w_out=w_out[0].astype(BF16), conv_w=conv_w[0],
             conv_b=conv_b[0][None], beta_ret=beta_ret[0][None], beta_conv=beta_conv[0][None],
             ln1_g=ln1_g[0][None], ln1_b=ln1_b[0][None], w_router=jnp.stack([wr_hi, wr_lo]),
             b_router=b_router[0][:, None])

    s_ret0 = jnp.zeros((B, N_RET_HEADS, HEAD_DIM, HEAD_DIM), F32)
    s_conv0 = jnp.zeros((B, CONV_W - 1, D_CONV), F32)
    h_s, choices_s, gates_s, ranks_s, counts_s, sret_s, sconv_s = _mixer(
        x_sample, state_retention[0], state_conv[0], PAST_LEN, SAMPLE_BATCH_BLOCK, Ld, w)
    h_all, choices_t, gates_t, ranks_t, counts_p, sret_p, sconv_p = _mixer(
        x_prompt, s_ret0, s_conv0, 0, 1, PROMPT_CHUNK, w, tail=(h_s, choices_s, gates_s, ranks_s))

    pos, dst, block_e, n_valid, zero_blocks = _routing_plan(
        choices_t, ranks_t, counts_p[:, 0].astype(jnp.int32), counts_s[:, 0].astype(jnp.int32), Tp,
        EXPERT_BLOCK)
    n_slots = block_e.shape[0] * EXPERT_BLOCK
    x_slots = _dispatch(h_all, pos, zero_blocks, n_slots, DISPATCH_TILE, EXPERT_BLOCK)
    y_rows = _experts(x_slots, dst, block_e, n_valid, w_gu[0], b_gu[0], w_down[0], b_down[0],
                      EXPERT_BLOCK, T * TOP_K)
    y_p, y_s = _combine(y_rows, gates_t.T, h_all, ln2_g[0][None], ln2_b[0][None], COMBINE_TILE, Tp)

    return (y_p.reshape(B, S, D_MODEL), y_s.reshape(Bd, Ld, D_MODEL),
            sret_p[None], sconv_p[None], sret_s[None], sconv_s[None])
```

```python
import functools

import jax
import jax.numpy as jnp
from jax import lax
from jax.experimental import pallas as pl
from jax.experimental.pallas import tpu as pltpu

D_MODEL = 1024
N_RET_HEADS = 4
HEAD_DIM = 128
D_RET = N_RET_HEADS * HEAD_DIM
D_CONV = D_MODEL - D_RET
CONV_W = 3
N_EXPERTS = 32
TOP_K = 4
D_FF = D_MODEL
SWIGLU_LIMIT = 7.0
SWIGLU_ALPHA = 1.702
ROPE_BASE = 10000.0
LN_EPS = 1e-5
GN_EPS = 1e-5
DEPTH = 1
PAST_LEN = 1024
DEEPNORM_ALPHA = (2.0 * DEPTH) ** 0.25

LANES = 128
SUBLANES = 8
ROW_CHUNKS = D_MODEL // LANES
assert ROW_CHUNKS == SUBLANES

PROMPT_CHUNK = 256
SAMPLE_BATCH_BLOCK = 8
EXPERT_BLOCK = 512
DISPATCH_TILE = 512
COMBINE_TILE = 256
DMA_UNROLL = 8
N_CHUNKS = 4
VMEM_LIMIT_BYTES = 56 * 1024 * 1024

F32 = jnp.float32
BF16 = jnp.bfloat16


def _layer_norm(x, g, b):
    mu = jnp.mean(x, axis=-1, keepdims=True)
    xc = x - mu
    var = jnp.mean(xc * xc, axis=-1, keepdims=True)
    return xc * lax.rsqrt(var + LN_EPS) * g + b


def _load_rows(ref, n_rows):
    return jnp.concatenate(
        [ref[pl.ds(j, n_rows, stride=SUBLANES), :] for j in range(ROW_CHUNKS)], axis=1)


def _store_rows(ref, val, n_rows):
    for j in range(ROW_CHUNKS):
        ref[pl.ds(j, n_rows, stride=SUBLANES), :] = val[:, j * LANES:(j + 1) * LANES]


def _row_tile(r):
    if isinstance(r, int):
        return pl.ds(r * SUBLANES, SUBLANES)
    return pl.ds(pl.multiple_of(r * SUBLANES, SUBLANES), SUBLANES)


N_MIXER_INPUTS = 19
N_TOKEN_OUTPUTS = 4


def _mixer_body(*refs, nb, C, n_batch_steps, n_tail):
    if not n_tail:
        _mixer_compute(*refs, nb=nb, C=C)
        return
    tails = refs[N_MIXER_INPUTS:N_MIXER_INPUTS + N_TOKEN_OUTPUTS]
    outs = refs[N_MIXER_INPUTS + N_TOKEN_OUTPUTS:N_MIXER_INPUTS + 2 * N_TOKEN_OUTPUTS]
    bstep = pl.program_id(0)

    @pl.when(bstep < n_batch_steps)
    def _():
        _mixer_compute(*refs[:N_MIXER_INPUTS], *refs[N_MIXER_INPUTS + N_TOKEN_OUTPUTS:], nb=nb, C=C)

    @pl.when((bstep == n_batch_steps) & (pl.program_id(1) < n_tail))
    def _():
        for tail_ref, out_ref in zip(tails, outs):
            out_ref[...] = tail_ref[...]


def _route(logits_t, run_ref):
    n_tok = logits_t.shape[1]
    expert = lax.broadcasted_iota(jnp.int32, logits_t.shape, 0)
    vals, idxs, hots = [], [], []
    for _ in range(TOP_K):
        m = jnp.max(logits_t, axis=0, keepdims=True)
        idx = jnp.min(jnp.where(logits_t == m, expert, N_EXPERTS), axis=0, keepdims=True)
        hot = expert == idx
        vals.append(m)
        idxs.append(idx)
        hots.append(hot)
        logits_t = jnp.where(hot, -jnp.inf, logits_t)
    ex = [jnp.exp(v - vals[0]) for v in vals]
    total = ex[0]
    for e in ex[1:]:
        total = total + e
    gates = jnp.concatenate([e / total for e in ex], axis=0)

    chosen = hots[0].astype(F32)
    for hot in hots[1:]:
        chosen = chosen + hot.astype(F32)
    chosen = chosen.astype(BF16)
    t_row = lax.broadcasted_iota(jnp.int32, (n_tok, n_tok), 0)
    t_col = lax.broadcasted_iota(jnp.int32, (n_tok, n_tok), 1)
    earlier = jnp.where(t_row < t_col, 1.0, 0.0).astype(BF16)
    before = run_ref[...] + jnp.dot(chosen, earlier, preferred_element_type=F32)
    ranks = [jnp.sum(jnp.where(hot, before, 0.0), axis=0, keepdims=True) for hot in hots]
    run_ref[...] = run_ref[...] + jnp.dot(chosen, jnp.ones((n_tok, n_tok), BF16), preferred_element_type=F32)
    return jnp.concatenate(idxs, axis=0), gates, jnp.concatenate(ranks, axis=0).astype(jnp.int32)


def _mixer_compute(x_ref, cos_ref, sin_ref, decay_ref, qdec_ref, kdec_ref, sdec_ref, win_ref, wout_ref,
                   convw_ref, convb_ref, bret_ref, bconv_ref, g1_ref, b1_ref, wr_ref, br_ref,
                   sret_ref, sconv_ref, h_ref, choice_ref, gate_ref, rank_ref, count_ref, sret_out, sconv_out,
                   mix_ref, run_ref, *, nb, C):
    c = pl.program_id(1)

    @pl.when((pl.program_id(0) == 0) & (c == 0))
    def _():
        run_ref[...] = jnp.zeros(run_ref.shape, F32)

    @pl.when(c == 0)
    def _():
        sret_out[...] = sret_ref[...]
        sconv_out[...] = sconv_ref[...]

    x = x_ref[...].reshape(nb * C, D_MODEL)
    xb = x.astype(BF16)

    def proj(col0, width):
        return jnp.dot(xb, win_ref[:, col0:col0 + width], preferred_element_type=F32)

    q = proj(0, D_RET)
    k = proj(D_RET, D_RET)
    v = proj(2 * D_RET, D_RET)
    g = proj(3 * D_RET, D_RET)
    bg = proj(4 * D_RET, D_CONV)
    cg = proj(4 * D_RET + D_CONV, D_CONV)
    hc = proj(4 * D_RET + 2 * D_CONV, D_CONV)

    cos = cos_ref[...]
    sin = sin_ref[...]
    row = lax.broadcasted_iota(jnp.int32, (C, D_CONV), 0)
    k_scale = HEAD_DIM ** -0.5

    for b in range(nb):
        r0 = b * C
        for h in range(N_RET_HEADS):
            c0 = h * HEAD_DIM
            qh = q[r0:r0 + C, c0:c0 + HEAD_DIM]
            kh = k[r0:r0 + C, c0:c0 + HEAD_DIM]
            vh = v[r0:r0 + C, c0:c0 + HEAD_DIM]
            qh = qh * cos + pltpu.roll(qh, HEAD_DIM // 2, axis=1) * sin
            kh = (kh * cos + pltpu.roll(kh, HEAD_DIM // 2, axis=1) * sin) * k_scale
            qb = qh.astype(BF16)
            kb = kh.astype(BF16)
            vb = vh.astype(BF16)
            s_old = sret_out[b, h]
            scores = lax.dot_general(qb, kb, (((1,), (1,)), ((), ())), preferred_element_type=F32)
            scores = scores * decay_ref[h]
            intra = jnp.dot(scores.astype(BF16), vb, preferred_element_type=F32)
            cross = jnp.dot(qb, s_old.astype(BF16), preferred_element_type=F32) * qdec_ref[h]
            o = intra + cross
            kd = (kh * kdec_ref[h]).astype(BF16)
            s_new = sdec_ref[h] * s_old + lax.dot_general(
                kd, vb, (((0,), (0,)), ((), ())), preferred_element_type=F32)
            sret_out[b, h] = s_new
            mu = jnp.mean(o, axis=-1, keepdims=True)
            oc = o - mu
            var = jnp.mean(oc * oc, axis=-1, keepdims=True)
            on = oc * lax.rsqrt(var + GN_EPS)
            gh = g[r0:r0 + C, c0:c0 + HEAD_DIM]
            ret = on * (gh * jax.nn.sigmoid(gh)) * bret_ref[:, c0:c0 + HEAD_DIM]
            mix_ref[r0:r0 + C, c0:c0 + HEAD_DIM] = ret.astype(BF16)

        u = cg[r0:r0 + C] * hc[r0:r0 + C]
        prev = sconv_out[b]
        u1 = jnp.where(row == 0, prev[1:2], pltpu.roll(u, 1, axis=0))
        u2 = jnp.where(row == 0, prev[0:1], jnp.where(row == 1, prev[1:2], pltpu.roll(u, 2, axis=0)))
        z = convb_ref[...] + convw_ref[0:1] * u2 + convw_ref[1:2] * u1 + convw_ref[2:3] * u
        conv_out = bg[r0:r0 + C] * z * bconv_ref[...]
        mix_ref[r0:r0 + C, D_RET:D_RET + D_CONV] = conv_out.astype(BF16)
        sconv_out[b] = u[C - 2:C]

    m = jnp.dot(mix_ref[...], wout_ref[...], preferred_element_type=F32)
    hval = _layer_norm(DEEPNORM_ALPHA * x + m, g1_ref[...], b1_ref[...])
    _store_rows(h_ref, hval, nb * C)
    h_hi = hval.astype(BF16)
    h_lo = (hval - h_hi.astype(F32)).astype(BF16)
    nt_dot = lambda a, b: lax.dot_general(a, b, (((1,), (1,)), ((), ())), preferred_element_type=F32)
    logits_t = nt_dot(wr_ref[0], h_hi) + nt_dot(wr_ref[0], h_lo) + nt_dot(wr_ref[1], h_hi) + br_ref[...]
    choice_ref[...], gate_ref[...], rank_ref[...] = _route(logits_t, run_ref)
    count_ref[...] = run_ref[:, :LANES]


def _retention_tables(C, pos0, L):
    lg = jnp.log1p(-jnp.power(2.0, -5.0 - jnp.arange(N_RET_HEADS, dtype=F32)))
    i = jnp.arange(C, dtype=F32)
    diff = i[:, None] - i[None, :]
    decay = jnp.where(diff[None] >= 0, jnp.exp(lg[:, None, None] * jnp.maximum(diff, 0.0)[None]), 0.0)
    qdec = jnp.exp(lg[:, None] * (i + 1.0)[None, :])
    kdec = jnp.exp(lg[:, None] * (C - 1.0 - i)[None, :])
    sdec = jnp.exp(lg * C)
    qdec = jnp.broadcast_to(qdec[:, :, None], (N_RET_HEADS, C, HEAD_DIM))
    kdec = jnp.broadcast_to(kdec[:, :, None], (N_RET_HEADS, C, HEAD_DIM))
    sdec = jnp.broadcast_to(sdec[:, None, None], (N_RET_HEADS, 1, HEAD_DIM))
    pos = pos0 + jnp.arange(L, dtype=jnp.int32)
    inv_freq = ROPE_BASE ** (-jnp.arange(0, HEAD_DIM, 2, dtype=F32) / HEAD_DIM)
    ang = pos.astype(F32)[:, None] * inv_freq[None, :]
    cos = jnp.cos(ang)
    sin = jnp.sin(ang)
    cos_full = jnp.concatenate([cos, cos], axis=-1)
    sin_signed = jnp.concatenate([-sin, sin], axis=-1)
    return cos_full, sin_signed, decay, qdec, kdec, sdec


def _mixer(x, s_ret, s_conv, pos0, nb, C, w, tail=None):
    B, L, _ = x.shape
    nc = L // C
    nbs = B // nb
    rows = nb * C
    n_tokens = B * L
    n_tail = 0
    if tail is not None:
        n_tokens += tail[1].shape[1]
        n_tail = tail[1].shape[1] // rows
        assert tail[1].shape[1] % rows == 0 and 0 < n_tail <= nc
    cos, sin, decay, qdec, kdec, sdec = _retention_tables(C, pos0, L)

    bb = lambda b: jnp.minimum(b, nbs - 1)
    cc = lambda b, c: jnp.where(b < nbs, c, nc - 1)
    out_blk = lambda b, c: jnp.where(b < nbs, b * nc + c, nbs * nc + jnp.minimum(c, n_tail - 1))
    const2 = lambda b, c: (0, 0)
    const3 = lambda b, c: (0, 0, 0)
    full = lambda a: pl.BlockSpec(a.shape, const2 if a.ndim == 2 else const3)
    per_token = lambda imap: pl.BlockSpec((TOP_K, rows), lambda b, c: (0, imap(b, c)))
    in_arrays = [x, cos, sin, decay, qdec, kdec, sdec, w["w_in"], w["w_out"], w["conv_w"], w["conv_b"],
                 w["beta_ret"], w["beta_conv"], w["ln1_g"], w["ln1_b"], w["w_router"], w["b_router"],
                 s_ret, s_conv]
    assert len(in_arrays) == N_MIXER_INPUTS
    in_specs = [pl.BlockSpec((nb, C, D_MODEL), lambda b, c: (bb(b), cc(b, c), 0)),
                pl.BlockSpec((C, HEAD_DIM), lambda b, c: (cc(b, c), 0)),
                pl.BlockSpec((C, HEAD_DIM), lambda b, c: (cc(b, c), 0))]
    in_specs += [full(a) for a in in_arrays[3:17]]
    in_specs += [pl.BlockSpec((nb, N_RET_HEADS, HEAD_DIM, HEAD_DIM), lambda b, c: (bb(b), 0, 0, 0)),
                 pl.BlockSpec((nb, CONV_W - 1, D_CONV), lambda b, c: (bb(b), 0, 0))]
    if n_tail:
        tail_blk = lambda b, c: jnp.where(b < nbs, 0, jnp.minimum(c, n_tail - 1))
        in_arrays += list(tail)
        in_specs += [pl.BlockSpec((rows * SUBLANES, LANES), lambda b, c: (tail_blk(b, c), 0))]
        in_specs += [per_token(tail_blk)] * (N_TOKEN_OUTPUTS - 1)
    out_shape = [jax.ShapeDtypeStruct((n_tokens * SUBLANES, LANES), F32),
                 jax.ShapeDtypeStruct((TOP_K, n_tokens), jnp.int32),
                 jax.ShapeDtypeStruct((TOP_K, n_tokens), F32),
                 jax.ShapeDtypeStruct((TOP_K, n_tokens), jnp.int32),
                 jax.ShapeDtypeStruct((N_EXPERTS, LANES), F32),
                 jax.ShapeDtypeStruct((B, N_RET_HEADS, HEAD_DIM, HEAD_DIM), F32),
                 jax.ShapeDtypeStruct((B, CONV_W - 1, D_CONV), F32)]
    assert len(out_shape) == N_TOKEN_OUTPUTS + 3
    out_specs = [pl.BlockSpec((rows * SUBLANES, LANES), lambda b, c: (out_blk(b, c), 0))]
    out_specs += [per_token(out_blk)] * (N_TOKEN_OUTPUTS - 1)
    out_specs += [pl.BlockSpec((N_EXPERTS, LANES), const2),
                  pl.BlockSpec((nb, N_RET_HEADS, HEAD_DIM, HEAD_DIM), lambda b, c: (bb(b), 0, 0, 0)),
                  pl.BlockSpec((nb, CONV_W - 1, D_CONV), lambda b, c: (bb(b), 0, 0))]
    return pl.pallas_call(
        functools.partial(_mixer_body, nb=nb, C=C, n_batch_steps=nbs, n_tail=n_tail),
        grid=(nbs + (1 if n_tail else 0), nc), in_specs=in_specs, out_specs=out_specs, out_shape=out_shape,
        scratch_shapes=[pltpu.VMEM((rows, D_MODEL), BF16), pltpu.VMEM((N_EXPERTS, rows), F32)],
        compiler_params=pltpu.CompilerParams(
            dimension_semantics=("arbitrary", "arbitrary"), vmem_limit_bytes=VMEM_LIMIT_BYTES),
        name="mixer",
    )(*in_arrays)


def _lookup(table, choices):
    experts = jnp.arange(N_EXPERTS, dtype=jnp.int32)
    return jnp.sum(jnp.where(choices[..., None] == experts, table, 0), axis=-1)


def _routing_plan(choices_t, ranks_t, counts_first, counts_second, n_first, blk):
    T = choices_t.shape[1]
    A = T * TOP_K
    assert A % blk == 0
    n_blocks = A // blk + N_EXPERTS
    experts = jnp.arange(N_EXPERTS, dtype=jnp.int32)
    counts = counts_first + counts_second
    nblk_e = (counts + blk - 1) // blk
    bend = jnp.cumsum(nblk_e)
    n_valid = bend[-1]
    start = (bend - nblk_e) * blk
    token = jnp.arange(T, dtype=jnp.int32)[None, :]
    rank = ranks_t + jnp.where(token >= n_first, _lookup(counts_first, choices_t), 0)
    pos = _lookup(start, choices_t) + rank

    bi = jnp.arange(n_blocks, dtype=jnp.int32)
    block_e = jnp.sum((bi[:, None] >= bend[None, :]).astype(jnp.int32), axis=1)
    block_e = jnp.where(bi < n_valid, block_e, block_e[n_valid - 1])
    pad_dst = (A + block_e[:, None] * blk + jnp.arange(blk, dtype=jnp.int32)[None, :]).reshape(n_blocks * blk)
    last_block = jnp.where(nblk_e > 0, bend - 1, -1)
    unused = n_valid + experts
    unused = jnp.where(unused < n_blocks, unused, -1)
    zero_blocks = jnp.concatenate([last_block, unused]).astype(jnp.int32)
    return pos, pad_dst, block_e, n_valid.reshape(1).astype(jnp.int32), zero_blocks


def _dispatch_body(zero_ref, pos_ref, h_ref, pad_dst_hbm, xs_hbm, dst_hbm, hbuf, dst_tab, sem, tab_sem,
                   *, tt, blk, n_tokens):
    i = pl.program_id(0)
    n = pl.num_programs(0)
    slot = i % 2

    def wait_rows(s):
        for _ in range(TOP_K):
            pltpu.make_async_copy(hbuf.at[s], xs_hbm.at[pl.ds(0, tt * SUBLANES)], sem.at[s]).wait()

    @pl.when(i == 0)
    def _():
        load_table = pltpu.make_async_copy(pad_dst_hbm, dst_tab, tab_sem)
        load_table.start()
        load_table.wait()
        hbuf[1] = jnp.zeros(hbuf.shape[1:], F32)
        zeros = hbuf.at[1, pl.ds(0, blk * SUBLANES)]
        for e in range(zero_ref.shape[0]):
            @pl.when(zero_ref[e] >= 0)
            def _():
                first_row = pl.multiple_of(zero_ref[e] * (blk * SUBLANES), blk * SUBLANES)
                pltpu.make_async_copy(zeros, xs_hbm.at[pl.ds(first_row, blk * SUBLANES)], sem.at[1]).start()
        for e in range(zero_ref.shape[0]):
            @pl.when(zero_ref[e] >= 0)
            def _():
                pltpu.make_async_copy(zeros, xs_hbm.at[pl.ds(0, blk * SUBLANES)], sem.at[1]).wait()

    @pl.when(i >= 2)
    def _():
        wait_rows(slot)

    hbuf[slot] = h_ref[...]
    tokens_per_trip = DMA_UNROLL // TOP_K

    def body(gi, carry):
        for u in range(tokens_per_trip):
            t = gi * tokens_per_trip + u
            row = hbuf.at[slot, _row_tile(t)]
            for kk in range(TOP_K):
                p = pos_ref[0, 0, t * TOP_K + kk]
                dst_tab[p] = kk * n_tokens + i * tt + t
                pltpu.make_async_copy(row, xs_hbm.at[_row_tile(p)], sem.at[slot]).start(kk % 2)
        return carry
    lax.fori_loop(0, tt // tokens_per_trip, body, 0)

    @pl.when(i == n - 1)
    def _():
        wait_rows(slot)

        @pl.when(i >= 1)
        def _():
            wait_rows(1 - slot)

        store_table = pltpu.make_async_copy(dst_tab, dst_hbm, tab_sem)
        store_table.start()
        store_table.wait()


def _dispatch(h_all, pos, pad_dst, zero_blocks, tt, blk):
    T = pos.shape[1]
    n_slots = pad_dst.shape[0]
    n_tiles = T // tt
    assert T % tt == 0 and blk <= tt and DMA_UNROLL % TOP_K == 0
    pos3 = pos.T.reshape(n_tiles, 1, tt * TOP_K)
    grid_spec = pltpu.PrefetchScalarGridSpec(
        num_scalar_prefetch=1,
        grid=(n_tiles,),
        in_specs=[pl.BlockSpec((1, 1, tt * TOP_K), lambda i, zb: (i, 0, 0), memory_space=pltpu.SMEM),
                  pl.BlockSpec((tt * SUBLANES, LANES), lambda i, zb: (i, 0)),
                  pl.BlockSpec(memory_space=pl.ANY)],
        out_specs=[pl.BlockSpec(memory_space=pl.ANY), pl.BlockSpec(memory_space=pl.ANY)],
        scratch_shapes=[pltpu.VMEM((2, tt * SUBLANES, LANES), F32),
                        pltpu.SMEM((n_slots,), jnp.int32),
                        pltpu.SemaphoreType.DMA((2,)),
                        pltpu.SemaphoreType.DMA],
    )
    return pl.pallas_call(
        functools.partial(_dispatch_body, tt=tt, blk=blk, n_tokens=T),
        grid_spec=grid_spec,
        out_shape=[jax.ShapeDtypeStruct((n_slots * SUBLANES, LANES), F32),
                   jax.ShapeDtypeStruct((n_slots,), jnp.int32)],
        compiler_params=pltpu.CompilerParams(
            dimension_semantics=("arbitrary",), vmem_limit_bytes=VMEM_LIMIT_BYTES),
        name="dispatch",
    )(zero_blocks, pos3, h_all, pad_dst)


def _expert_body(be_ref, nv_ref, dst_ref, x_ref, wgu_ref, bgu_ref, wd_ref, bd_ref,
                 y_hbm, ybuf, osem, wgu_bf, wd_bf, xb_ref, act_ref, *, blk, n_real_rows):
    i = pl.program_id(0)
    n_valid = nv_ref[0]

    def scatter_row(s, r, prio=0):
        pltpu.make_async_copy(
            ybuf.at[s, _row_tile(r)], y_hbm.at[_row_tile(dst_ref[0, 0, r])], osem.at[s]).start(prio)

    def wait_scatter(s):
        pltpu.make_async_copy(ybuf.at[s], y_hbm.at[pl.ds(0, blk * SUBLANES)], osem.at[s]).wait()

    @pl.when(i == 0)
    def _():
        ybuf[...] = jnp.zeros(ybuf.shape, F32)
        for e in range(N_EXPERTS):
            pltpu.make_async_copy(
                ybuf.at[0], y_hbm.at[pl.ds((n_real_rows + e * blk) * SUBLANES, blk * SUBLANES)],
                osem.at[0]).start()
        for e in range(N_EXPERTS):
            wait_scatter(0)

    @pl.when((i < n_valid) & ((i == 0) | (be_ref[i] != be_ref[jnp.maximum(i - 1, 0)])))
    def _():
        wgu_bf[...] = wgu_ref[0].astype(BF16)
        wd_bf[...] = wd_ref[0].astype(BF16)

    rows_per_chunk = blk // N_CHUNKS
    cols = D_FF // N_CHUNKS

    def block_step(s):
        @pl.when(i < n_valid)
        def _():
            @pl.when(i >= 1)
            def _():
                wait_scatter(s)

            xb_ref[...] = _load_rows(x_ref, blk).astype(BF16)
            for c in range(N_CHUNKS):
                lo = c * cols
                xb = xb_ref[...]
                gate = jnp.dot(xb, wgu_bf[:, lo:lo + cols], preferred_element_type=F32)
                up = jnp.dot(xb, wgu_bf[:, D_FF + lo:D_FF + lo + cols], preferred_element_type=F32)
                gate = jnp.minimum(gate + bgu_ref[0, :, lo:lo + cols], SWIGLU_LIMIT)
                up = jnp.clip(up + bgu_ref[0, :, D_FF + lo:D_FF + lo + cols], -SWIGLU_LIMIT, SWIGLU_LIMIT)
                act = (up + 1.0) * gate * jax.nn.sigmoid(SWIGLU_ALPHA * gate)
                act_ref[:, lo:lo + cols] = act.astype(BF16)
                for r in range(c * rows_per_chunk, (c + 1) * rows_per_chunk):
                    scatter_row(1 - s, r, r % 2)
            y = jnp.dot(act_ref[...], wd_bf[...], preferred_element_type=F32) + bd_ref[0]
            _store_rows(ybuf.at[s], y, blk)

        @pl.when(i == n_valid)
        def _():
            wait_scatter(s)

            def body(gi, carry):
                for u in range(DMA_UNROLL):
                    scatter_row(1 - s, gi * DMA_UNROLL + u)
                return carry
            lax.fori_loop(0, blk // DMA_UNROLL, body, 0)
            wait_scatter(1 - s)

    for s in range(2):
        pl.when(i % 2 == s)(functools.partial(block_step, s))


def _experts(x_slots, dst, block_e, n_valid, w_gu, b_gu, w_down, b_down, blk, n_real_rows):
    n_blocks = block_e.shape[0]
    assert n_blocks * blk == n_real_rows + N_EXPERTS * blk and blk % (2 * N_CHUNKS) == 0
    n_out_rows = n_real_rows + (N_EXPERTS + 1) * blk
    priming = n_real_rows + N_EXPERTS * blk + jnp.arange(blk, dtype=jnp.int32)
    dst3 = jnp.concatenate([priming, dst]).reshape(n_blocks + 1, 1, blk)
    used = lambda i, nv: jnp.minimum(i, nv[0] - 1)
    grid_spec = pltpu.PrefetchScalarGridSpec(
        num_scalar_prefetch=2,
        grid=(n_blocks,),
        in_specs=[
            pl.BlockSpec((1, 1, blk), lambda i, be, nv: (i, 0, 0), memory_space=pltpu.SMEM),
            pl.BlockSpec((blk * SUBLANES, LANES), lambda i, be, nv: (used(i, nv), 0)),
            pl.BlockSpec((1, D_MODEL, 2 * D_FF), lambda i, be, nv: (be[i], 0, 0)),
            pl.BlockSpec((1, 1, 2 * D_FF), lambda i, be, nv: (be[i], 0, 0)),
            pl.BlockSpec((1, D_FF, D_MODEL), lambda i, be, nv: (be[i], 0, 0)),
            pl.BlockSpec((1, 1, D_MODEL), lambda i, be, nv: (be[i], 0, 0)),
        ],
        out_specs=pl.BlockSpec(memory_space=pl.ANY),
        scratch_shapes=[pltpu.VMEM((2, blk * SUBLANES, LANES), F32),
                        pltpu.SemaphoreType.DMA((2,)),
                        pltpu.VMEM((D_MODEL, 2 * D_FF), BF16),
                        pltpu.VMEM((D_FF, D_MODEL), BF16),
                        pltpu.VMEM((blk, D_MODEL), BF16),
                        pltpu.VMEM((blk, D_FF), BF16)],
    )
    return pl.pallas_call(
        functools.partial(_expert_body, blk=blk, n_real_rows=n_real_rows),
        grid_spec=grid_spec,
        out_shape=jax.ShapeDtypeStruct((n_out_rows * SUBLANES, LANES), F32),
        compiler_params=pltpu.CompilerParams(
            dimension_semantics=("arbitrary",), vmem_limit_bytes=VMEM_LIMIT_BYTES),
        name="experts",
    )(block_e, n_valid, dst3, x_slots, w_gu, b_gu.reshape(N_EXPERTS, 1, 2 * D_FF),
      w_down, b_down.reshape(N_EXPERTS, 1, D_MODEL))


def _combine_body(*refs, tt, n_first):
    y_refs = refs[:TOP_K]
    h_ref, gate_ref, g2_ref, b2_ref, out_first, out_second = refs[TOP_K:]
    i = pl.program_id(0)
    gates = gate_ref[...]
    f = gates[:, 0:1] * _load_rows(y_refs[0], tt)
    for kk in range(1, TOP_K):
        f = f + gates[:, kk:kk + 1] * _load_rows(y_refs[kk], tt)
    out = _layer_norm(DEEPNORM_ALPHA * _load_rows(h_ref, tt) + f, g2_ref[...], b2_ref[...])

    @pl.when(i < n_first)
    def _():
        out_first[...] = out

    @pl.when(i >= n_first)
    def _():
        out_second[...] = out


def _combine(y_rows, gates, h_all, ln2_g, ln2_b, tt, n_tokens_first):
    T = gates.shape[0]
    n_tiles = T // tt
    n_first = n_tokens_first // tt
    assert T % tt == 0 and n_tokens_first % tt == 0 and 0 < n_first < n_tiles
    row_blk = lambda imap: pl.BlockSpec((tt * SUBLANES, LANES), imap)
    y_specs = [row_blk(functools.partial(lambda i, kk: (kk * n_tiles + i, 0), kk=kk)) for kk in range(TOP_K)]
    return pl.pallas_call(
        functools.partial(_combine_body, tt=tt, n_first=n_first),
        grid=(n_tiles,),
        in_specs=y_specs + [
            row_blk(lambda i: (i, 0)),
            pl.BlockSpec((tt, TOP_K), lambda i: (i, 0)),
            pl.BlockSpec((1, D_MODEL), lambda i: (0, 0)),
            pl.BlockSpec((1, D_MODEL), lambda i: (0, 0)),
        ],
        out_specs=[pl.BlockSpec((tt, D_MODEL), lambda i: (jnp.minimum(i, n_first - 1), 0)),
                   pl.BlockSpec((tt, D_MODEL), lambda i: (jnp.maximum(i - n_first, 0), 0))],
        out_shape=[jax.ShapeDtypeStruct((n_tokens_first, D_MODEL), F32),
                   jax.ShapeDtypeStruct((T - n_tokens_first, D_MODEL), F32)],
        compiler_params=pltpu.CompilerParams(
            dimension_semantics=("arbitrary",), vmem_limit_bytes=VMEM_LIMIT_BYTES),
        name="combine",
    )(*([y_rows] * TOP_K), h_all, gates, ln2_g, ln2_b)


def kernel(x_prompt, x_sample, state_retention, state_conv, w_in, beta_ret, beta_conv, conv_w, conv_b,
           w_out, ln1_g, ln1_b, w_router, b_router, w_gu, b_gu, w_down, b_down, ln2_g, ln2_b):
    assert w_in.shape[0] == DEPTH == 1
    B, S, _ = x_prompt.shape
    Bd, Ld, _ = x_sample.shape
    Tp, Ts = B * S, Bd * Ld
    T = Tp + Ts
    wr_t = w_router[0].T
    wr_hi = wr_t.astype(BF16)
    wr_lo = (wr_t - wr_hi.astype(F32)).astype(BF16)
    w = dict(w_in=w_in[0].astype(BF16), w_out=w_out[0].astype(BF16), conv_w=conv_w[0],
             conv_b=conv_b[0][None], beta_ret=beta_ret[0][None], beta_conv=beta_conv[0][None],
             ln1_g=ln1_g[0][None], ln1_b=ln1_b[0][None], w_router=jnp.stack([wr_hi, wr_lo]),
             b_router=b_router[0][:, None])

    s_ret0 = jnp.zeros((B, N_RET_HEADS, HEAD_DIM, HEAD_DIM), F32)
    s_conv0 = jnp.zeros((B, CONV_W - 1, D_CONV), F32)
    h_s, choices_s, gates_s, ranks_s, counts_s, sret_s, sconv_s = _mixer(
        x_sample, state_retention[0], state_conv[0], PAST_LEN, SAMPLE_BATCH_BLOCK, Ld, w)
    h_all, choices_t, gates_t, ranks_t, counts_p, sret_p, sconv_p = _mixer(
        x_prompt, s_ret0, s_conv0, 0, 1, PROMPT_CHUNK, w, tail=(h_s, choices_s, gates_s, ranks_s))

    pos, pad_dst, block_e, n_valid, zero_blocks = _routing_plan(
        choices_t, ranks_t, counts_p[:, 0].astype(jnp.int32), counts_s[:, 0].astype(jnp.int32), Tp,
        EXPERT_BLOCK)
    x_slots, dst = _dispatch(h_all, pos, pad_dst, zero_blocks, DISPATCH_TILE, EXPERT_BLOCK)
    y_rows = _experts(x_slots, dst, block_e, n_valid, w_gu[0], b_gu[0], w_down[0], b_down[0],
                      EXPERT_BLOCK, T * TOP_K)
    y_p, y_s = _combine(y_rows, gates_t.T, h_all, ln2_g[0][None], ln2_b[0][None], COMBINE_TILE, Tp)

    return (y_p.reshape(B, S, D_MODEL), y_s.reshape(Bd, Ld, D_MODEL),
            sret_p[None], sconv_p[None], sret_s[None], sconv_s[None])
```

```python
import functools

import jax
import jax.numpy as jnp
from jax import lax
from jax.experimental import pallas as pl
from jax.experimental.pallas import tpu as pltpu

D_MODEL = 1024
N_RET_HEADS = 4
HEAD_DIM = 128
D_RET = N_RET_HEADS * HEAD_DIM
D_CONV = D_MODEL - D_RET
CONV_W = 3
N_EXPERTS = 32
TOP_K = 4
D_FF = D_MODEL
SWIGLU_LIMIT = 7.0
SWIGLU_ALPHA = 1.702
ROPE_BASE = 10000.0
LN_EPS = 1e-5
GN_EPS = 1e-5
DEPTH = 1
PAST_LEN = 1024
DEEPNORM_ALPHA = (2.0 * DEPTH) ** 0.25

LANES = 128
SUBLANES = 8
ROW_CHUNKS = D_MODEL // LANES
assert ROW_CHUNKS == SUBLANES

PROMPT_CHUNK = 256
SAMPLE_BATCH_BLOCK = 8
EXPERT_BLOCK = 512
DISPATCH_TILE = 512
COMBINE_TILE = 256
DMA_UNROLL = 8
N_CHUNKS = 4
VMEM_LIMIT_BYTES = 56 * 1024 * 1024

F32 = jnp.float32
BF16 = jnp.bfloat16


def _layer_norm(x, g, b):
    mu = jnp.mean(x, axis=-1, keepdims=True)
    xc = x - mu
    var = jnp.mean(xc * xc, axis=-1, keepdims=True)
    return xc * lax.rsqrt(var + LN_EPS) * g + b


def _load_rows(ref, n_rows):
    return jnp.concatenate(
        [ref[pl.ds(j, n_rows, stride=SUBLANES), :] for j in range(ROW_CHUNKS)], axis=1)


def _store_rows(ref, val, n_rows):
    for j in range(ROW_CHUNKS):
        ref[pl.ds(j, n_rows, stride=SUBLANES), :] = val[:, j * LANES:(j + 1) * LANES]


def _row_tile(r):
    if isinstance(r, int):
        return pl.ds(r * SUBLANES, SUBLANES)
    return pl.ds(pl.multiple_of(r * SUBLANES, SUBLANES), SUBLANES)


N_MIXER_INPUTS = 19
N_TOKEN_OUTPUTS = 4


def _mixer_body(*refs, nb, C, n_batch_steps, n_tail):
    if not n_tail:
        _mixer_compute(*refs, nb=nb, C=C)
        return
    tails = refs[N_MIXER_INPUTS:N_MIXER_INPUTS + N_TOKEN_OUTPUTS]
    outs = refs[N_MIXER_INPUTS + N_TOKEN_OUTPUTS:N_MIXER_INPUTS + 2 * N_TOKEN_OUTPUTS]
    bstep = pl.program_id(0)

    @pl.when(bstep < n_batch_steps)
    def _():
        _mixer_compute(*refs[:N_MIXER_INPUTS], *refs[N_MIXER_INPUTS + N_TOKEN_OUTPUTS:], nb=nb, C=C)

    @pl.when((bstep == n_batch_steps) & (pl.program_id(1) < n_tail))
    def _():
        for tail_ref, out_ref in zip(tails, outs):
            out_ref[...] = tail_ref[...]


def _route(logits_t, run_ref):
    n_tok = logits_t.shape[1]
    expert = lax.broadcasted_iota(jnp.int32, logits_t.shape, 0)
    vals, idxs, hots = [], [], []
    for _ in range(TOP_K):
        m = jnp.max(logits_t, axis=0, keepdims=True)
        idx = jnp.min(jnp.where(logits_t == m, expert, N_EXPERTS), axis=0, keepdims=True)
        hot = expert == idx
        vals.append(m)
        idxs.append(idx)
        hots.append(hot)
        logits_t = jnp.where(hot, -jnp.inf, logits_t)
    ex = [jnp.exp(v - vals[0]) for v in vals]
    total = ex[0]
    for e in ex[1:]:
        total = total + e
    gates = jnp.concatenate([e / total for e in ex], axis=0)

    chosen = hots[0].astype(F32)
    for hot in hots[1:]:
        chosen = chosen + hot.astype(F32)
    chosen = chosen.astype(BF16)
    t_row = lax.broadcasted_iota(jnp.int32, (n_tok, n_tok), 0)
    t_col = lax.broadcasted_iota(jnp.int32, (n_tok, n_tok), 1)
    earlier = jnp.where(t_row < t_col, 1.0, 0.0).astype(BF16)
    before = run_ref[...] + jnp.dot(chosen, earlier, preferred_element_type=F32)
    ranks = [jnp.sum(jnp.where(hot, before, 0.0), axis=0, keepdims=True) for hot in hots]
    run_ref[...] = run_ref[...] + jnp.dot(chosen, jnp.ones((n_tok, n_tok), BF16), preferred_element_type=F32)
    return jnp.concatenate(idxs, axis=0), gates, jnp.concatenate(ranks, axis=0).astype(jnp.int32)


def _mixer_compute(x_ref, cos_ref, sin_ref, decay_ref, qdec_ref, kdec_ref, sdec_ref, win_ref, wout_ref,
                   convw_ref, convb_ref, bret_ref, bconv_ref, g1_ref, b1_ref, wr_ref, br_ref,
                   sret_ref, sconv_ref, h_ref, choice_ref, gate_ref, rank_ref, count_ref, sret_out, sconv_out,
                   mix_ref, run_ref, *, nb, C):
    c = pl.program_id(1)

    @pl.when((pl.program_id(0) == 0) & (c == 0))
    def _():
        run_ref[...] = jnp.zeros(run_ref.shape, F32)

    @pl.when(c == 0)
    def _():
        sret_out[...] = sret_ref[...]
        sconv_out[...] = sconv_ref[...]

    x = x_ref[...].reshape(nb * C, D_MODEL)
    xb = x.astype(BF16)

    def proj(col0, width):
        return jnp.dot(xb, win_ref[:, col0:col0 + width], preferred_element_type=F32)

    q = proj(0, D_RET)
    k = proj(D_RET, D_RET)
    v = proj(2 * D_RET, D_RET)
    g = proj(3 * D_RET, D_RET)
    bg = proj(4 * D_RET, D_CONV)
    cg = proj(4 * D_RET + D_CONV, D_CONV)
    hc = proj(4 * D_RET + 2 * D_CONV, D_CONV)

    cos = cos_ref[...]
    sin = sin_ref[...]
    row = lax.broadcasted_iota(jnp.int32, (C, D_CONV), 0)
    k_scale = HEAD_DIM ** -0.5

    for b in range(nb):
        r0 = b * C
        for h in range(N_RET_HEADS):
            c0 = h * HEAD_DIM
            qh = q[r0:r0 + C, c0:c0 + HEAD_DIM]
            kh = k[r0:r0 + C, c0:c0 + HEAD_DIM]
            vh = v[r0:r0 + C, c0:c0 + HEAD_DIM]
            qh = qh * cos + pltpu.roll(qh, HEAD_DIM // 2, axis=1) * sin
            kh = (kh * cos + pltpu.roll(kh, HEAD_DIM // 2, axis=1) * sin) * k_scale
            qb = qh.astype(BF16)
            kb = kh.astype(BF16)
            vb = vh.astype(BF16)
            s_old = sret_out[b, h]
            scores = lax.dot_general(qb, kb, (((1,), (1,)), ((), ())), preferred_element_type=F32)
            scores = scores * decay_ref[h]
            intra = jnp.dot(scores.astype(BF16), vb, preferred_element_type=F32)
            cross = jnp.dot(qb, s_old.astype(BF16), preferred_element_type=F32) * qdec_ref[h]
            o = intra + cross
            kd = (kh * kdec_ref[h]).astype(BF16)
            s_new = sdec_ref[h] * s_old + lax.dot_general(
                kd, vb, (((0,), (0,)), ((), ())), preferred_element_type=F32)
            sret_out[b, h] = s_new
            mu = jnp.mean(o, axis=-1, keepdims=True)
            oc = o - mu
            var = jnp.mean(oc * oc, axis=-1, keepdims=True)
            on = oc * lax.rsqrt(var + GN_EPS)
            gh = g[r0:r0 + C, c0:c0 + HEAD_DIM]
            ret = on * (gh * jax.nn.sigmoid(gh)) * bret_ref[:, c0:c0 + HEAD_DIM]
            mix_ref[r0:r0 + C, c0:c0 + HEAD_DIM] = ret.astype(BF16)

        u = cg[r0:r0 + C] * hc[r0:r0 + C]
        prev = sconv_out[b]
        u1 = jnp.where(row == 0, prev[1:2], pltpu.roll(u, 1, axis=0))
        u2 = jnp.where(row == 0, prev[0:1], jnp.where(row == 1, prev[1:2], pltpu.roll(u, 2, axis=0)))
        z = convb_ref[...] + convw_ref[0:1] * u2 + convw_ref[1:2] * u1 + convw_ref[2:3] * u
        conv_out = bg[r0:r0 + C] * z * bconv_ref[...]
        mix_ref[r0:r0 + C, D_RET:D_RET + D_CONV] = conv_out.astype(BF16)
        sconv_out[b] = u[C - 2:C]

    m = jnp.dot(mix_ref[...], wout_ref[...], preferred_element_type=F32)
    hval = _layer_norm(DEEPNORM_ALPHA * x + m, g1_ref[...], b1_ref[...])
    _store_rows(h_ref, hval, nb * C)
    h_hi = hval.astype(BF16)
    h_lo = (hval - h_hi.astype(F32)).astype(BF16)
    nt_dot = lambda a, b: lax.dot_general(a, b, (((1,), (1,)), ((), ())), preferred_element_type=F32)
    logits_t = nt_dot(wr_ref[0], h_hi) + nt_dot(wr_ref[0], h_lo) + nt_dot(wr_ref[1], h_hi) + br_ref[...]
    choice_ref[...], gate_ref[...], rank_ref[...] = _route(logits_t, run_ref)
    count_ref[...] = run_ref[:, :LANES]


def _retention_tables(C, pos0, L):
    lg = jnp.log1p(-jnp.power(2.0, -5.0 - jnp.arange(N_RET_HEADS, dtype=F32)))
    i = jnp.arange(C, dtype=F32)
    diff = i[:, None] - i[None, :]
    decay = jnp.where(diff[None] >= 0, jnp.exp(lg[:, None, None] * jnp.maximum(diff, 0.0)[None]), 0.0)
    qdec = jnp.exp(lg[:, None] * (i + 1.0)[None, :])
    kdec = jnp.exp(lg[:, None] * (C - 1.0 - i)[None, :])
    sdec = jnp.exp(lg * C)
    qdec = jnp.broadcast_to(qdec[:, :, None], (N_RET_HEADS, C, HEAD_DIM))
    kdec = jnp.broadcast_to(kdec[:, :, None], (N_RET_HEADS, C, HEAD_DIM))
    sdec = jnp.broadcast_to(sdec[:, None, None], (N_RET_HEADS, 1, HEAD_DIM))
    pos = pos0 + jnp.arange(L, dtype=jnp.int32)
    inv_freq = ROPE_BASE ** (-jnp.arange(0, HEAD_DIM, 2, dtype=F32) / HEAD_DIM)
    ang = pos.astype(F32)[:, None] * inv_freq[None, :]
    cos = jnp.cos(ang)
    sin = jnp.sin(ang)
    cos_full = jnp.concatenate([cos, cos], axis=-1)
    sin_signed = jnp.concatenate([-sin, sin], axis=-1)
    return cos_full, sin_signed, decay, qdec, kdec, sdec


def _mixer(x, s_ret, s_conv, pos0, nb, C, w, tail=None):
    B, L, _ = x.shape
    nc = L // C
    nbs = B // nb
    rows = nb * C
    n_tokens = B * L
    n_tail = 0
    if tail is not None:
        n_tokens += tail[1].shape[1]
        n_tail = tail[1].shape[1] // rows
        assert tail[1].shape[1] % rows == 0 and 0 < n_tail <= nc
    cos, sin, decay, qdec, kdec, sdec = _retention_tables(C, pos0, L)

    bb = lambda b: jnp.minimum(b, nbs - 1)
    cc = lambda b, c: jnp.where(b < nbs, c, nc - 1)
    out_blk = lambda b, c: jnp.where(b < nbs, b * nc + c, nbs * nc + jnp.minimum(c, n_tail - 1))
    const2 = lambda b, c: (0, 0)
    const3 = lambda b, c: (0, 0, 0)
    full = lambda a: pl.BlockSpec(a.shape, const2 if a.ndim == 2 else const3)
    per_token = lambda imap: pl.BlockSpec((TOP_K, rows), lambda b, c: (0, imap(b, c)))
    in_arrays = [x, cos, sin, decay, qdec, kdec, sdec, w["w_in"], w["w_out"], w["conv_w"], w["conv_b"],
                 w["beta_ret"], w["beta_conv"], w["ln1_g"], w["ln1_b"], w["w_router"], w["b_router"],
                 s_ret, s_conv]
    assert len(in_arrays) == N_MIXER_INPUTS
    in_specs = [pl.BlockSpec((nb, C, D_MODEL), lambda b, c: (bb(b), cc(b, c), 0)),
                pl.BlockSpec((C, HEAD_DIM), lambda b, c: (cc(b, c), 0)),
                pl.BlockSpec((C, HEAD_DIM), lambda b, c: (cc(b, c), 0))]
    in_specs += [full(a) for a in in_arrays[3:17]]
    in_specs += [pl.BlockSpec((nb, N_RET_HEADS, HEAD_DIM, HEAD_DIM), lambda b, c: (bb(b), 0, 0, 0)),
                 pl.BlockSpec((nb, CONV_W - 1, D_CONV), lambda b, c: (bb(b), 0, 0))]
    if n_tail:
        tail_blk = lambda b, c: jnp.where(b < nbs, 0, jnp.minimum(c, n_tail - 1))
        in_arrays += list(tail)
        in_specs += [pl.BlockSpec((rows * SUBLANES, LANES), lambda b, c: (tail_blk(b, c), 0))]
        in_specs += [per_token(tail_blk)] * (N_TOKEN_OUTPUTS - 1)
    out_shape = [jax.ShapeDtypeStruct((n_tokens * SUBLANES, LANES), F32),
                 jax.ShapeDtypeStruct((TOP_K, n_tokens), jnp.int32),
                 jax.ShapeDtypeStruct((TOP_K, n_tokens), F32),
                 jax.ShapeDtypeStruct((TOP_K, n_tokens), jnp.int32),
                 jax.ShapeDtypeStruct((N_EXPERTS, LANES), F32),
                 jax.ShapeDtypeStruct((B, N_RET_HEADS, HEAD_DIM, HEAD_DIM), F32),
                 jax.ShapeDtypeStruct((B, CONV_W - 1, D_CONV), F32)]
    assert len(out_shape) == N_TOKEN_OUTPUTS + 3
    out_specs = [pl.BlockSpec((rows * SUBLANES, LANES), lambda b, c: (out_blk(b, c), 0))]
    out_specs += [per_token(out_blk)] * (N_TOKEN_OUTPUTS - 1)
    out_specs += [pl.BlockSpec((N_EXPERTS, LANES), const2),
                  pl.BlockSpec((nb, N_RET_HEADS, HEAD_DIM, HEAD_DIM), lambda b, c: (bb(b), 0, 0, 0)),
                  pl.BlockSpec((nb, CONV_W - 1, D_CONV), lambda b, c: (bb(b), 0, 0))]
    return pl.pallas_call(
        functools.partial(_mixer_body, nb=nb, C=C, n_batch_steps=nbs, n_tail=n_tail),
        grid=(nbs + (1 if n_tail else 0), nc), in_specs=in_specs, out_specs=out_specs, out_shape=out_shape,
        scratch_shapes=[pltpu.VMEM((rows, D_MODEL), BF16), pltpu.VMEM((N_EXPERTS, rows), F32)],
        compiler_params=pltpu.CompilerParams(
            dimension_semantics=("arbitrary", "arbitrary"), vmem_limit_bytes=VMEM_LIMIT_BYTES),
        name="mixer",
    )(*in_arrays)


def _lookup(table, choices):
    experts = jnp.arange(N_EXPERTS, dtype=jnp.int32)
    return jnp.sum(jnp.where(choices[..., None] == experts, table, 0), axis=-1)


def _routing_plan(choices_t, ranks_t, counts_first, counts_second, n_first, blk):
    T = choices_t.shape[1]
    A = T * TOP_K
    assert A % blk == 0
    n_blocks = A // blk + N_EXPERTS
    experts = jnp.arange(N_EXPERTS, dtype=jnp.int32)
    counts = counts_first + counts_second
    nblk_e = (counts + blk - 1) // blk
    bend = jnp.cumsum(nblk_e)
    n_valid = bend[-1]
    start = (bend - nblk_e) * blk
    token = jnp.arange(T, dtype=jnp.int32)[None, :]
    rank = ranks_t + jnp.where(token >= n_first, _lookup(counts_first, choices_t), 0)
    pos = _lookup(start, choices_t) + rank

    bi = jnp.arange(n_blocks, dtype=jnp.int32)
    block_e = jnp.sum((bi[:, None] >= bend[None, :]).astype(jnp.int32), axis=1)
    block_e = jnp.where(bi < n_valid, block_e, block_e[n_valid - 1])
    pad_dst = (A + block_e[:, None] * blk + jnp.arange(blk, dtype=jnp.int32)[None, :]).reshape(n_blocks * blk)
    last_block = jnp.where(nblk_e > 0, bend - 1, -1)
    unused = n_valid + experts
    unused = jnp.where(unused < n_blocks, unused, -1)
    zero_blocks = jnp.concatenate([last_block, unused]).astype(jnp.int32)
    return pos, pad_dst, block_e, n_valid.reshape(1).astype(jnp.int32), zero_blocks


def _dispatch_body(zero_ref, pos_ref, h_ref, pad_dst_hbm, xs_hbm, dst_hbm, hbuf, dst_tab, sem, tab_sem,
                   *, tt, blk, n_tokens):
    i = pl.program_id(0)
    n = pl.num_programs(0)
    slot = i % 2

    def wait_rows(s):
        for _ in range(TOP_K):
            pltpu.make_async_copy(hbuf.at[s], xs_hbm.at[pl.ds(0, tt * SUBLANES)], sem.at[s]).wait()

    @pl.when(i == 0)
    def _():
        load_table = pltpu.make_async_copy(pad_dst_hbm, dst_tab, tab_sem)
        load_table.start()
        load_table.wait()
        hbuf[1] = jnp.zeros(hbuf.shape[1:], F32)
        zeros = hbuf.at[1, pl.ds(0, blk * SUBLANES)]
        for e in range(zero_ref.shape[0]):
            @pl.when(zero_ref[e] >= 0)
            def _():
                first_row = pl.multiple_of(zero_ref[e] * (blk * SUBLANES), blk * SUBLANES)
                pltpu.make_async_copy(zeros, xs_hbm.at[pl.ds(first_row, blk * SUBLANES)], sem.at[1]).start()
        for e in range(zero_ref.shape[0]):
            @pl.when(zero_ref[e] >= 0)
            def _():
                pltpu.make_async_copy(zeros, xs_hbm.at[pl.ds(0, blk * SUBLANES)], sem.at[1]).wait()

    @pl.when(i >= 2)
    def _():
        wait_rows(slot)

    hbuf[slot] = h_ref[...]
    first_token = i * tt
    for t in range(tt):
        row = hbuf.at[slot, _row_tile(t)]
        for kk in range(TOP_K):
            p = pos_ref[0, 0, t * TOP_K + kk]
            dst_tab[p] = first_token + (kk * n_tokens + t)
            pltpu.make_async_copy(row, xs_hbm.at[_row_tile(p)], sem.at[slot]).start(kk % 2)

    @pl.when(i == n - 1)
    def _():
        wait_rows(slot)

        @pl.when(i >= 1)
        def _():
            wait_rows(1 - slot)

        store_table = pltpu.make_async_copy(dst_tab, dst_hbm, tab_sem)
        store_table.start()
        store_table.wait()


def _dispatch(h_all, pos, pad_dst, zero_blocks, tt, blk):
    T = pos.shape[1]
    n_slots = pad_dst.shape[0]
    n_tiles = T // tt
    assert T % tt == 0 and blk <= tt and DMA_UNROLL % TOP_K == 0
    pos3 = pos.T.reshape(n_tiles, 1, tt * TOP_K)
    grid_spec = pltpu.PrefetchScalarGridSpec(
        num_scalar_prefetch=1,
        grid=(n_tiles,),
        in_specs=[pl.BlockSpec((1, 1, tt * TOP_K), lambda i, zb: (i, 0, 0), memory_space=pltpu.SMEM),
                  pl.BlockSpec((tt * SUBLANES, LANES), lambda i, zb: (i, 0)),
                  pl.BlockSpec(memory_space=pl.ANY)],
        out_specs=[pl.BlockSpec(memory_space=pl.ANY), pl.BlockSpec(memory_space=pl.ANY)],
        scratch_shapes=[pltpu.VMEM((2, tt * SUBLANES, LANES), F32),
                        pltpu.SMEM((n_slots,), jnp.int32),
                        pltpu.SemaphoreType.DMA((2,)),
                        pltpu.SemaphoreType.DMA],
    )
    return pl.pallas_call(
        functools.partial(_dispatch_body, tt=tt, blk=blk, n_tokens=T),
        grid_spec=grid_spec,
        out_shape=[jax.ShapeDtypeStruct((n_slots * SUBLANES, LANES), F32),
                   jax.ShapeDtypeStruct((n_slots,), jnp.int32)],
        compiler_params=pltpu.CompilerParams(
            dimension_semantics=("arbitrary",), vmem_limit_bytes=VMEM_LIMIT_BYTES),
        name="dispatch",
    )(zero_blocks, pos3, h_all, pad_dst)


def _expert_body(be_ref, nv_ref, dst_ref, x_ref, wgu_ref, bgu_ref, wd_ref, bd_ref,
                 y_hbm, ybuf, osem, wgu_bf, wd_bf, xb_ref, act_ref, *, blk, n_real_rows):
    i = pl.program_id(0)
    n_valid = nv_ref[0]

    def scatter_row(s, r, prio=0):
        pltpu.make_async_copy(
            ybuf.at[s, _row_tile(r)], y_hbm.at[_row_tile(dst_ref[0, 0, r])], osem.at[s]).start(prio)

    def wait_scatter(s):
        pltpu.make_async_copy(ybuf.at[s], y_hbm.at[pl.ds(0, blk * SUBLANES)], osem.at[s]).wait()

    @pl.when(i == 0)
    def _():
        ybuf[...] = jnp.zeros(ybuf.shape, F32)
        for e in range(N_EXPERTS):
            pltpu.make_async_copy(
                ybuf.at[0], y_hbm.at[pl.ds((n_real_rows + e * blk) * SUBLANES, blk * SUBLANES)],
                osem.at[0]).start()
        for e in range(N_EXPERTS):
            wait_scatter(0)

    @pl.when((i < n_valid) & ((i == 0) | (be_ref[i] != be_ref[jnp.maximum(i - 1, 0)])))
    def _():
        wgu_bf[...] = wgu_ref[0].astype(BF16)
        wd_bf[...] = wd_ref[0].astype(BF16)

    rows_per_chunk = blk // N_CHUNKS
    cols = D_FF // N_CHUNKS

    def block_step(s):
        @pl.when(i < n_valid)
        def _():
            @pl.when(i >= 1)
            def _():
                wait_scatter(s)

            xb_ref[...] = _load_rows(x_ref, blk).astype(BF16)
            for c in range(N_CHUNKS):
                lo = c * cols
                xb = xb_ref[...]
                gate = jnp.dot(xb, wgu_bf[:, lo:lo + cols], preferred_element_type=F32)
                up = jnp.dot(xb, wgu_bf[:, D_FF + lo:D_FF + lo + cols], preferred_element_type=F32)
                gate = jnp.minimum(gate + bgu_ref[0, :, lo:lo + cols], SWIGLU_LIMIT)
                up = jnp.clip(up + bgu_ref[0, :, D_FF + lo:D_FF + lo + cols], -SWIGLU_LIMIT, SWIGLU_LIMIT)
                act = (up + 1.0) * gate * jax.nn.sigmoid(SWIGLU_ALPHA * gate)
                act_ref[:, lo:lo + cols] = act.astype(BF16)
                for r in range(c * rows_per_chunk, (c + 1) * rows_per_chunk):
                    scatter_row(1 - s, r, r % 2)
            y = jnp.dot(act_ref[...], wd_bf[...], preferred_element_type=F32) + bd_ref[0]
            _store_rows(ybuf.at[s], y, blk)

        @pl.when(i == n_valid)
        def _():
            wait_scatter(s)

            def body(gi, carry):
                for u in range(DMA_UNROLL):
                    scatter_row(1 - s, gi * DMA_UNROLL + u)
                return carry
            lax.fori_loop(0, blk // DMA_UNROLL, body, 0)
            wait_scatter(1 - s)

    for s in range(2):
        pl.when(i % 2 == s)(functools.partial(block_step, s))


def _experts(x_slots, dst, block_e, n_valid, w_gu, b_gu, w_down, b_down, blk, n_real_rows):
    n_blocks = block_e.shape[0]
    assert n_blocks * blk == n_real_rows + N_EXPERTS * blk and blk % (2 * N_CHUNKS) == 0
    n_out_rows = n_real_rows + (N_EXPERTS + 1) * blk
    priming = n_real_rows + N_EXPERTS * blk + jnp.arange(blk, dtype=jnp.int32)
    dst3 = jnp.concatenate([priming, dst]).reshape(n_blocks + 1, 1, blk)
    used = lambda i, nv: jnp.minimum(i, nv[0] - 1)
    grid_spec = pltpu.PrefetchScalarGridSpec(
        num_scalar_prefetch=2,
        grid=(n_blocks,),
        in_specs=[
            pl.BlockSpec((1, 1, blk), lambda i, be, nv: (i, 0, 0), memory_space=pltpu.SMEM),
            pl.BlockSpec((blk * SUBLANES, LANES), lambda i, be, nv: (used(i, nv), 0)),
            pl.BlockSpec((1, D_MODEL, 2 * D_FF), lambda i, be, nv: (be[i], 0, 0)),
            pl.BlockSpec((1, 1, 2 * D_FF), lambda i, be, nv: (be[i], 0, 0)),
            pl.BlockSpec((1, D_FF, D_MODEL), lambda i, be, nv: (be[i], 0, 0)),
            pl.BlockSpec((1, 1, D_MODEL), lambda i, be, nv: (be[i], 0, 0)),
        ],
        out_specs=pl.BlockSpec(memory_space=pl.ANY),
        scratch_shapes=[pltpu.VMEM((2, blk * SUBLANES, LANES), F32),
                        pltpu.SemaphoreType.DMA((2,)),
                        pltpu.VMEM((D_MODEL, 2 * D_FF), BF16),
                        pltpu.VMEM((D_FF, D_MODEL), BF16),
                        pltpu.VMEM((blk, D_MODEL), BF16),
                        pltpu.VMEM((blk, D_FF), BF16)],
    )
    return pl.pallas_call(
        functools.partial(_expert_body, blk=blk, n_real_rows=n_real_rows),
        grid_spec=grid_spec,
        out_shape=jax.ShapeDtypeStruct((n_out_rows * SUBLANES, LANES), F32),
        compiler_params=pltpu.CompilerParams(
            dimension_semantics=("arbitrary",), vmem_limit_bytes=VMEM_LIMIT_BYTES),
        name="experts",
    )(block_e, n_valid, dst3, x_slots, w_gu, b_gu.reshape(N_EXPERTS, 1, 2 * D_FF),
      w_down, b_down.reshape(N_EXPERTS, 1, D_MODEL))


def _combine_body(*refs, tt, n_first):
    y_refs = refs[:TOP_K]
    h_ref, gate_ref, g2_ref, b2_ref, out_first, out_second = refs[TOP_K:]
    i = pl.program_id(0)
    gates = gate_ref[...]
    f = gates[:, 0:1] * _load_rows(y_refs[0], tt)
    for kk in range(1, TOP_K):
        f = f + gates[:, kk:kk + 1] * _load_rows(y_refs[kk], tt)
    out = _layer_norm(DEEPNORM_ALPHA * _load_rows(h_ref, tt) + f, g2_ref[...], b2_ref[...])

    @pl.when(i < n_first)
    def _():
        out_first[...] = out

    @pl.when(i >= n_first)
    def _():
        out_second[...] = out


def _combine(y_rows, gates, h_all, ln2_g, ln2_b, tt, n_tokens_first):
    T = gates.shape[0]
    n_tiles = T // tt
    n_first = n_tokens_first // tt
    assert T % tt == 0 and n_tokens_first % tt == 0 and 0 < n_first < n_tiles
    row_blk = lambda imap: pl.BlockSpec((tt * SUBLANES, LANES), imap)
    y_specs = [row_blk(functools.partial(lambda i, kk: (kk * n_tiles + i, 0), kk=kk)) for kk in range(TOP_K)]
    return pl.pallas_call(
        functools.partial(_combine_body, tt=tt, n_first=n_first),
        grid=(n_tiles,),
        in_specs=y_specs + [
            row_blk(lambda i: (i, 0)),
            pl.BlockSpec((tt, TOP_K), lambda i: (i, 0)),
            pl.BlockSpec((1, D_MODEL), lambda i: (0, 0)),
            pl.BlockSpec((1, D_MODEL), lambda i: (0, 0)),
        ],
        out_specs=[pl.BlockSpec((tt, D_MODEL), lambda i: (jnp.minimum(i, n_first - 1), 0)),
                   pl.BlockSpec((tt, D_MODEL), lambda i: (jnp.maximum(i - n_first, 0), 0))],
        out_shape=[jax.ShapeDtypeStruct((n_tokens_first, D_MODEL), F32),
                   jax.ShapeDtypeStruct((T - n_tokens_first, D_MODEL), F32)],
        compiler_params=pltpu.CompilerParams(
            dimension_semantics=("arbitrary",), vmem_limit_bytes=VMEM_LIMIT_BYTES),
        name="combine",
    )(*([y_rows] * TOP_K), h_all, gates, ln2_g, ln2_b)


def kernel(x_prompt, x_sample, state_retention, state_conv, w_in, beta_ret, beta_conv, conv_w, conv_b,
           w_out, ln1_g, ln1_b, w_router, b_router, w_gu, b_gu, w_down, b_down, ln2_g, ln2_b):
    assert w_in.shape[0] == DEPTH == 1
    B, S, _ = x_prompt.shape
    Bd, Ld, _ = x_sample.shape
    Tp, Ts = B * S, Bd * Ld
    T = Tp + Ts
    wr_t = w_router[0].T
    wr_hi = wr_t.astype(BF16)
    wr_lo = (wr_t - wr_hi.astype(F32)).astype(BF16)
    w = dict(w_in=w_in[0].astype(BF16), w_out=w_out[0].astype(BF16), conv_w=conv_w[0],
             conv_b=conv_b[0][None], beta_ret=beta_ret[0][None], beta_conv=beta_conv[0][None],
             ln1_g=ln1_g[0][None], ln1_b=ln1_b[0][None], w_router=jnp.stack([wr_hi, wr_lo]),
             b_router=b_router[0][:, None])

    s_ret0 = jnp.zeros((B, N_RET_HEADS, HEAD_DIM, HEAD_DIM), F32)
    s_conv0 = jnp.zeros((B, CONV_W - 1, D_CONV), F32)
    h_s, choices_s, gates_s, ranks_s, counts_s, sret_s, sconv_s = _mixer(
        x_sample, state_retention[0], state_conv[0], PAST_LEN, SAMPLE_BATCH_BLOCK, Ld, w)
    h_all, choices_t, gates_t, ranks_t, counts_p, sret_p, sconv_p = _mixer(
        x_prompt, s_ret0, s_conv0, 0, 1, PROMPT_CHUNK, w, tail=(h_s, choices_s, gates_s, ranks_s))

    pos, pad_dst, block_e, n_valid, zero_blocks = _routing_plan(
        choices_t, ranks_t, counts_p[:, 0].astype(jnp.int32), counts_s[:, 0].astype(jnp.int32), Tp,
        EXPERT_BLOCK)
    x_slots, dst = _dispatch(h_all, pos, pad_dst, zero_blocks, DISPATCH_TILE, EXPERT_BLOCK)
    y_rows = _experts(x_slots, dst, block_e, n_valid, w_gu[0], b_gu[0], w_down[0], b_down[0],
                      EXPERT_BLOCK, T * TOP_K)
    y_p, y_s = _combine(y_rows, gates_t.T, h_all, ln2_g[0][None], ln2_b[0][None], COMBINE_TILE, Tp)

    return (y_p.reshape(B, S, D_MODEL), y_s.reshape(Bd, Ld, D_MODEL),
            sret_p[None], sconv_p[None], sret_s[None], sconv_s[None])
```

```python
import functools

import jax
import jax.numpy as jnp
from jax import lax
from jax.experimental import pallas as pl
from jax.experimental.pallas import tpu as pltpu

D_MODEL = 1024
N_RET_HEADS = 4
HEAD_DIM = 128
D_RET = N_RET_HEADS * HEAD_DIM
D_CONV = D_MODEL - D_RET
CONV_W = 3
N_EXPERTS = 32
TOP_K = 4
D_FF = D_MODEL
SWIGLU_LIMIT = 7.0
SWIGLU_ALPHA = 1.702
ROPE_BASE = 10000.0
LN_EPS = 1e-5
GN_EPS = 1e-5
DEPTH = 1
PAST_LEN = 1024
DEEPNORM_ALPHA = (2.0 * DEPTH) ** 0.25

LANES = 128
SUBLANES = 8
ROW_CHUNKS = D_MODEL // LANES
assert ROW_CHUNKS == SUBLANES

PROMPT_CHUNK = 256
SAMPLE_BATCH_BLOCK = 8
EXPERT_BLOCK = 512
DISPATCH_TILE = 512
COMBINE_TILE = 512
DMA_UNROLL = 8
N_CHUNKS = 4
VMEM_LIMIT_BYTES = 56 * 1024 * 1024

F32 = jnp.float32
BF16 = jnp.bfloat16


def _layer_norm(x, g, b):
    mu = jnp.mean(x, axis=-1, keepdims=True)
    xc = x - mu
    var = jnp.mean(xc * xc, axis=-1, keepdims=True)
    return xc * lax.rsqrt(var + LN_EPS) * g + b


def _load_rows(ref, n_rows):
    return jnp.concatenate(
        [ref[pl.ds(j, n_rows, stride=SUBLANES), :] for j in range(ROW_CHUNKS)], axis=1)


def _store_rows(ref, val, n_rows):
    for j in range(ROW_CHUNKS):
        ref[pl.ds(j, n_rows, stride=SUBLANES), :] = val[:, j * LANES:(j + 1) * LANES]


def _row_tile(r):
    if isinstance(r, int):
        return pl.ds(r * SUBLANES, SUBLANES)
    return pl.ds(pl.multiple_of(r * SUBLANES, SUBLANES), SUBLANES)


N_MIXER_INPUTS = 19
N_TOKEN_OUTPUTS = 4


def _mixer_body(*refs, nb, C, n_batch_steps, n_tail):
    if not n_tail:
        _mixer_compute(*refs, nb=nb, C=C)
        return
    tails = refs[N_MIXER_INPUTS:N_MIXER_INPUTS + N_TOKEN_OUTPUTS]
    outs = refs[N_MIXER_INPUTS + N_TOKEN_OUTPUTS:N_MIXER_INPUTS + 2 * N_TOKEN_OUTPUTS]
    bstep = pl.program_id(0)

    @pl.when(bstep < n_batch_steps)
    def _():
        _mixer_compute(*refs[:N_MIXER_INPUTS], *refs[N_MIXER_INPUTS + N_TOKEN_OUTPUTS:], nb=nb, C=C)

    @pl.when((bstep == n_batch_steps) & (pl.program_id(1) < n_tail))
    def _():
        for tail_ref, out_ref in zip(tails, outs):
            out_ref[...] = tail_ref[...]


def _route(logits_t, run_ref):
    n_tok = logits_t.shape[1]
    expert = lax.broadcasted_iota(jnp.int32, logits_t.shape, 0)
    vals, idxs, hots = [], [], []
    for _ in range(TOP_K):
        m = jnp.max(logits_t, axis=0, keepdims=True)
        idx = jnp.min(jnp.where(logits_t == m, expert, N_EXPERTS), axis=0, keepdims=True)
        hot = expert == idx
        vals.append(m)
        idxs.append(idx)
        hots.append(hot)
        logits_t = jnp.where(hot, -jnp.inf, logits_t)
    ex = [jnp.exp(v - vals[0]) for v in vals]
    total = ex[0]
    for e in ex[1:]:
        total = total + e
    gates = jnp.concatenate([e / total for e in ex], axis=0)

    chosen = hots[0].astype(F32)
    for hot in hots[1:]:
        chosen = chosen + hot.astype(F32)
    chosen = chosen.astype(BF16)
    t_row = lax.broadcasted_iota(jnp.int32, (n_tok, n_tok), 0)
    t_col = lax.broadcasted_iota(jnp.int32, (n_tok, n_tok), 1)
    earlier = jnp.where(t_row < t_col, 1.0, 0.0).astype(BF16)
    before = run_ref[...] + jnp.dot(chosen, earlier, preferred_element_type=F32)
    ranks = [jnp.sum(jnp.where(hot, before, 0.0), axis=0, keepdims=True) for hot in hots]
    run_ref[...] = run_ref[...] + jnp.dot(chosen, jnp.ones((n_tok, n_tok), BF16), preferred_element_type=F32)
    return jnp.concatenate(idxs, axis=0), gates, jnp.concatenate(ranks, axis=0).astype(jnp.int32)


def _mixer_compute(x_ref, cos_ref, sin_ref, decay_ref, qdec_ref, kdec_ref, sdec_ref, win_ref, wout_ref,
                   convw_ref, convb_ref, bret_ref, bconv_ref, g1_ref, b1_ref, wr_ref, br_ref,
                   sret_ref, sconv_ref, h_ref, choice_ref, gate_ref, rank_ref, count_ref, sret_out, sconv_out,
                   mix_ref, run_ref, *, nb, C):
    c = pl.program_id(1)

    @pl.when((pl.program_id(0) == 0) & (c == 0))
    def _():
        run_ref[...] = jnp.zeros(run_ref.shape, F32)

    @pl.when(c == 0)
    def _():
        sret_out[...] = sret_ref[...]
        sconv_out[...] = sconv_ref[...]

    x = x_ref[...].reshape(nb * C, D_MODEL)
    xb = x.astype(BF16)

    def proj(col0, width):
        return jnp.dot(xb, win_ref[:, col0:col0 + width], preferred_element_type=F32)

    q = proj(0, D_RET)
    k = proj(D_RET, D_RET)
    v = proj(2 * D_RET, D_RET)
    g = proj(3 * D_RET, D_RET)
    bg = proj(4 * D_RET, D_CONV)
    cg = proj(4 * D_RET + D_CONV, D_CONV)
    hc = proj(4 * D_RET + 2 * D_CONV, D_CONV)

    cos = cos_ref[...]
    sin = sin_ref[...]
    row = lax.broadcasted_iota(jnp.int32, (C, D_CONV), 0)
    k_scale = HEAD_DIM ** -0.5

    for b in range(nb):
        r0 = b * C
        for h in range(N_RET_HEADS):
            c0 = h * HEAD_DIM
            qh = q[r0:r0 + C, c0:c0 + HEAD_DIM]
            kh = k[r0:r0 + C, c0:c0 + HEAD_DIM]
            vh = v[r0:r0 + C, c0:c0 + HEAD_DIM]
            qh = qh * cos + pltpu.roll(qh, HEAD_DIM // 2, axis=1) * sin
            kh = (kh * cos + pltpu.roll(kh, HEAD_DIM // 2, axis=1) * sin) * k_scale
            qb = qh.astype(BF16)
            kb = kh.astype(BF16)
            vb = vh.astype(BF16)
            s_old = sret_out[b, h]
            scores = lax.dot_general(qb, kb, (((1,), (1,)), ((), ())), preferred_element_type=F32)
            scores = scores * decay_ref[h]
            intra = jnp.dot(scores.astype(BF16), vb, preferred_element_type=F32)
            cross = jnp.dot(qb, s_old.astype(BF16), preferred_element_type=F32) * qdec_ref[h]
            o = intra + cross
            kd = (kh * kdec_ref[h]).astype(BF16)
            s_new = sdec_ref[h] * s_old + lax.dot_general(
                kd, vb, (((0,), (0,)), ((), ())), preferred_element_type=F32)
            sret_out[b, h] = s_new
            mu = jnp.mean(o, axis=-1, keepdims=True)
            oc = o - mu
            var = jnp.mean(oc * oc, axis=-1, keepdims=True)
            on = oc * lax.rsqrt(var + GN_EPS)
            gh = g[r0:r0 + C, c0:c0 + HEAD_DIM]
            ret = on * (gh * jax.nn.sigmoid(gh)) * bret_ref[:, c0:c0 + HEAD_DIM]
            mix_ref[r0:r0 + C, c0:c0 + HEAD_DIM] = ret.astype(BF16)

        u = cg[r0:r0 + C] * hc[r0:r0 + C]
        prev = sconv_out[b]
        u1 = jnp.where(row == 0, prev[1:2], pltpu.roll(u, 1, axis=0))
        u2 = jnp.where(row == 0, prev[0:1], jnp.where(row == 1, prev[1:2], pltpu.roll(u, 2, axis=0)))
        z = convb_ref[...] + convw_ref[0:1] * u2 + convw_ref[1:2] * u1 + convw_ref[2:3] * u
        conv_out = bg[r0:r0 + C] * z * bconv_ref[...]
        mix_ref[r0:r0 + C, D_RET:D_RET + D_CONV] = conv_out.astype(BF16)
        sconv_out[b] = u[C - 2:C]

    m = jnp.dot(mix_ref[...], wout_ref[...], preferred_element_type=F32)
    hval = _layer_norm(DEEPNORM_ALPHA * x + m, g1_ref[...], b1_ref[...])
    _store_rows(h_ref, hval, nb * C)
    h_hi = hval.astype(BF16)
    h_lo = (hval - h_hi.astype(F32)).astype(BF16)
    nt_dot = lambda a, b: lax.dot_general(a, b, (((1,), (1,)), ((), ())), preferred_element_type=F32)
    logits_t = nt_dot(wr_ref[0], h_hi) + nt_dot(wr_ref[0], h_lo) + nt_dot(wr_ref[1], h_hi) + br_ref[...]
    choice_ref[...], gate_ref[...], rank_ref[...] = _route(logits_t, run_ref)
    count_ref[...] = run_ref[:, :LANES]


def _retention_tables(C, pos0, L):
    lg = jnp.log1p(-jnp.power(2.0, -5.0 - jnp.arange(N_RET_HEADS, dtype=F32)))
    i = jnp.arange(C, dtype=F32)
    diff = i[:, None] - i[None, :]
    decay = jnp.where(diff[None] >= 0, jnp.exp(lg[:, None, None] * jnp.maximum(diff, 0.0)[None]), 0.0)
    qdec = jnp.exp(lg[:, None] * (i + 1.0)[None, :])
    kdec = jnp.exp(lg[:, None] * (C - 1.0 - i)[None, :])
    sdec = jnp.exp(lg * C)
    qdec = jnp.broadcast_to(qdec[:, :, None], (N_RET_HEADS, C, HEAD_DIM))
    kdec = jnp.broadcast_to(kdec[:, :, None], (N_RET_HEADS, C, HEAD_DIM))
    sdec = jnp.broadcast_to(sdec[:, None, None], (N_RET_HEADS, 1, HEAD_DIM))
    pos = pos0 + jnp.arange(L, dtype=jnp.int32)
    inv_freq = ROPE_BASE ** (-jnp.arange(0, HEAD_DIM, 2, dtype=F32) / HEAD_DIM)
    ang = pos.astype(F32)[:, None] * inv_freq[None, :]
    cos = jnp.cos(ang)
    sin = jnp.sin(ang)
    cos_full = jnp.concatenate([cos, cos], axis=-1)
    sin_signed = jnp.concatenate([-sin, sin], axis=-1)
    return cos_full, sin_signed, decay, qdec, kdec, sdec


def _mixer(x, s_ret, s_conv, pos0, nb, C, w, tail=None):
    B, L, _ = x.shape
    nc = L // C
    nbs = B // nb
    rows = nb * C
    n_tokens = B * L
    n_tail = 0
    if tail is not None:
        n_tokens += tail[1].shape[1]
        n_tail = tail[1].shape[1] // rows
        assert tail[1].shape[1] % rows == 0 and 0 < n_tail <= nc
    cos, sin, decay, qdec, kdec, sdec = _retention_tables(C, pos0, L)

    bb = lambda b: jnp.minimum(b, nbs - 1)
    cc = lambda b, c: jnp.where(b < nbs, c, nc - 1)
    out_blk = lambda b, c: jnp.where(b < nbs, b * nc + c, nbs * nc + jnp.minimum(c, n_tail - 1))
    const2 = lambda b, c: (0, 0)
    const3 = lambda b, c: (0, 0, 0)
    full = lambda a: pl.BlockSpec(a.shape, const2 if a.ndim == 2 else const3)
    per_token = lambda imap: pl.BlockSpec((TOP_K, rows), lambda b, c: (0, imap(b, c)))
    in_arrays = [x, cos, sin, decay, qdec, kdec, sdec, w["w_in"], w["w_out"], w["conv_w"], w["conv_b"],
                 w["beta_ret"], w["beta_conv"], w["ln1_g"], w["ln1_b"], w["w_router"], w["b_router"],
                 s_ret, s_conv]
    assert len(in_arrays) == N_MIXER_INPUTS
    in_specs = [pl.BlockSpec((nb, C, D_MODEL), lambda b, c: (bb(b), cc(b, c), 0)),
                pl.BlockSpec((C, HEAD_DIM), lambda b, c: (cc(b, c), 0)),
                pl.BlockSpec((C, HEAD_DIM), lambda b, c: (cc(b, c), 0))]
    in_specs += [full(a) for a in in_arrays[3:17]]
    in_specs += [pl.BlockSpec((nb, N_RET_HEADS, HEAD_DIM, HEAD_DIM), lambda b, c: (bb(b), 0, 0, 0)),
                 pl.BlockSpec((nb, CONV_W - 1, D_CONV), lambda b, c: (bb(b), 0, 0))]
    if n_tail:
        tail_blk = lambda b, c: jnp.where(b < nbs, 0, jnp.minimum(c, n_tail - 1))
        in_arrays += list(tail)
        in_specs += [pl.BlockSpec((rows * SUBLANES, LANES), lambda b, c: (tail_blk(b, c), 0))]
        in_specs += [per_token(tail_blk)] * (N_TOKEN_OUTPUTS - 1)
    out_shape = [jax.ShapeDtypeStruct((n_tokens * SUBLANES, LANES), F32),
                 jax.ShapeDtypeStruct((TOP_K, n_tokens), jnp.int32),
                 jax.ShapeDtypeStruct((TOP_K, n_tokens), F32),
                 jax.ShapeDtypeStruct((TOP_K, n_tokens), jnp.int32),
                 jax.ShapeDtypeStruct((N_EXPERTS, LANES), F32),
                 jax.ShapeDtypeStruct((B, N_RET_HEADS, HEAD_DIM, HEAD_DIM), F32),
                 jax.ShapeDtypeStruct((B, CONV_W - 1, D_CONV), F32)]
    assert len(out_shape) == N_TOKEN_OUTPUTS + 3
    out_specs = [pl.BlockSpec((rows * SUBLANES, LANES), lambda b, c: (out_blk(b, c), 0))]
    out_specs += [per_token(out_blk)] * (N_TOKEN_OUTPUTS - 1)
    out_specs += [pl.BlockSpec((N_EXPERTS, LANES), const2),
                  pl.BlockSpec((nb, N_RET_HEADS, HEAD_DIM, HEAD_DIM), lambda b, c: (bb(b), 0, 0, 0)),
                  pl.BlockSpec((nb, CONV_W - 1, D_CONV), lambda b, c: (bb(b), 0, 0))]
    return pl.pallas_call(
        functools.partial(_mixer_body, nb=nb, C=C, n_batch_steps=nbs, n_tail=n_tail),
        grid=(nbs + (1 if n_tail else 0), nc), in_specs=in_specs, out_specs=out_specs, out_shape=out_shape,
        scratch_shapes=[pltpu.VMEM((rows, D_MODEL), BF16), pltpu.VMEM((N_EXPERTS, rows), F32)],
        compiler_params=pltpu.CompilerParams(
            dimension_semantics=("arbitrary", "arbitrary"), vmem_limit_bytes=VMEM_LIMIT_BYTES),
        name="mixer",
    )(*in_arrays)


def _lookup(table, choices):
    experts = jnp.arange(N_EXPERTS, dtype=jnp.int32)
    return jnp.sum(jnp.where(choices[..., None] == experts, table, 0), axis=-1)


def _routing_plan(choices_t, ranks_t, counts_first, counts_second, n_first, blk):
    T = choices_t.shape[1]
    A = T * TOP_K
    assert A % blk == 0
    n_blocks = A // blk + N_EXPERTS
    experts = jnp.arange(N_EXPERTS, dtype=jnp.int32)
    counts = counts_first + counts_second
    nblk_e = (counts + blk - 1) // blk
    bend = jnp.cumsum(nblk_e)
    n_valid = bend[-1]
    start = (bend - nblk_e) * blk
    token = jnp.arange(T, dtype=jnp.int32)[None, :]
    rank = ranks_t + jnp.where(token >= n_first, _lookup(counts_first, choices_t), 0)
    pos = _lookup(start, choices_t) + rank

    bi = jnp.arange(n_blocks, dtype=jnp.int32)
    block_e = jnp.sum((bi[:, None] >= bend[None, :]).astype(jnp.int32), axis=1)
    block_e = jnp.where(bi < n_valid, block_e, block_e[n_valid - 1])
    pad_dst = (A + block_e[:, None] * blk + jnp.arange(blk, dtype=jnp.int32)[None, :]).reshape(n_blocks * blk)
    last_block = jnp.where(nblk_e > 0, bend - 1, -1)
    unused = n_valid + experts
    unused = jnp.where(unused < n_blocks, unused, -1)
    zero_blocks = jnp.concatenate([last_block, unused]).astype(jnp.int32)
    return pos, pad_dst, block_e, n_valid.reshape(1).astype(jnp.int32), zero_blocks


def _dispatch_body(zero_ref, pos_ref, h_ref, pad_dst_hbm, xs_hbm, dst_hbm, hbuf, dst_tab, sem, tab_sem,
                   *, tt, blk, n_tokens):
    i = pl.program_id(0)
    n = pl.num_programs(0)
    slot = i % 2

    def wait_rows(s):
        for _ in range(TOP_K):
            pltpu.make_async_copy(hbuf.at[s], xs_hbm.at[pl.ds(0, tt * SUBLANES)], sem.at[s]).wait()

    @pl.when(i == 0)
    def _():
        load_table = pltpu.make_async_copy(pad_dst_hbm, dst_tab, tab_sem)
        load_table.start()
        load_table.wait()
        hbuf[1] = jnp.zeros(hbuf.shape[1:], F32)
        zeros = hbuf.at[1, pl.ds(0, blk * SUBLANES)]
        for e in range(zero_ref.shape[0]):
            @pl.when(zero_ref[e] >= 0)
            def _():
                first_row = pl.multiple_of(zero_ref[e] * (blk * SUBLANES), blk * SUBLANES)
                pltpu.make_async_copy(zeros, xs_hbm.at[pl.ds(first_row, blk * SUBLANES)], sem.at[1]).start()
        for e in range(zero_ref.shape[0]):
            @pl.when(zero_ref[e] >= 0)
            def _():
                pltpu.make_async_copy(zeros, xs_hbm.at[pl.ds(0, blk * SUBLANES)], sem.at[1]).wait()

    @pl.when(i >= 2)
    def _():
        wait_rows(slot)

    hbuf[slot] = h_ref[...]
    first_token = i * tt
    for t in range(tt):
        row = hbuf.at[slot, _row_tile(t)]
        for kk in range(TOP_K):
            p = pos_ref[0, 0, t * TOP_K + kk]
            dst_tab[p] = first_token + (kk * n_tokens + t)
            pltpu.make_async_copy(row, xs_hbm.at[_row_tile(p)], sem.at[slot]).start(kk % 2)

    @pl.when(i == n - 1)
    def _():
        wait_rows(slot)

        @pl.when(i >= 1)
        def _():
            wait_rows(1 - slot)

        store_table = pltpu.make_async_copy(dst_tab, dst_hbm, tab_sem)
        store_table.start()
        store_table.wait()


def _dispatch(h_all, pos, pad_dst, zero_blocks, tt, blk):
    T = pos.shape[1]
    n_slots = pad_dst.shape[0]
    n_tiles = T // tt
    assert T % tt == 0 and blk <= tt and DMA_UNROLL % TOP_K == 0
    pos3 = pos.T.reshape(n_tiles, 1, tt * TOP_K)
    grid_spec = pltpu.PrefetchScalarGridSpec(
        num_scalar_prefetch=1,
        grid=(n_tiles,),
        in_specs=[pl.BlockSpec((1, 1, tt * TOP_K), lambda i, zb: (i, 0, 0), memory_space=pltpu.SMEM),
                  pl.BlockSpec((tt * SUBLANES, LANES), lambda i, zb: (i, 0)),
                  pl.BlockSpec(memory_space=pl.ANY)],
        out_specs=[pl.BlockSpec(memory_space=pl.ANY), pl.BlockSpec(memory_space=pl.ANY)],
        scratch_shapes=[pltpu.VMEM((2, tt * SUBLANES, LANES), F32),
                        pltpu.SMEM((n_slots,), jnp.int32),
                        pltpu.SemaphoreType.DMA((2,)),
                        pltpu.SemaphoreType.DMA],
    )
    return pl.pallas_call(
        functools.partial(_dispatch_body, tt=tt, blk=blk, n_tokens=T),
        grid_spec=grid_spec,
        out_shape=[jax.ShapeDtypeStruct((n_slots * SUBLANES, LANES), F32),
                   jax.ShapeDtypeStruct((n_slots,), jnp.int32)],
        compiler_params=pltpu.CompilerParams(
            dimension_semantics=("arbitrary",), vmem_limit_bytes=VMEM_LIMIT_BYTES),
        name="dispatch",
    )(zero_blocks, pos3, h_all, pad_dst)


def _expert_body(be_ref, nv_ref, dst_ref, x_ref, wgu_ref, bgu_ref, wd_ref, bd_ref,
                 y_hbm, ybuf, osem, wgu_bf, wd_bf, xb_ref, act_ref, *, blk, n_real_rows):
    i = pl.program_id(0)
    n_valid = nv_ref[0]

    def scatter_row(s, r, prio=0):
        pltpu.make_async_copy(
            ybuf.at[s, _row_tile(r)], y_hbm.at[_row_tile(dst_ref[0, 0, r])], osem.at[s]).start(prio)

    def wait_scatter(s):
        pltpu.make_async_copy(ybuf.at[s], y_hbm.at[pl.ds(0, blk * SUBLANES)], osem.at[s]).wait()

    @pl.when(i == 0)
    def _():
        ybuf[...] = jnp.zeros(ybuf.shape, F32)
        for e in range(N_EXPERTS):
            pltpu.make_async_copy(
                ybuf.at[0], y_hbm.at[pl.ds((n_real_rows + e * blk) * SUBLANES, blk * SUBLANES)],
                osem.at[0]).start()
        for e in range(N_EXPERTS):
            wait_scatter(0)

    @pl.when((i < n_valid) & ((i == 0) | (be_ref[i] != be_ref[jnp.maximum(i - 1, 0)])))
    def _():
        wgu_bf[...] = wgu_ref[0].astype(BF16)
        wd_bf[...] = wd_ref[0].astype(BF16)

    rows_per_chunk = blk // N_CHUNKS
    cols = D_FF // N_CHUNKS

    def block_step(s):
        @pl.when(i < n_valid)
        def _():
            @pl.when(i >= 1)
            def _():
                wait_scatter(s)

            xb_ref[...] = _load_rows(x_ref, blk).astype(BF16)
            for c in range(N_CHUNKS):
                lo = c * cols
                xb = xb_ref[...]
                gate = jnp.dot(xb, wgu_bf[:, lo:lo + cols], preferred_element_type=F32)
                up = jnp.dot(xb, wgu_bf[:, D_FF + lo:D_FF + lo + cols], preferred_element_type=F32)
                gate = jnp.minimum(gate + bgu_ref[0, :, lo:lo + cols], SWIGLU_LIMIT)
                up = jnp.clip(up + bgu_ref[0, :, D_FF + lo:D_FF + lo + cols], -SWIGLU_LIMIT, SWIGLU_LIMIT)
                act = (up + 1.0) * gate * jax.nn.sigmoid(SWIGLU_ALPHA * gate)
                act_ref[:, lo:lo + cols] = act.astype(BF16)
                for r in range(c * rows_per_chunk, (c + 1) * rows_per_chunk):
                    scatter_row(1 - s, r, r % 2)
            y = jnp.dot(act_ref[...], wd_bf[...], preferred_element_type=F32) + bd_ref[0]
            _store_rows(ybuf.at[s], y, blk)

        @pl.when(i == n_valid)
        def _():
            wait_scatter(s)

            def body(gi, carry):
                for u in range(DMA_UNROLL):
                    scatter_row(1 - s, gi * DMA_UNROLL + u)
                return carry
            lax.fori_loop(0, blk // DMA_UNROLL, body, 0)
            wait_scatter(1 - s)

    for s in range(2):
        pl.when(i % 2 == s)(functools.partial(block_step, s))


def _experts(x_slots, dst, block_e, n_valid, w_gu, b_gu, w_down, b_down, blk, n_real_rows):
    n_blocks = block_e.shape[0]
    assert n_blocks * blk == n_real_rows + N_EXPERTS * blk and blk % (2 * N_CHUNKS) == 0
    n_out_rows = n_real_rows + (N_EXPERTS + 1) * blk
    priming = n_real_rows + N_EXPERTS * blk + jnp.arange(blk, dtype=jnp.int32)
    dst3 = jnp.concatenate([priming, dst]).reshape(n_blocks + 1, 1, blk)
    used = lambda i, nv: jnp.minimum(i, nv[0] - 1)
    grid_spec = pltpu.PrefetchScalarGridSpec(
        num_scalar_prefetch=2,
        grid=(n_blocks,),
        in_specs=[
            pl.BlockSpec((1, 1, blk), lambda i, be, nv: (i, 0, 0), memory_space=pltpu.SMEM),
            pl.BlockSpec((blk * SUBLANES, LANES), lambda i, be, nv: (used(i, nv), 0)),
            pl.BlockSpec((1, D_MODEL, 2 * D_FF), lambda i, be, nv: (be[i], 0, 0)),
            pl.BlockSpec((1, 1, 2 * D_FF), lambda i, be, nv: (be[i], 0, 0)),
            pl.BlockSpec((1, D_FF, D_MODEL), lambda i, be, nv: (be[i], 0, 0)),
            pl.BlockSpec((1, 1, D_MODEL), lambda i, be, nv: (be[i], 0, 0)),
        ],
        out_specs=pl.BlockSpec(memory_space=pl.ANY),
        scratch_shapes=[pltpu.VMEM((2, blk * SUBLANES, LANES), F32),
                        pltpu.SemaphoreType.DMA((2,)),
                        pltpu.VMEM((D_MODEL, 2 * D_FF), BF16),
                        pltpu.VMEM((D_FF, D_MODEL), BF16),
                        pltpu.VMEM((blk, D_MODEL), BF16),
                        pltpu.VMEM((blk, D_FF), BF16)],
    )
    return pl.pallas_call(
        functools.partial(_expert_body, blk=blk, n_real_rows=n_real_rows),
        grid_spec=grid_spec,
        out_shape=jax.ShapeDtypeStruct((n_out_rows * SUBLANES, LANES), F32),
        compiler_params=pltpu.CompilerParams(
            dimension_semantics=("arbitrary",), vmem_limit_bytes=VMEM_LIMIT_BYTES),
        name="experts",
    )(block_e, n_valid, dst3, x_slots, w_gu, b_gu.reshape(N_EXPERTS, 1, 2 * D_FF),
      w_down, b_down.reshape(N_EXPERTS, 1, D_MODEL))


def _combine_body(*refs, tt, n_first):
    y_refs = refs[:TOP_K]
    h_ref, gate_ref, g2_ref, b2_ref, out_first, out_second = refs[TOP_K:]
    i = pl.program_id(0)
    gates = gate_ref[...]
    f = gates[:, 0:1] * _load_rows(y_refs[0], tt)
    for kk in range(1, TOP_K):
        f = f + gates[:, kk:kk + 1] * _load_rows(y_refs[kk], tt)
    out = _layer_norm(DEEPNORM_ALPHA * _load_rows(h_ref, tt) + f, g2_ref[...], b2_ref[...])

    @pl.when(i < n_first)
    def _():
        out_first[...] = out

    @pl.when(i >= n_first)
    def _():
        out_second[...] = out


def _combine(y_rows, gates, h_all, ln2_g, ln2_b, tt, n_tokens_first):
    T = gates.shape[0]
    n_tiles = T // tt
    n_first = n_tokens_first // tt
    assert T % tt == 0 and n_tokens_first % tt == 0 and 0 < n_first < n_tiles
    row_blk = lambda imap: pl.BlockSpec((tt * SUBLANES, LANES), imap)
    y_specs = [row_blk(functools.partial(lambda i, kk: (kk * n_tiles + i, 0), kk=kk)) for kk in range(TOP_K)]
    return pl.pallas_call(
        functools.partial(_combine_body, tt=tt, n_first=n_first),
        grid=(n_tiles,),
        in_specs=y_specs + [
            row_blk(lambda i: (i, 0)),
            pl.BlockSpec((tt, TOP_K), lambda i: (i, 0)),
            pl.BlockSpec((1, D_MODEL), lambda i: (0, 0)),
            pl.BlockSpec((1, D_MODEL), lambda i: (0, 0)),
        ],
        out_specs=[pl.BlockSpec((tt, D_MODEL), lambda i: (jnp.minimum(i, n_first - 1), 0)),
                   pl.BlockSpec((tt, D_MODEL), lambda i: (jnp.maximum(i - n_first, 0), 0))],
        out_shape=[jax.ShapeDtypeStruct((n_tokens_first, D_MODEL), F32),
                   jax.ShapeDtypeStruct((T - n_tokens_first, D_MODEL), F32)],
        compiler_params=pltpu.CompilerParams(
            dimension_semantics=("arbitrary",), vmem_limit_bytes=VMEM_LIMIT_BYTES),
        name="combine",
    )(*([y_rows] * TOP_K), h_all, gates, ln2_g, ln2_b)


def kernel(x_prompt, x_sample, state_retention, state_conv, w_in, beta_ret, beta_conv, conv_w, conv_b,
           w_out, ln1_g, ln1_b, w_router, b_router, w_gu, b_gu, w_down, b_down, ln2_g, ln2_b):
    assert w_in.shape[0] == DEPTH == 1
    B, S, _ = x_prompt.shape
    Bd, Ld, _ = x_sample.shape
    Tp, Ts = B * S, Bd * Ld
    T = Tp + Ts
    wr_t = w_router[0].T
    wr_hi = wr_t.astype(BF16)
    wr_lo = (wr_t - wr_hi.astype(F32)).astype(BF16)
    w = dict(w_in=w_in[0].astype(BF16), w_out=w_out[0].astype(BF16), conv_w=conv_w[0],
             conv_b=conv_b[0][None], beta_ret=beta_ret[0][None], beta_conv=beta_conv[0][None],
             ln1_g=ln1_g[0][None], ln1_b=ln1_b[0][None], w_router=jnp.stack([wr_hi, wr_lo]),
             b_router=b_router[0][:, None])

    s_ret0 = jnp.zeros((B, N_RET_HEADS, HEAD_DIM, HEAD_DIM), F32)
    s_conv0 = jnp.zeros((B, CONV_W - 1, D_CONV), F32)
    h_s, choices_s, gates_s, ranks_s, counts_s, sret_s, sconv_s = _mixer(
        x_sample, state_retention[0], state_conv[0], PAST_LEN, SAMPLE_BATCH_BLOCK, Ld, w)
    h_all, choices_t, gates_t, ranks_t, counts_p, sret_p, sconv_p = _mixer(
        x_prompt, s_ret0, s_conv0, 0, 1, PROMPT_CHUNK, w, tail=(h_s, choices_s, gates_s, ranks_s))

    pos, pad_dst, block_e, n_valid, zero_blocks = _routing_plan(
        choices_t, ranks_t, counts_p[:, 0].astype(jnp.int32), counts_s[:, 0].astype(jnp.int32), Tp,
        EXPERT_BLOCK)
    x_slots, dst = _dispatch(h_all, pos, pad_dst, zero_blocks, DISPATCH_TILE, EXPERT_BLOCK)
    y_rows = _experts(x_slots, dst, block_e, n_valid, w_gu[0], b_gu[0], w_down[0], b_down[0],
                      EXPERT_BLOCK, T * TOP_K)
    y_p, y_s = _combine(y_rows, gates_t.T, h_all, ln2_g[0][None], ln2_b[0][None], COMBINE_TILE, Tp)

    return (y_p.reshape(B, S, D_MODEL), y_s.reshape(Bd, Ld, D_MODEL),
            sret_p[None], sconv_p[None], sret_s[None], sconv_s[None])
```

```python
import functools

import jax
import jax.numpy as jnp
from jax import lax
from jax.experimental import pallas as pl
from jax.experimental.pallas import tpu as pltpu

D_MODEL = 1024
N_RET_HEADS = 4
HEAD_DIM = 128
D_RET = N_RET_HEADS * HEAD_DIM
D_CONV = D_MODEL - D_RET
CONV_W = 3
N_EXPERTS = 32
TOP_K = 4
D_FF = D_MODEL
SWIGLU_LIMIT = 7.0
SWIGLU_ALPHA = 1.702
ROPE_BASE = 10000.0
LN_EPS = 1e-5
GN_EPS = 1e-5
DEPTH = 1
PAST_LEN = 1024
DEEPNORM_ALPHA = (2.0 * DEPTH) ** 0.25

LANES = 128
SUBLANES = 8
ROW_CHUNKS = D_MODEL // LANES
assert ROW_CHUNKS == SUBLANES

PROMPT_CHUNK = 256
SAMPLE_BATCH_BLOCK = 8
EXPERT_BLOCK = 512
DISPATCH_TILE = 512
COMBINE_TILE = 512
DMA_UNROLL = 8
N_CHUNKS = 4
VMEM_LIMIT_BYTES = 56 * 1024 * 1024

F32 = jnp.float32
BF16 = jnp.bfloat16


def _layer_norm(x, g, b):
    mu = jnp.mean(x, axis=-1, keepdims=True)
    xc = x - mu
    var = jnp.mean(xc * xc, axis=-1, keepdims=True)
    return xc * lax.rsqrt(var + LN_EPS) * g + b


def _load_rows(ref, n_rows):
    return jnp.concatenate(
        [ref[pl.ds(j, n_rows, stride=SUBLANES), :] for j in range(ROW_CHUNKS)], axis=1)


def _store_rows(ref, val, n_rows):
    for j in range(ROW_CHUNKS):
        ref[pl.ds(j, n_rows, stride=SUBLANES), :] = val[:, j * LANES:(j + 1) * LANES]


def _row_tile(r):
    if isinstance(r, int):
        return pl.ds(r * SUBLANES, SUBLANES)
    return pl.ds(pl.multiple_of(r * SUBLANES, SUBLANES), SUBLANES)


N_MIXER_INPUTS = 19
N_TOKEN_OUTPUTS = 4


def _mixer_body(*refs, nb, C, n_batch_steps, n_tail):
    if not n_tail:
        _mixer_compute(*refs, nb=nb, C=C)
        return
    tails = refs[N_MIXER_INPUTS:N_MIXER_INPUTS + N_TOKEN_OUTPUTS]
    outs = refs[N_MIXER_INPUTS + N_TOKEN_OUTPUTS:N_MIXER_INPUTS + 2 * N_TOKEN_OUTPUTS]
    bstep = pl.program_id(0)

    @pl.when(bstep < n_batch_steps)
    def _():
        _mixer_compute(*refs[:N_MIXER_INPUTS], *refs[N_MIXER_INPUTS + N_TOKEN_OUTPUTS:], nb=nb, C=C)

    @pl.when((bstep == n_batch_steps) & (pl.program_id(1) < n_tail))
    def _():
        for tail_ref, out_ref in zip(tails, outs):
            out_ref[...] = tail_ref[...]


def _route(logits_t, run_ref):
    n_tok = logits_t.shape[1]
    expert = lax.broadcasted_iota(jnp.int32, logits_t.shape, 0)
    vals, idxs, hots = [], [], []
    for _ in range(TOP_K):
        m = jnp.max(logits_t, axis=0, keepdims=True)
        idx = jnp.min(jnp.where(logits_t == m, expert, N_EXPERTS), axis=0, keepdims=True)
        hot = expert == idx
        vals.append(m)
        idxs.append(idx)
        hots.append(hot)
        logits_t = jnp.where(hot, -jnp.inf, logits_t)
    ex = [jnp.exp(v - vals[0]) for v in vals]
    total = ex[0]
    for e in ex[1:]:
        total = total + e
    gates = jnp.concatenate([e / total for e in ex], axis=0)

    chosen = hots[0].astype(F32)
    for hot in hots[1:]:
        chosen = chosen + hot.astype(F32)
    chosen = chosen.astype(BF16)
    t_row = lax.broadcasted_iota(jnp.int32, (n_tok, n_tok), 0)
    t_col = lax.broadcasted_iota(jnp.int32, (n_tok, n_tok), 1)
    earlier = jnp.where(t_row < t_col, 1.0, 0.0).astype(BF16)
    before = run_ref[...] + jnp.dot(chosen, earlier, preferred_element_type=F32)
    ranks = [jnp.sum(jnp.where(hot, before, 0.0), axis=0, keepdims=True) for hot in hots]
    run_ref[...] = run_ref[...] + jnp.dot(chosen, jnp.ones((n_tok, n_tok), BF16), preferred_element_type=F32)
    return jnp.concatenate(idxs, axis=0), gates, jnp.concatenate(ranks, axis=0).astype(jnp.int32)


def _mixer_compute(x_ref, cos_ref, sin_ref, decay_ref, qdec_ref, kdec_ref, sdec_ref, win_ref, wout_ref,
                   convw_ref, convb_ref, bret_ref, bconv_ref, g1_ref, b1_ref, wr_ref, br_ref,
                   sret_ref, sconv_ref, h_ref, choice_ref, gate_ref, rank_ref, count_ref, sret_out, sconv_out,
                   mix_ref, run_ref, *, nb, C):
    c = pl.program_id(1)

    @pl.when((pl.program_id(0) == 0) & (c == 0))
    def _():
        run_ref[...] = jnp.zeros(run_ref.shape, F32)

    @pl.when(c == 0)
    def _():
        sret_out[...] = sret_ref[...]
        sconv_out[...] = sconv_ref[...]

    x = x_ref[...].reshape(nb * C, D_MODEL)
    xb = x.astype(BF16)

    def proj(col0, width):
        return jnp.dot(xb, win_ref[:, col0:col0 + width], preferred_element_type=F32)

    q = proj(0, D_RET)
    k = proj(D_RET, D_RET)
    v = proj(2 * D_RET, D_RET)
    g = proj(3 * D_RET, D_RET)
    bg = proj(4 * D_RET, D_CONV)
    cg = proj(4 * D_RET + D_CONV, D_CONV)
    hc = proj(4 * D_RET + 2 * D_CONV, D_CONV)

    cos = cos_ref[...]
    sin = sin_ref[...]
    row = lax.broadcasted_iota(jnp.int32, (C, D_CONV), 0)
    k_scale = HEAD_DIM ** -0.5

    for b in range(nb):
        r0 = b * C
        for h in range(N_RET_HEADS):
            c0 = h * HEAD_DIM
            qh = q[r0:r0 + C, c0:c0 + HEAD_DIM]
            kh = k[r0:r0 + C, c0:c0 + HEAD_DIM]
            vh = v[r0:r0 + C, c0:c0 + HEAD_DIM]
            qh = qh * cos + pltpu.roll(qh, HEAD_DIM // 2, axis=1) * sin
            kh = (kh * cos + pltpu.roll(kh, HEAD_DIM // 2, axis=1) * sin) * k_scale
            qb = qh.astype(BF16)
            kb = kh.astype(BF16)
            vb = vh.astype(BF16)
            s_old = sret_out[b, h]
            scores = lax.dot_general(qb, kb, (((1,), (1,)), ((), ())), preferred_element_type=F32)
            scores = scores * decay_ref[h]
            intra = jnp.dot(scores.astype(BF16), vb, preferred_element_type=F32)
            cross = jnp.dot(qb, s_old.astype(BF16), preferred_element_type=F32) * qdec_ref[h]
            o = intra + cross
            kd = (kh * kdec_ref[h]).astype(BF16)
            s_new = sdec_ref[h] * s_old + lax.dot_general(
                kd, vb, (((0,), (0,)), ((), ())), preferred_element_type=F32)
            sret_out[b, h] = s_new
            mu = jnp.mean(o, axis=-1, keepdims=True)
            oc = o - mu
            var = jnp.mean(oc * oc, axis=-1, keepdims=True)
            on = oc * lax.rsqrt(var + GN_EPS)
            gh = g[r0:r0 + C, c0:c0 + HEAD_DIM]
            ret = on * (gh * jax.nn.sigmoid(gh)) * bret_ref[:, c0:c0 + HEAD_DIM]
            mix_ref[r0:r0 + C, c0:c0 + HEAD_DIM] = ret.astype(BF16)

        u = cg[r0:r0 + C] * hc[r0:r0 + C]
        prev = sconv_out[b]
        u1 = jnp.where(row == 0, prev[1:2], pltpu.roll(u, 1, axis=0))
        u2 = jnp.where(row == 0, prev[0:1], jnp.where(row == 1, prev[1:2], pltpu.roll(u, 2, axis=0)))
        z = convb_ref[...] + convw_ref[0:1] * u2 + convw_ref[1:2] * u1 + convw_ref[2:3] * u
        conv_out = bg[r0:r0 + C] * z * bconv_ref[...]
        mix_ref[r0:r0 + C, D_RET:D_RET + D_CONV] = conv_out.astype(BF16)
        sconv_out[b] = u[C - 2:C]

    m = jnp.dot(mix_ref[...], wout_ref[...], preferred_element_type=F32)
    hval = _layer_norm(DEEPNORM_ALPHA * x + m, g1_ref[...], b1_ref[...])
    _store_rows(h_ref, hval, nb * C)
    h_hi = hval.astype(BF16)
    h_lo = (hval - h_hi.astype(F32)).astype(BF16)
    nt_dot = lambda a, b: lax.dot_general(a, b, (((1,), (1,)), ((), ())), preferred_element_type=F32)
    logits_t = nt_dot(wr_ref[0], h_hi) + nt_dot(wr_ref[0], h_lo) + nt_dot(wr_ref[1], h_hi) + br_ref[...]
    choice_ref[...], gate_ref[...], rank_ref[...] = _route(logits_t, run_ref)
    count_ref[...] = run_ref[:, :LANES]


def _retention_tables(C, pos0, L):
    lg = jnp.log1p(-jnp.power(2.0, -5.0 - jnp.arange(N_RET_HEADS, dtype=F32)))
    i = jnp.arange(C, dtype=F32)
    diff = i[:, None] - i[None, :]
    decay = jnp.where(diff[None] >= 0, jnp.exp(lg[:, None, None] * jnp.maximum(diff, 0.0)[None]), 0.0)
    qdec = jnp.exp(lg[:, None] * (i + 1.0)[None, :])
    kdec = jnp.exp(lg[:, None] * (C - 1.0 - i)[None, :])
    sdec = jnp.exp(lg * C)
    qdec = jnp.broadcast_to(qdec[:, :, None], (N_RET_HEADS, C, HEAD_DIM))
    kdec = jnp.broadcast_to(kdec[:, :, None], (N_RET_HEADS, C, HEAD_DIM))
    sdec = jnp.broadcast_to(sdec[:, None, None], (N_RET_HEADS, 1, HEAD_DIM))
    pos = pos0 + jnp.arange(L, dtype=jnp.int32)
    inv_freq = ROPE_BASE ** (-jnp.arange(0, HEAD_DIM, 2, dtype=F32) / HEAD_DIM)
    ang = pos.astype(F32)[:, None] * inv_freq[None, :]
    cos = jnp.cos(ang)
    sin = jnp.sin(ang)
    cos_full = jnp.concatenate([cos, cos], axis=-1)
    sin_signed = jnp.concatenate([-sin, sin], axis=-1)
    return cos_full, sin_signed, decay, qdec, kdec, sdec


def _mixer(x, s_ret, s_conv, pos0, nb, C, w, tail=None):
    B, L, _ = x.shape
    nc = L // C
    nbs = B // nb
    rows = nb * C
    n_tokens = B * L
    n_tail = 0
    if tail is not None:
        n_tokens += tail[1].shape[1]
        n_tail = tail[1].shape[1] // rows
        assert tail[1].shape[1] % rows == 0 and 0 < n_tail <= nc
    cos, sin, decay, qdec, kdec, sdec = _retention_tables(C, pos0, L)

    bb = lambda b: jnp.minimum(b, nbs - 1)
    cc = lambda b, c: jnp.where(b < nbs, c, nc - 1)
    out_blk = lambda b, c: jnp.where(b < nbs, b * nc + c, nbs * nc + jnp.minimum(c, n_tail - 1))
    const2 = lambda b, c: (0, 0)
    const3 = lambda b, c: (0, 0, 0)
    full = lambda a: pl.BlockSpec(a.shape, const2 if a.ndim == 2 else const3)
    per_token = lambda imap: pl.BlockSpec((TOP_K, rows), lambda b, c: (0, imap(b, c)))
    in_arrays = [x, cos, sin, decay, qdec, kdec, sdec, w["w_in"], w["w_out"], w["conv_w"], w["conv_b"],
                 w["beta_ret"], w["beta_conv"], w["ln1_g"], w["ln1_b"], w["w_router"], w["b_router"],
                 s_ret, s_conv]
    assert len(in_arrays) == N_MIXER_INPUTS
    in_specs = [pl.BlockSpec((nb, C, D_MODEL), lambda b, c: (bb(b), cc(b, c), 0)),
                pl.BlockSpec((C, HEAD_DIM), lambda b, c: (cc(b, c), 0)),
                pl.BlockSpec((C, HEAD_DIM), lambda b, c: (cc(b, c), 0))]
    in_specs += [full(a) for a in in_arrays[3:17]]
    in_specs += [pl.BlockSpec((nb, N_RET_HEADS, HEAD_DIM, HEAD_DIM), lambda b, c: (bb(b), 0, 0, 0)),
                 pl.BlockSpec((nb, CONV_W - 1, D_CONV), lambda b, c: (bb(b), 0, 0))]
    if n_tail:
        tail_blk = lambda b, c: jnp.where(b < nbs, 0, jnp.minimum(c, n_tail - 1))
        in_arrays += list(tail)
        in_specs += [pl.BlockSpec((rows * SUBLANES, LANES), lambda b, c: (tail_blk(b, c), 0))]
        in_specs += [per_token(tail_blk)] * (N_TOKEN_OUTPUTS - 1)
    out_shape = [jax.ShapeDtypeStruct((n_tokens * SUBLANES, LANES), F32),
                 jax.ShapeDtypeStruct((TOP_K, n_tokens), jnp.int32),
                 jax.ShapeDtypeStruct((TOP_K, n_tokens), F32),
                 jax.ShapeDtypeStruct((TOP_K, n_tokens), jnp.int32),
                 jax.ShapeDtypeStruct((N_EXPERTS, LANES), F32),
                 jax.ShapeDtypeStruct((B, N_RET_HEADS, HEAD_DIM, HEAD_DIM), F32),
                 jax.ShapeDtypeStruct((B, CONV_W - 1, D_CONV), F32)]
    assert len(out_shape) == N_TOKEN_OUTPUTS + 3
    out_specs = [pl.BlockSpec((rows * SUBLANES, LANES), lambda b, c: (out_blk(b, c), 0))]
    out_specs += [per_token(out_blk)] * (N_TOKEN_OUTPUTS - 1)
    out_specs += [pl.BlockSpec((N_EXPERTS, LANES), const2),
                  pl.BlockSpec((nb, N_RET_HEADS, HEAD_DIM, HEAD_DIM), lambda b, c: (bb(b), 0, 0, 0)),
                  pl.BlockSpec((nb, CONV_W - 1, D_CONV), lambda b, c: (bb(b), 0, 0))]
    return pl.pallas_call(
        functools.partial(_mixer_body, nb=nb, C=C, n_batch_steps=nbs, n_tail=n_tail),
        grid=(nbs + (1 if n_tail else 0), nc), in_specs=in_specs, out_specs=out_specs, out_shape=out_shape,
        scratch_shapes=[pltpu.VMEM((rows, D_MODEL), BF16), pltpu.VMEM((N_EXPERTS, rows), F32)],
        compiler_params=pltpu.CompilerParams(
            dimension_semantics=("arbitrary", "arbitrary"), vmem_limit_bytes=VMEM_LIMIT_BYTES),
        name="mixer",
    )(*in_arrays)


def _lookup(table, choices):
    experts = jnp.arange(N_EXPERTS, dtype=jnp.int32)
    return jnp.sum(jnp.where(choices[..., None] == experts, table, 0), axis=-1)


def _routing_plan(choices_t, ranks_t, counts_first, counts_second, n_first, blk):
    T = choices_t.shape[1]
    A = T * TOP_K
    assert A % blk == 0
    n_blocks = A // blk + N_EXPERTS
    experts = jnp.arange(N_EXPERTS, dtype=jnp.int32)
    counts = counts_first + counts_second
    nblk_e = (counts + blk - 1) // blk
    bend = jnp.cumsum(nblk_e)
    n_valid = bend[-1]
    start = (bend - nblk_e) * blk
    token = jnp.arange(T, dtype=jnp.int32)[None, :]
    rank = ranks_t + jnp.where(token >= n_first, _lookup(counts_first, choices_t), 0)
    pos = _lookup(start, choices_t) + rank

    bi = jnp.arange(n_blocks, dtype=jnp.int32)
    block_e = jnp.sum((bi[:, None] >= bend[None, :]).astype(jnp.int32), axis=1)
    block_e = jnp.where(bi < n_valid, block_e, block_e[n_valid - 1])
    pad_dst = (A + block_e[:, None] * blk + jnp.arange(blk, dtype=jnp.int32)[None, :]).reshape(n_blocks * blk)
    last_block = jnp.where(nblk_e > 0, bend - 1, -1)
    unused = n_valid + experts
    unused = jnp.where(unused < n_blocks, unused, -1)
    zero_blocks = jnp.concatenate([last_block, unused]).astype(jnp.int32)
    return pos, pad_dst, block_e, n_valid.reshape(1).astype(jnp.int32), zero_blocks


def _dispatch_body(zero_ref, pos_ref, h_ref, pad_dst_hbm, xs_hbm, dst_hbm, hbuf, dst_tab, sem, tab_sem,
                   *, tt, blk, n_tokens):
    i = pl.program_id(0)
    n = pl.num_programs(0)
    slot = i % 2

    def wait_rows(s):
        for _ in range(TOP_K):
            pltpu.make_async_copy(hbuf.at[s], xs_hbm.at[pl.ds(0, tt * SUBLANES)], sem.at[s]).wait()

    @pl.when(i == 0)
    def _():
        load_table = pltpu.make_async_copy(pad_dst_hbm, dst_tab, tab_sem)
        load_table.start()
        load_table.wait()
        hbuf[1] = jnp.zeros(hbuf.shape[1:], F32)
        zeros = hbuf.at[1, pl.ds(0, blk * SUBLANES)]
        for e in range(zero_ref.shape[0]):
            @pl.when(zero_ref[e] >= 0)
            def _():
                first_row = pl.multiple_of(zero_ref[e] * (blk * SUBLANES), blk * SUBLANES)
                pltpu.make_async_copy(zeros, xs_hbm.at[pl.ds(first_row, blk * SUBLANES)], sem.at[1]).start()
        for e in range(zero_ref.shape[0]):
            @pl.when(zero_ref[e] >= 0)
            def _():
                pltpu.make_async_copy(zeros, xs_hbm.at[pl.ds(0, blk * SUBLANES)], sem.at[1]).wait()

    @pl.when(i >= 2)
    def _():
        wait_rows(slot)

    hbuf[slot] = h_ref[...]
    first_token = i * tt
    for t in range(tt):
        row = hbuf.at[slot, _row_tile(t)]
        for kk in range(TOP_K):
            p = pos_ref[kk, t]
            dst_tab[p] = first_token + (kk * n_tokens + t)
            pltpu.make_async_copy(row, xs_hbm.at[_row_tile(p)], sem.at[slot]).start(kk % 2)

    @pl.when(i == n - 1)
    def _():
        wait_rows(slot)

        @pl.when(i >= 1)
        def _():
            wait_rows(1 - slot)

        store_table = pltpu.make_async_copy(dst_tab, dst_hbm, tab_sem)
        store_table.start()
        store_table.wait()


def _dispatch(h_all, pos, pad_dst, zero_blocks, tt, blk):
    T = pos.shape[1]
    n_slots = pad_dst.shape[0]
    n_tiles = T // tt
    assert T % tt == 0 and blk <= tt
    grid_spec = pltpu.PrefetchScalarGridSpec(
        num_scalar_prefetch=1,
        grid=(n_tiles,),
        in_specs=[pl.BlockSpec((TOP_K, tt), lambda i, zb: (0, i), memory_space=pltpu.SMEM),
                  pl.BlockSpec((tt * SUBLANES, LANES), lambda i, zb: (i, 0)),
                  pl.BlockSpec(memory_space=pl.ANY)],
        out_specs=[pl.BlockSpec(memory_space=pl.ANY), pl.BlockSpec(memory_space=pl.ANY)],
        scratch_shapes=[pltpu.VMEM((2, tt * SUBLANES, LANES), F32),
                        pltpu.SMEM((n_slots,), jnp.int32),
                        pltpu.SemaphoreType.DMA((2,)),
                        pltpu.SemaphoreType.DMA],
    )
    return pl.pallas_call(
        functools.partial(_dispatch_body, tt=tt, blk=blk, n_tokens=T),
        grid_spec=grid_spec,
        out_shape=[jax.ShapeDtypeStruct((n_slots * SUBLANES, LANES), F32),
                   jax.ShapeDtypeStruct((n_slots,), jnp.int32)],
        compiler_params=pltpu.CompilerParams(
            dimension_semantics=("arbitrary",), vmem_limit_bytes=VMEM_LIMIT_BYTES),
        name="dispatch",
    )(zero_blocks, pos, h_all, pad_dst)


def _expert_body(be_ref, nv_ref, dst_ref, x_ref, wgu_ref, bgu_ref, wd_ref, bd_ref,
                 y_hbm, ybuf, osem, wgu_bf, wd_bf, xb_ref, act_ref, *, blk, n_real_rows):
    i = pl.program_id(0)
    n_valid = nv_ref[0]

    def scatter_row(s, r, prio=0):
        pltpu.make_async_copy(
            ybuf.at[s, _row_tile(r)], y_hbm.at[_row_tile(dst_ref[0, 0, r])], osem.at[s]).start(prio)

    def wait_scatter(s):
        pltpu.make_async_copy(ybuf.at[s], y_hbm.at[pl.ds(0, blk * SUBLANES)], osem.at[s]).wait()

    @pl.when(i == 0)
    def _():
        ybuf[...] = jnp.zeros(ybuf.shape, F32)
        for e in range(N_EXPERTS):
            pltpu.make_async_copy(
                ybuf.at[0], y_hbm.at[pl.ds((n_real_rows + e * blk) * SUBLANES, blk * SUBLANES)],
                osem.at[0]).start()
        for e in range(N_EXPERTS):
            wait_scatter(0)

    @pl.when((i < n_valid) & ((i == 0) | (be_ref[i] != be_ref[jnp.maximum(i - 1, 0)])))
    def _():
        wgu_bf[...] = wgu_ref[0].astype(BF16)
        wd_bf[...] = wd_ref[0].astype(BF16)

    rows_per_chunk = blk // N_CHUNKS
    cols = D_FF // N_CHUNKS

    def block_step(s):
        @pl.when(i < n_valid)
        def _():
            @pl.when(i >= 1)
            def _():
                wait_scatter(s)

            xb_ref[...] = _load_rows(x_ref, blk).astype(BF16)
            for c in range(N_CHUNKS):
                lo = c * cols
                xb = xb_ref[...]
                gate = jnp.dot(xb, wgu_bf[:, lo:lo + cols], preferred_element_type=F32)
                up = jnp.dot(xb, wgu_bf[:, D_FF + lo:D_FF + lo + cols], preferred_element_type=F32)
                gate = jnp.minimum(gate + bgu_ref[0, :, lo:lo + cols], SWIGLU_LIMIT)
                up = jnp.clip(up + bgu_ref[0, :, D_FF + lo:D_FF + lo + cols], -SWIGLU_LIMIT, SWIGLU_LIMIT)
                act = (up + 1.0) * gate * jax.nn.sigmoid(SWIGLU_ALPHA * gate)
                act_ref[:, lo:lo + cols] = act.astype(BF16)
                for r in range(c * rows_per_chunk, (c + 1) * rows_per_chunk):
                    scatter_row(1 - s, r, r % 2)
            y = jnp.dot(act_ref[...], wd_bf[...], preferred_element_type=F32) + bd_ref[0]
            _store_rows(ybuf.at[s], y, blk)

        @pl.when(i == n_valid)
        def _():
            wait_scatter(s)

            def body(gi, carry):
                for u in range(DMA_UNROLL):
                    scatter_row(1 - s, gi * DMA_UNROLL + u)
                return carry
            lax.fori_loop(0, blk // DMA_UNROLL, body, 0)
            wait_scatter(1 - s)

    for s in range(2):
        pl.when(i % 2 == s)(functools.partial(block_step, s))


def _experts(x_slots, dst, block_e, n_valid, w_gu, b_gu, w_down, b_down, blk, n_real_rows):
    n_blocks = block_e.shape[0]
    assert n_blocks * blk == n_real_rows + N_EXPERTS * blk and blk % (2 * N_CHUNKS) == 0
    n_out_rows = n_real_rows + (N_EXPERTS + 1) * blk
    priming = n_real_rows + N_EXPERTS * blk + jnp.arange(blk, dtype=jnp.int32)
    dst3 = jnp.concatenate([priming, dst]).reshape(n_blocks + 1, 1, blk)
    used = lambda i, nv: jnp.minimum(i, nv[0] - 1)
    grid_spec = pltpu.PrefetchScalarGridSpec(
        num_scalar_prefetch=2,
        grid=(n_blocks,),
        in_specs=[
            pl.BlockSpec((1, 1, blk), lambda i, be, nv: (i, 0, 0), memory_space=pltpu.SMEM),
            pl.BlockSpec((blk * SUBLANES, LANES), lambda i, be, nv: (used(i, nv), 0)),
            pl.BlockSpec((1, D_MODEL, 2 * D_FF), lambda i, be, nv: (be[i], 0, 0)),
            pl.BlockSpec((1, 1, 2 * D_FF), lambda i, be, nv: (be[i], 0, 0)),
            pl.BlockSpec((1, D_FF, D_MODEL), lambda i, be, nv: (be[i], 0, 0)),
            pl.BlockSpec((1, 1, D_MODEL), lambda i, be, nv: (be[i], 0, 0)),
        ],
        out_specs=pl.BlockSpec(memory_space=pl.ANY),
        scratch_shapes=[pltpu.VMEM((2, blk * SUBLANES, LANES), F32),
                        pltpu.SemaphoreType.DMA((2,)),
                        pltpu.VMEM((D_MODEL, 2 * D_FF), BF16),
                        pltpu.VMEM((D_FF, D_MODEL), BF16),
                        pltpu.VMEM((blk, D_MODEL), BF16),
                        pltpu.VMEM((blk, D_FF), BF16)],
    )
    return pl.pallas_call(
        functools.partial(_expert_body, blk=blk, n_real_rows=n_real_rows),
        grid_spec=grid_spec,
        out_shape=jax.ShapeDtypeStruct((n_out_rows * SUBLANES, LANES), F32),
        compiler_params=pltpu.CompilerParams(
            dimension_semantics=("arbitrary",), vmem_limit_bytes=VMEM_LIMIT_BYTES),
        name="experts",
    )(block_e, n_valid, dst3, x_slots, w_gu, b_gu.reshape(N_EXPERTS, 1, 2 * D_FF),
      w_down, b_down.reshape(N_EXPERTS, 1, D_MODEL))


def _combine_body(*refs, tt, n_first):
    y_refs = refs[:TOP_K]
    h_ref, gate_ref, g2_ref, b2_ref, out_first, out_second = refs[TOP_K:]
    i = pl.program_id(0)
    gates = gate_ref[...]
    f = gates[:, 0:1] * _load_rows(y_refs[0], tt)
    for kk in range(1, TOP_K):
        f = f + gates[:, kk:kk + 1] * _load_rows(y_refs[kk], tt)
    out = _layer_norm(DEEPNORM_ALPHA * _load_rows(h_ref, tt) + f, g2_ref[...], b2_ref[...])

    @pl.when(i < n_first)
    def _():
        out_first[...] = out

    @pl.when(i >= n_first)
    def _():
        out_second[...] = out


def _combine(y_rows, gates, h_all, ln2_g, ln2_b, tt, n_tokens_first):
    T = gates.shape[0]
    n_tiles = T // tt
    n_first = n_tokens_first // tt
    assert T % tt == 0 and n_tokens_first % tt == 0 and 0 < n_first < n_tiles
    row_blk = lambda imap: pl.BlockSpec((tt * SUBLANES, LANES), imap)
    y_specs = [row_blk(functools.partial(lambda i, kk: (kk * n_tiles + i, 0), kk=kk)) for kk in range(TOP_K)]
    return pl.pallas_call(
        functools.partial(_combine_body, tt=tt, n_first=n_first),
        grid=(n_tiles,),
        in_specs=y_specs + [
            row_blk(lambda i: (i, 0)),
            pl.BlockSpec((tt, TOP_K), lambda i: (i, 0)),
            pl.BlockSpec((1, D_MODEL), lambda i: (0, 0)),
            pl.BlockSpec((1, D_MODEL), lambda i: (0, 0)),
        ],
        out_specs=[pl.BlockSpec((tt, D_MODEL), lambda i: (jnp.minimum(i, n_first - 1), 0)),
                   pl.BlockSpec((tt, D_MODEL), lambda i: (jnp.maximum(i - n_first, 0), 0))],
        out_shape=[jax.ShapeDtypeStruct((n_tokens_first, D_MODEL), F32),
                   jax.ShapeDtypeStruct((T - n_tokens_first, D_MODEL), F32)],
        compiler_params=pltpu.CompilerParams(
            dimension_semantics=("arbitrary",), vmem_limit_bytes=VMEM_LIMIT_BYTES),
        name="combine",
    )(*([y_rows] * TOP_K), h_all, gates, ln2_g, ln2_b)


def kernel(x_prompt, x_sample, state_retention, state_conv, w_in, beta_ret, beta_conv, conv_w, conv_b,
           w_out, ln1_g, ln1_b, w_router, b_router, w_gu, b_gu, w_down, b_down, ln2_g, ln2_b):
    assert w_in.shape[0] == DEPTH == 1
    B, S, _ = x_prompt.shape
    Bd, Ld, _ = x_sample.shape
    Tp, Ts = B * S, Bd * Ld
    T = Tp + Ts
    wr_t = w_router[0].T
    wr_hi = wr_t.astype(BF16)
    wr_lo = (wr_t - wr_hi.astype(F32)).astype(BF16)
    w = dict(w_in=w_in[0].astype(BF16), w_out=w_out[0].astype(BF16), conv_w=conv_w[0],
             conv_b=conv_b[0][None], beta_ret=beta_ret[0][None], beta_conv=beta_conv[0][None],
             ln1_g=ln1_g[0][None], ln1_b=ln1_b[0][None], w_router=jnp.stack([wr_hi, wr_lo]),
             b_router=b_router[0][:, None])

    s_ret0 = jnp.zeros((B, N_RET_HEADS, HEAD_DIM, HEAD_DIM), F32)
    s_conv0 = jnp.zeros((B, CONV_W - 1, D_CONV), F32)
    h_s, choices_s, gates_s, ranks_s, counts_s, sret_s, sconv_s = _mixer(
        x_sample, state_retention[0], state_conv[0], PAST_LEN, SAMPLE_BATCH_BLOCK, Ld, w)
    h_all, choices_t, gates_t, ranks_t, counts_p, sret_p, sconv_p = _mixer(
        x_prompt, s_ret0, s_conv0, 0, 1, PROMPT_CHUNK, w, tail=(h_s, choices_s, gates_s, ranks_s))

    pos, pad_dst, block_e, n_valid, zero_blocks = _routing_plan(
        choices_t, ranks_t, counts_p[:, 0].astype(jnp.int32), counts_s[:, 0].astype(jnp.int32), Tp,
        EXPERT_BLOCK)
    x_slots, dst = _dispatch(h_all, pos, pad_dst, zero_blocks, DISPATCH_TILE, EXPERT_BLOCK)
    y_rows = _experts(x_slots, dst, block_e, n_valid, w_gu[0], b_gu[0], w_down[0], b_down[0],
                      EXPERT_BLOCK, T * TOP_K)
    y_p, y_s = _combine(y_rows, gates_t.T, h_all, ln2_g[0][None], ln2_b[0][None], COMBINE_TILE, Tp)

    return (y_p.reshape(B, S, D_MODEL), y_s.reshape(Bd, Ld, D_MODEL),
            sret_p[None], sconv_p[None], sret_s[None], sconv_s[None])
```

```python
import functools

import jax
import jax.numpy as jnp
from jax import lax
from jax.experimental import pallas as pl
from jax.experimental.pallas import tpu as pltpu

D_MODEL = 1024
N_RET_HEADS = 4
HEAD_DIM = 128
D_RET = N_RET_HEADS * HEAD_DIM
D_CONV = D_MODEL - D_RET
CONV_W = 3
N_EXPERTS = 32
TOP_K = 4
D_FF = D_MODEL
SWIGLU_LIMIT = 7.0
SWIGLU_ALPHA = 1.702
ROPE_BASE = 10000.0
LN_EPS = 1e-5
GN_EPS = 1e-5
DEPTH = 1
PAST_LEN = 1024
DEEPNORM_ALPHA = (2.0 * DEPTH) ** 0.25

LANES = 128
SUBLANES = 8
ROW_CHUNKS = D_MODEL // LANES
assert ROW_CHUNKS == SUBLANES

PROMPT_CHUNK = 256
SAMPLE_BATCH_BLOCK = 8
EXPERT_BLOCK = 512
DISPATCH_TILE = 512
COMBINE_TILE = 512
DMA_UNROLL = 8
N_CHUNKS = 4
VMEM_LIMIT_BYTES = 56 * 1024 * 1024

F32 = jnp.float32
BF16 = jnp.bfloat16


def _layer_norm(x, g, b):
    mu = jnp.mean(x, axis=-1, keepdims=True)
    xc = x - mu
    var = jnp.mean(xc * xc, axis=-1, keepdims=True)
    return xc * lax.rsqrt(var + LN_EPS) * g + b


def _load_rows(ref, n_rows):
    return jnp.concatenate(
        [ref[pl.ds(j, n_rows, stride=SUBLANES), :] for j in range(ROW_CHUNKS)], axis=1)


def _store_rows(ref, val, n_rows):
    for j in range(ROW_CHUNKS):
        ref[pl.ds(j, n_rows, stride=SUBLANES), :] = val[:, j * LANES:(j + 1) * LANES]


def _row_tile(r):
    if isinstance(r, int):
        return pl.ds(r * SUBLANES, SUBLANES)
    return pl.ds(pl.multiple_of(r * SUBLANES, SUBLANES), SUBLANES)


N_MIXER_INPUTS = 19
N_TOKEN_OUTPUTS = 4


def _mixer_body(*refs, nb, C, n_batch_steps, n_tail):
    if not n_tail:
        _mixer_compute(*refs, nb=nb, C=C)
        return
    tails = refs[N_MIXER_INPUTS:N_MIXER_INPUTS + N_TOKEN_OUTPUTS]
    outs = refs[N_MIXER_INPUTS + N_TOKEN_OUTPUTS:N_MIXER_INPUTS + 2 * N_TOKEN_OUTPUTS]
    bstep = pl.program_id(0)

    @pl.when(bstep < n_batch_steps)
    def _():
        _mixer_compute(*refs[:N_MIXER_INPUTS], *refs[N_MIXER_INPUTS + N_TOKEN_OUTPUTS:], nb=nb, C=C)

    @pl.when((bstep == n_batch_steps) & (pl.program_id(1) < n_tail))
    def _():
        for tail_ref, out_ref in zip(tails, outs):
            out_ref[...] = tail_ref[...]


def _route(logits_t, run_ref):
    n_tok = logits_t.shape[1]
    expert = lax.broadcasted_iota(jnp.int32, logits_t.shape, 0)
    vals, idxs, hots = [], [], []
    for _ in range(TOP_K):
        m = jnp.max(logits_t, axis=0, keepdims=True)
        idx = jnp.min(jnp.where(logits_t == m, expert, N_EXPERTS), axis=0, keepdims=True)
        hot = expert == idx
        vals.append(m)
        idxs.append(idx)
        hots.append(hot)
        logits_t = jnp.where(hot, -jnp.inf, logits_t)
    ex = [jnp.exp(v - vals[0]) for v in vals]
    total = ex[0]
    for e in ex[1:]:
        total = total + e
    gates = jnp.concatenate([e / total for e in ex], axis=0)

    chosen = hots[0].astype(F32)
    for hot in hots[1:]:
        chosen = chosen + hot.astype(F32)
    chosen = chosen.astype(BF16)
    t_row = lax.broadcasted_iota(jnp.int32, (n_tok, n_tok), 0)
    t_col = lax.broadcasted_iota(jnp.int32, (n_tok, n_tok), 1)
    earlier = jnp.where(t_row < t_col, 1.0, 0.0).astype(BF16)
    before = run_ref[...] + jnp.dot(chosen, earlier, preferred_element_type=F32)
    ranks = [jnp.sum(jnp.where(hot, before, 0.0), axis=0, keepdims=True) for hot in hots]
    run_ref[...] = run_ref[...] + jnp.dot(chosen, jnp.ones((n_tok, n_tok), BF16), preferred_element_type=F32)
    return jnp.concatenate(idxs, axis=0), gates, jnp.concatenate(ranks, axis=0).astype(jnp.int32)


def _mixer_compute(x_ref, cos_ref, sin_ref, decay_ref, qdec_ref, kdec_ref, sdec_ref, win_ref, wout_ref,
                   convw_ref, convb_ref, bret_ref, bconv_ref, g1_ref, b1_ref, wr_ref, br_ref,
                   sret_ref, sconv_ref, h_ref, choice_ref, gate_ref, rank_ref, count_ref, sret_out, sconv_out,
                   mix_ref, run_ref, *, nb, C):
    c = pl.program_id(1)

    @pl.when((pl.program_id(0) == 0) & (c == 0))
    def _():
        run_ref[...] = jnp.zeros(run_ref.shape, F32)

    @pl.when(c == 0)
    def _():
        sret_out[...] = sret_ref[...]
        sconv_out[...] = sconv_ref[...]

    x = x_ref[...].reshape(nb * C, D_MODEL)
    xb = x.astype(BF16)

    def proj(col0, width):
        return jnp.dot(xb, win_ref[:, col0:col0 + width], preferred_element_type=F32)

    q = proj(0, D_RET)
    k = proj(D_RET, D_RET)
    v = proj(2 * D_RET, D_RET)
    g = proj(3 * D_RET, D_RET)
    bg = proj(4 * D_RET, D_CONV)
    cg = proj(4 * D_RET + D_CONV, D_CONV)
    hc = proj(4 * D_RET + 2 * D_CONV, D_CONV)

    cos = cos_ref[...]
    sin = sin_ref[...]
    row = lax.broadcasted_iota(jnp.int32, (C, D_CONV), 0)
    k_scale = HEAD_DIM ** -0.5

    for b in range(nb):
        r0 = b * C
        for h in range(N_RET_HEADS):
            c0 = h * HEAD_DIM
            qh = q[r0:r0 + C, c0:c0 + HEAD_DIM]
            kh = k[r0:r0 + C, c0:c0 + HEAD_DIM]
            vh = v[r0:r0 + C, c0:c0 + HEAD_DIM]
            qh = qh * cos + pltpu.roll(qh, HEAD_DIM // 2, axis=1) * sin
            kh = (kh * cos + pltpu.roll(kh, HEAD_DIM // 2, axis=1) * sin) * k_scale
            qb = qh.astype(BF16)
            kb = kh.astype(BF16)
            vb = vh.astype(BF16)
            s_old = sret_out[b, h]
            scores = lax.dot_general(qb, kb, (((1,), (1,)), ((), ())), preferred_element_type=F32)
            scores = scores * decay_ref[h]
            intra = jnp.dot(scores.astype(BF16), vb, preferred_element_type=F32)
            cross = jnp.dot(qb, s_old.astype(BF16), preferred_element_type=F32) * qdec_ref[h]
            o = intra + cross
            kd = (kh * kdec_ref[h]).astype(BF16)
            s_new = sdec_ref[h] * s_old + lax.dot_general(
                kd, vb, (((0,), (0,)), ((), ())), preferred_element_type=F32)
            sret_out[b, h] = s_new
            mu = jnp.mean(o, axis=-1, keepdims=True)
            oc = o - mu
            var = jnp.mean(oc * oc, axis=-1, keepdims=True)
            on = oc * lax.rsqrt(var + GN_EPS)
            gh = g[r0:r0 + C, c0:c0 + HEAD_DIM]
            ret = on * (gh * jax.nn.sigmoid(gh)) * bret_ref[:, c0:c0 + HEAD_DIM]
            mix_ref[r0:r0 + C, c0:c0 + HEAD_DIM] = ret.astype(BF16)

        u = cg[r0:r0 + C] * hc[r0:r0 + C]
        prev = sconv_out[b]
        u1 = jnp.where(row == 0, prev[1:2], pltpu.roll(u, 1, axis=0))
        u2 = jnp.where(row == 0, prev[0:1], jnp.where(row == 1, prev[1:2], pltpu.roll(u, 2, axis=0)))
        z = convb_ref[...] + convw_ref[0:1] * u2 + convw_ref[1:2] * u1 + convw_ref[2:3] * u
        conv_out = bg[r0:r0 + C] * z * bconv_ref[...]
        mix_ref[r0:r0 + C, D_RET:D_RET + D_CONV] = conv_out.astype(BF16)
        sconv_out[b] = u[C - 2:C]

    m = jnp.dot(mix_ref[...], wout_ref[...], preferred_element_type=F32)
    hval = _layer_norm(DEEPNORM_ALPHA * x + m, g1_ref[...], b1_ref[...])
    _store_rows(h_ref, hval, nb * C)
    h_hi = hval.astype(BF16)
    h_lo = (hval - h_hi.astype(F32)).astype(BF16)
    nt_dot = lambda a, b: lax.dot_general(a, b, (((1,), (1,)), ((), ())), preferred_element_type=F32)
    logits_t = nt_dot(wr_ref[0], h_hi) + nt_dot(wr_ref[0], h_lo) + nt_dot(wr_ref[1], h_hi) + br_ref[...]
    choice_ref[...], gate_ref[...], rank_ref[...] = _route(logits_t, run_ref)
    count_ref[...] = run_ref[:, :LANES]


def _retention_tables(C, pos0, L):
    lg = jnp.log1p(-jnp.power(2.0, -5.0 - jnp.arange(N_RET_HEADS, dtype=F32)))
    i = jnp.arange(C, dtype=F32)
    diff = i[:, None] - i[None, :]
    decay = jnp.where(diff[None] >= 0, jnp.exp(lg[:, None, None] * jnp.maximum(diff, 0.0)[None]), 0.0)
    qdec = jnp.exp(lg[:, None] * (i + 1.0)[None, :])
    kdec = jnp.exp(lg[:, None] * (C - 1.0 - i)[None, :])
    sdec = jnp.exp(lg * C)
    qdec = jnp.broadcast_to(qdec[:, :, None], (N_RET_HEADS, C, HEAD_DIM))
    kdec = jnp.broadcast_to(kdec[:, :, None], (N_RET_HEADS, C, HEAD_DIM))
    sdec = jnp.broadcast_to(sdec[:, None, None], (N_RET_HEADS, 1, HEAD_DIM))
    pos = pos0 + jnp.arange(L, dtype=jnp.int32)
    inv_freq = ROPE_BASE ** (-jnp.arange(0, HEAD_DIM, 2, dtype=F32) / HEAD_DIM)
    ang = pos.astype(F32)[:, None] * inv_freq[None, :]
    cos = jnp.cos(ang)
    sin = jnp.sin(ang)
    cos_full = jnp.concatenate([cos, cos], axis=-1)
    sin_signed = jnp.concatenate([-sin, sin], axis=-1)
    return cos_full, sin_signed, decay, qdec, kdec, sdec


def _mixer(x, s_ret, s_conv, pos0, nb, C, w, tail=None):
    B, L, _ = x.shape
    nc = L // C
    nbs = B // nb
    rows = nb * C
    n_tokens = B * L
    n_tail = 0
    if tail is not None:
        n_tokens += tail[1].shape[1]
        n_tail = tail[1].shape[1] // rows
        assert tail[1].shape[1] % rows == 0 and 0 < n_tail <= nc
    cos, sin, decay, qdec, kdec, sdec = _retention_tables(C, pos0, L)

    bb = lambda b: jnp.minimum(b, nbs - 1)
    cc = lambda b, c: jnp.where(b < nbs, c, nc - 1)
    out_blk = lambda b, c: jnp.where(b < nbs, b * nc + c, nbs * nc + jnp.minimum(c, n_tail - 1))
    const2 = lambda b, c: (0, 0)
    const3 = lambda b, c: (0, 0, 0)
    full = lambda a: pl.BlockSpec(a.shape, const2 if a.ndim == 2 else const3)
    per_token = lambda imap: pl.BlockSpec((TOP_K, rows), lambda b, c: (0, imap(b, c)))
    in_arrays = [x, cos, sin, decay, qdec, kdec, sdec, w["w_in"], w["w_out"], w["conv_w"], w["conv_b"],
                 w["beta_ret"], w["beta_conv"], w["ln1_g"], w["ln1_b"], w["w_router"], w["b_router"],
                 s_ret, s_conv]
    assert len(in_arrays) == N_MIXER_INPUTS
    in_specs = [pl.BlockSpec((nb, C, D_MODEL), lambda b, c: (bb(b), cc(b, c), 0)),
                pl.BlockSpec((C, HEAD_DIM), lambda b, c: (cc(b, c), 0)),
                pl.BlockSpec((C, HEAD_DIM), lambda b, c: (cc(b, c), 0))]
    in_specs += [full(a) for a in in_arrays[3:17]]
    in_specs += [pl.BlockSpec((nb, N_RET_HEADS, HEAD_DIM, HEAD_DIM), lambda b, c: (bb(b), 0, 0, 0)),
                 pl.BlockSpec((nb, CONV_W - 1, D_CONV), lambda b, c: (bb(b), 0, 0))]
    if n_tail:
        tail_blk = lambda b, c: jnp.where(b < nbs, 0, jnp.minimum(c, n_tail - 1))
        in_arrays += list(tail)
        in_specs += [pl.BlockSpec((rows * SUBLANES, LANES), lambda b, c: (tail_blk(b, c), 0))]
        in_specs += [per_token(tail_blk)] * (N_TOKEN_OUTPUTS - 1)
    out_shape = [jax.ShapeDtypeStruct((n_tokens * SUBLANES, LANES), F32),
                 jax.ShapeDtypeStruct((TOP_K, n_tokens), jnp.int32),
                 jax.ShapeDtypeStruct((TOP_K, n_tokens), F32),
                 jax.ShapeDtypeStruct((TOP_K, n_tokens), jnp.int32),
                 jax.ShapeDtypeStruct((N_EXPERTS, LANES), F32),
                 jax.ShapeDtypeStruct((B, N_RET_HEADS, HEAD_DIM, HEAD_DIM), F32),
                 jax.ShapeDtypeStruct((B, CONV_W - 1, D_CONV), F32)]
    assert len(out_shape) == N_TOKEN_OUTPUTS + 3
    out_specs = [pl.BlockSpec((rows * SUBLANES, LANES), lambda b, c: (out_blk(b, c), 0))]
    out_specs += [per_token(out_blk)] * (N_TOKEN_OUTPUTS - 1)
    out_specs += [pl.BlockSpec((N_EXPERTS, LANES), const2),
                  pl.BlockSpec((nb, N_RET_HEADS, HEAD_DIM, HEAD_DIM), lambda b, c: (bb(b), 0, 0, 0)),
                  pl.BlockSpec((nb, CONV_W - 1, D_CONV), lambda b, c: (bb(b), 0, 0))]
    return pl.pallas_call(
        functools.partial(_mixer_body, nb=nb, C=C, n_batch_steps=nbs, n_tail=n_tail),
        grid=(nbs + (1 if n_tail else 0), nc), in_specs=in_specs, out_specs=out_specs, out_shape=out_shape,
        scratch_shapes=[pltpu.VMEM((rows, D_MODEL), BF16), pltpu.VMEM((N_EXPERTS, rows), F32)],
        compiler_params=pltpu.CompilerParams(
            dimension_semantics=("arbitrary", "arbitrary"), vmem_limit_bytes=VMEM_LIMIT_BYTES),
        name="mixer",
    )(*in_arrays)


def _lookup(table, choices):
    experts = jnp.arange(N_EXPERTS, dtype=jnp.int32)
    return jnp.sum(jnp.where(choices[..., None] == experts, table, 0), axis=-1)


def _routing_plan(choices_t, ranks_t, counts_first, counts_second, n_first, blk):
    T = choices_t.shape[1]
    A = T * TOP_K
    assert A % blk == 0
    n_blocks = A // blk + N_EXPERTS
    experts = jnp.arange(N_EXPERTS, dtype=jnp.int32)
    counts = counts_first + counts_second
    nblk_e = (counts + blk - 1) // blk
    bend = jnp.cumsum(nblk_e)
    n_valid = bend[-1]
    start = (bend - nblk_e) * blk
    token = jnp.arange(T, dtype=jnp.int32)[None, :]
    rank = ranks_t + jnp.where(token >= n_first, _lookup(counts_first, choices_t), 0)
    pos = _lookup(start, choices_t) + rank

    bi = jnp.arange(n_blocks, dtype=jnp.int32)
    block_e = jnp.sum((bi[:, None] >= bend[None, :]).astype(jnp.int32), axis=1)
    block_e = jnp.where(bi < n_valid, block_e, block_e[n_valid - 1])
    pad_dst = (A + block_e[:, None] * blk + jnp.arange(blk, dtype=jnp.int32)[None, :]).reshape(n_blocks * blk)
    last_block = jnp.where(nblk_e > 0, bend - 1, -1)
    unused = n_valid + experts
    unused = jnp.where(unused < n_blocks, unused, -1)
    zero_blocks = jnp.concatenate([last_block, unused]).astype(jnp.int32)
    return pos, pad_dst, block_e, n_valid.reshape(1).astype(jnp.int32), zero_blocks


def _dispatch_body(zero_ref, pos_ref, h_ref, pad_dst_hbm, xs_hbm, dst_hbm, hbuf, dst_tab, sem, tab_sem,
                   *, tt, blk, n_tokens):
    i = pl.program_id(0)
    n = pl.num_programs(0)
    slot = i % 2

    def wait_rows(s):
        for _ in range(TOP_K):
            pltpu.make_async_copy(hbuf.at[s], xs_hbm.at[pl.ds(0, tt * SUBLANES)], sem.at[s]).wait()

    @pl.when(i == 0)
    def _():
        load_table = pltpu.make_async_copy(pad_dst_hbm, dst_tab, tab_sem)
        load_table.start()
        load_table.wait()
        hbuf[1] = jnp.zeros(hbuf.shape[1:], F32)
        zeros = hbuf.at[1, pl.ds(0, blk * SUBLANES)]
        for e in range(zero_ref.shape[0]):
            @pl.when(zero_ref[e] >= 0)
            def _():
                first_row = pl.multiple_of(zero_ref[e] * (blk * SUBLANES), blk * SUBLANES)
                pltpu.make_async_copy(zeros, xs_hbm.at[pl.ds(first_row, blk * SUBLANES)], sem.at[1]).start()
        for e in range(zero_ref.shape[0]):
            @pl.when(zero_ref[e] >= 0)
            def _():
                pltpu.make_async_copy(zeros, xs_hbm.at[pl.ds(0, blk * SUBLANES)], sem.at[1]).wait()

    @pl.when(i >= 2)
    def _():
        wait_rows(slot)

    hbuf[slot] = h_ref[...]
    first_token = i * tt
    for t in range(tt):
        row = hbuf.at[slot, _row_tile(t)]
        for kk in range(TOP_K):
            p = pos_ref[kk, t]
            dst_tab[p] = first_token + (kk * n_tokens + t)
            pltpu.make_async_copy(row, xs_hbm.at[_row_tile(p)], sem.at[slot]).start(kk % 2)

    @pl.when(i == n - 1)
    def _():
        wait_rows(slot)

        @pl.when(i >= 1)
        def _():
            wait_rows(1 - slot)

        store_table = pltpu.make_async_copy(dst_tab, dst_hbm, tab_sem)
        store_table.start()
        store_table.wait()


def _dispatch(h_all, pos, pad_dst, zero_blocks, tt, blk):
    T = pos.shape[1]
    n_slots = pad_dst.shape[0]
    n_tiles = T // tt
    assert T % tt == 0 and blk <= tt
    grid_spec = pltpu.PrefetchScalarGridSpec(
        num_scalar_prefetch=1,
        grid=(n_tiles,),
        in_specs=[pl.BlockSpec((TOP_K, tt), lambda i, zb: (0, i), memory_space=pltpu.SMEM),
                  pl.BlockSpec((tt * SUBLANES, LANES), lambda i, zb: (i, 0)),
                  pl.BlockSpec(memory_space=pl.ANY)],
        out_specs=[pl.BlockSpec(memory_space=pl.ANY), pl.BlockSpec(memory_space=pl.ANY)],
        scratch_shapes=[pltpu.VMEM((2, tt * SUBLANES, LANES), F32),
                        pltpu.SMEM((n_slots,), jnp.int32),
                        pltpu.SemaphoreType.DMA((2,)),
                        pltpu.SemaphoreType.DMA],
    )
    return pl.pallas_call(
        functools.partial(_dispatch_body, tt=tt, blk=blk, n_tokens=T),
        grid_spec=grid_spec,
        out_shape=[jax.ShapeDtypeStruct((n_slots * SUBLANES, LANES), F32),
                   jax.ShapeDtypeStruct((n_slots,), jnp.int32)],
        compiler_params=pltpu.CompilerParams(
            dimension_semantics=("arbitrary",), vmem_limit_bytes=VMEM_LIMIT_BYTES),
        name="dispatch",
    )(zero_blocks, pos, h_all, pad_dst)


def _expert_body(be_ref, nv_ref, dst_ref, x_ref, wgu_ref, bgu_ref, wd_ref, bd_ref,
                 y_hbm, ybuf, osem, wgu_bf, wd_bf, xb_ref, act_ref, *, blk, n_real_rows):
    i = pl.program_id(0)
    n_valid = nv_ref[0]

    def scatter_row(s, r, prio=0):
        pltpu.make_async_copy(
            ybuf.at[s, _row_tile(r)], y_hbm.at[_row_tile(dst_ref[0, 0, r])], osem.at[s]).start(prio)

    def wait_scatter(s):
        pltpu.make_async_copy(ybuf.at[s], y_hbm.at[pl.ds(0, blk * SUBLANES)], osem.at[s]).wait()

    @pl.when(i == 0)
    def _():
        ybuf[...] = jnp.zeros(ybuf.shape, F32)
        for e in range(N_EXPERTS):
            pltpu.make_async_copy(
                ybuf.at[0], y_hbm.at[pl.ds((n_real_rows + e * blk) * SUBLANES, blk * SUBLANES)],
                osem.at[0]).start()
        for e in range(N_EXPERTS):
            wait_scatter(0)

    @pl.when((i < n_valid) & ((i == 0) | (be_ref[i] != be_ref[jnp.maximum(i - 1, 0)])))
    def _():
        wgu_bf[...] = wgu_ref[0].astype(BF16)
        wd_bf[...] = wd_ref[0].astype(BF16)

    rows_per_chunk = blk // N_CHUNKS
    cols = D_FF // N_CHUNKS

    def block_step(s):
        @pl.when(i < n_valid)
        def _():
            @pl.when(i >= 1)
            def _():
                wait_scatter(s)

            xb_ref[...] = _load_rows(x_ref, blk).astype(BF16)
            for c in range(N_CHUNKS):
                lo = c * cols
                xb = xb_ref[...]
                gate = jnp.dot(xb, wgu_bf[:, lo:lo + cols], preferred_element_type=F32)
                up = jnp.dot(xb, wgu_bf[:, D_FF + lo:D_FF + lo + cols], preferred_element_type=F32)
                gate = jnp.minimum(gate + bgu_ref[0, :, lo:lo + cols], SWIGLU_LIMIT)
                up = jnp.clip(up + bgu_ref[0, :, D_FF + lo:D_FF + lo + cols], -SWIGLU_LIMIT, SWIGLU_LIMIT)
                act = (up + 1.0) * gate * jax.nn.sigmoid(SWIGLU_ALPHA * gate)
                act_ref[:, lo:lo + cols] = act.astype(BF16)
                for r in range(c * rows_per_chunk, (c + 1) * rows_per_chunk):
                    scatter_row(1 - s, r, 1)
            y = jnp.dot(act_ref[...], wd_bf[...], preferred_element_type=F32) + bd_ref[0]
            _store_rows(ybuf.at[s], y, blk)

        @pl.when(i == n_valid)
        def _():
            wait_scatter(s)

            def body(gi, carry):
                for u in range(DMA_UNROLL):
                    scatter_row(1 - s, gi * DMA_UNROLL + u)
                return carry
            lax.fori_loop(0, blk // DMA_UNROLL, body, 0)
            wait_scatter(1 - s)

    for s in range(2):
        pl.when(i % 2 == s)(functools.partial(block_step, s))


def _experts(x_slots, dst, block_e, n_valid, w_gu, b_gu, w_down, b_down, blk, n_real_rows):
    n_blocks = block_e.shape[0]
    assert n_blocks * blk == n_real_rows + N_EXPERTS * blk and blk % (2 * N_CHUNKS) == 0
    n_out_rows = n_real_rows + (N_EXPERTS + 1) * blk
    priming = n_real_rows + N_EXPERTS * blk + jnp.arange(blk, dtype=jnp.int32)
    dst3 = jnp.concatenate([priming, dst]).reshape(n_blocks + 1, 1, blk)
    used = lambda i, nv: jnp.minimum(i, nv[0] - 1)
    grid_spec = pltpu.PrefetchScalarGridSpec(
        num_scalar_prefetch=2,
        grid=(n_blocks,),
        in_specs=[
            pl.BlockSpec((1, 1, blk), lambda i, be, nv: (i, 0, 0), memory_space=pltpu.SMEM),
            pl.BlockSpec((blk * SUBLANES, LANES), lambda i, be, nv: (used(i, nv), 0)),
            pl.BlockSpec((1, D_MODEL, 2 * D_FF), lambda i, be, nv: (be[i], 0, 0)),
            pl.BlockSpec((1, 1, 2 * D_FF), lambda i, be, nv: (be[i], 0, 0)),
            pl.BlockSpec((1, D_FF, D_MODEL), lambda i, be, nv: (be[i], 0, 0)),
            pl.BlockSpec((1, 1, D_MODEL), lambda i, be, nv: (be[i], 0, 0)),
        ],
        out_specs=pl.BlockSpec(memory_space=pl.ANY),
        scratch_shapes=[pltpu.VMEM((2, blk * SUBLANES, LANES), F32),
                        pltpu.SemaphoreType.DMA((2,)),
                        pltpu.VMEM((D_MODEL, 2 * D_FF), BF16),
                        pltpu.VMEM((D_FF, D_MODEL), BF16),
                        pltpu.VMEM((blk, D_MODEL), BF16),
                        pltpu.VMEM((blk, D_FF), BF16)],
    )
    return pl.pallas_call(
        functools.partial(_expert_body, blk=blk, n_real_rows=n_real_rows),
        grid_spec=grid_spec,
        out_shape=jax.ShapeDtypeStruct((n_out_rows * SUBLANES, LANES), F32),
        compiler_params=pltpu.CompilerParams(
            dimension_semantics=("arbitrary",), vmem_limit_bytes=VMEM_LIMIT_BYTES),
        name="experts",
    )(block_e, n_valid, dst3, x_slots, w_gu, b_gu.reshape(N_EXPERTS, 1, 2 * D_FF),
      w_down, b_down.reshape(N_EXPERTS, 1, D_MODEL))


def _combine_body(*refs, tt, n_first):
    y_refs = refs[:TOP_K]
    h_ref, gate_ref, g2_ref, b2_ref, out_first, out_second = refs[TOP_K:]
    i = pl.program_id(0)
    gates = gate_ref[...]
    f = gates[:, 0:1] * _load_rows(y_refs[0], tt)
    for kk in range(1, TOP_K):
        f = f + gates[:, kk:kk + 1] * _load_rows(y_refs[kk], tt)
    out = _layer_norm(DEEPNORM_ALPHA * _load_rows(h_ref, tt) + f, g2_ref[...], b2_ref[...])

    @pl.when(i < n_first)
    def _():
        out_first[...] = out

    @pl.when(i >= n_first)
    def _():
        out_second[...] = out


def _combine(y_rows, gates, h_all, ln2_g, ln2_b, tt, n_tokens_first):
    T = gates.shape[0]
    n_tiles = T // tt
    n_first = n_tokens_first // tt
    assert T % tt == 0 and n_tokens_first % tt == 0 and 0 < n_first < n_tiles
    row_blk = lambda imap: pl.BlockSpec((tt * SUBLANES, LANES), imap)
    y_specs = [row_blk(functools.partial(lambda i, kk: (kk * n_tiles + i, 0), kk=kk)) for kk in range(TOP_K)]
    return pl.pallas_call(
        functools.partial(_combine_body, tt=tt, n_first=n_first),
        grid=(n_tiles,),
        in_specs=y_specs + [
            row_blk(lambda i: (i, 0)),
            pl.BlockSpec((tt, TOP_K), lambda i: (i, 0)),
            pl.BlockSpec((1, D_MODEL), lambda i: (0, 0)),
            pl.BlockSpec((1, D_MODEL), lambda i: (0, 0)),
        ],
        out_specs=[pl.BlockSpec((tt, D_MODEL), lambda i: (jnp.minimum(i, n_first - 1), 0)),
                   pl.BlockSpec((tt, D_MODEL), lambda i: (jnp.maximum(i - n_first, 0), 0))],
        out_shape=[jax.ShapeDtypeStruct((n_tokens_first, D_MODEL), F32),
                   jax.ShapeDtypeStruct((T - n_tokens_first, D_MODEL), F32)],
        compiler_params=pltpu.CompilerParams(
            dimension_semantics=("arbitrary",), vmem_limit_bytes=VMEM_LIMIT_BYTES),
        name="combine",
    )(*([y_rows] * TOP_K), h_all, gates, ln2_g, ln2_b)


def kernel(x_prompt, x_sample, state_retention, state_conv, w_in, beta_ret, beta_conv, conv_w, conv_b,
           w_out, ln1_g, ln1_b, w_router, b_router, w_gu, b_gu, w_down, b_down, ln2_g, ln2_b):
    assert w_in.shape[0] == DEPTH == 1
    B, S, _ = x_prompt.shape
    Bd, Ld, _ = x_sample.shape
    Tp, Ts = B * S, Bd * Ld
    T = Tp + Ts
    wr_t = w_router[0].T
    wr_hi = wr_t.astype(BF16)
    wr_lo = (wr_t - wr_hi.astype(F32)).astype(BF16)
    w = dict(w_in=w_in[0].astype(BF16), w_out=w_out[0].astype(BF16), conv_w=conv_w[0],
             conv_b=conv_b[0][None], beta_ret=beta_ret[0][None], beta_conv=beta_conv[0][None],
             ln1_g=ln1_g[0][None], ln1_b=ln1_b[0][None], w_router=jnp.stack([wr_hi, wr_lo]),
             b_router=b_router[0][:, None])

    s_ret0 = jnp.zeros((B, N_RET_HEADS, HEAD_DIM, HEAD_DIM), F32)
    s_conv0 = jnp.zeros((B, CONV_W - 1, D_CONV), F32)
    h_s, choices_s, gates_s, ranks_s, counts_s, sret_s, sconv_s = _mixer(
        x_sample, state_retention[0], state_conv[0], PAST_LEN, SAMPLE_BATCH_BLOCK, Ld, w)
    h_all, choices_t, gates_t, ranks_t, counts_p, sret_p, sconv_p = _mixer(
        x_prompt, s_ret0, s_conv0, 0, 1, PROMPT_CHUNK, w, tail=(h_s, choices_s, gates_s, ranks_s))

    pos, pad_dst, block_e, n_valid, zero_blocks = _routing_plan(
        choices_t, ranks_t, counts_p[:, 0].astype(jnp.int32), counts_s[:, 0].astype(jnp.int32), Tp,
        EXPERT_BLOCK)
    x_slots, dst = _dispatch(h_all, pos, pad_dst, zero_blocks, DISPATCH_TILE, EXPERT_BLOCK)
    y_rows = _experts(x_slots, dst, block_e, n_valid, w_gu[0], b_gu[0], w_down[0], b_down[0],
                      EXPERT_BLOCK, T * TOP_K)
    y_p, y_s = _combine(y_rows, gates_t.T, h_all, ln2_g[0][None], ln2_b[0][None], COMBINE_TILE, Tp)

    return (y_p.reshape(B, S, D_MODEL), y_s.reshape(Bd, Ld, D_MODEL),
            sret_p[None], sconv_p[None], sret_s[None], sconv_s[None])
```

```python
import functools

import jax
import jax.numpy as jnp
from jax import lax
from jax.experimental import pallas as pl
from jax.experimental.pallas import tpu as pltpu

D_MODEL = 1024
N_RET_HEADS = 4
HEAD_DIM = 128
D_RET = N_RET_HEADS * HEAD_DIM
D_CONV = D_MODEL - D_RET
CONV_W = 3
N_EXPERTS = 32
TOP_K = 4
D_FF = D_MODEL
SWIGLU_LIMIT = 7.0
SWIGLU_ALPHA = 1.702
ROPE_BASE = 10000.0
LN_EPS = 1e-5
GN_EPS = 1e-5
DEPTH = 1
PAST_LEN = 1024
DEEPNORM_ALPHA = (2.0 * DEPTH) ** 0.25

LANES = 128
SUBLANES = 8
ROW_CHUNKS = D_MODEL // LANES
assert ROW_CHUNKS == SUBLANES

PROMPT_CHUNK = 512
SAMPLE_BATCH_BLOCK = 8
EXPERT_BLOCK = 512
DISPATCH_TILE = 512
COMBINE_TILE = 512
DMA_UNROLL = 8
N_CHUNKS = 4
VMEM_LIMIT_BYTES = 56 * 1024 * 1024

F32 = jnp.float32
BF16 = jnp.bfloat16


def _layer_norm(x, g, b):
    mu = jnp.mean(x, axis=-1, keepdims=True)
    xc = x - mu
    var = jnp.mean(xc * xc, axis=-1, keepdims=True)
    return xc * lax.rsqrt(var + LN_EPS) * g + b


def _load_rows(ref, n_rows):
    return jnp.concatenate(
        [ref[pl.ds(j, n_rows, stride=SUBLANES), :] for j in range(ROW_CHUNKS)], axis=1)


def _store_rows(ref, val, n_rows):
    for j in range(ROW_CHUNKS):
        ref[pl.ds(j, n_rows, stride=SUBLANES), :] = val[:, j * LANES:(j + 1) * LANES]


def _row_tile(r):
    if isinstance(r, int):
        return pl.ds(r * SUBLANES, SUBLANES)
    return pl.ds(pl.multiple_of(r * SUBLANES, SUBLANES), SUBLANES)


N_MIXER_INPUTS = 19
N_TOKEN_OUTPUTS = 4


def _mixer_body(*refs, nb, C, n_batch_steps, n_tail):
    if not n_tail:
        _mixer_compute(*refs, nb=nb, C=C)
        return
    tails = refs[N_MIXER_INPUTS:N_MIXER_INPUTS + N_TOKEN_OUTPUTS]
    outs = refs[N_MIXER_INPUTS + N_TOKEN_OUTPUTS:N_MIXER_INPUTS + 2 * N_TOKEN_OUTPUTS]
    bstep = pl.program_id(0)

    @pl.when(bstep < n_batch_steps)
    def _():
        _mixer_compute(*refs[:N_MIXER_INPUTS], *refs[N_MIXER_INPUTS + N_TOKEN_OUTPUTS:], nb=nb, C=C)

    @pl.when((bstep == n_batch_steps) & (pl.program_id(1) < n_tail))
    def _():
        for tail_ref, out_ref in zip(tails, outs):
            out_ref[...] = tail_ref[...]


def _route(logits_t, run_ref):
    n_tok = logits_t.shape[1]
    expert = lax.broadcasted_iota(jnp.int32, logits_t.shape, 0)
    vals, idxs, hots = [], [], []
    for _ in range(TOP_K):
        m = jnp.max(logits_t, axis=0, keepdims=True)
        idx = jnp.min(jnp.where(logits_t == m, expert, N_EXPERTS), axis=0, keepdims=True)
        hot = expert == idx
        vals.append(m)
        idxs.append(idx)
        hots.append(hot)
        logits_t = jnp.where(hot, -jnp.inf, logits_t)
    ex = [jnp.exp(v - vals[0]) for v in vals]
    total = ex[0]
    for e in ex[1:]:
        total = total + e
    gates = jnp.concatenate([e / total for e in ex], axis=0)

    chosen = hots[0].astype(F32)
    for hot in hots[1:]:
        chosen = chosen + hot.astype(F32)
    chosen = chosen.astype(BF16)
    t_row = lax.broadcasted_iota(jnp.int32, (n_tok, n_tok), 0)
    t_col = lax.broadcasted_iota(jnp.int32, (n_tok, n_tok), 1)
    earlier = jnp.where(t_row < t_col, 1.0, 0.0).astype(BF16)
    before = run_ref[...] + jnp.dot(chosen, earlier, preferred_element_type=F32)
    ranks = [jnp.sum(jnp.where(hot, before, 0.0), axis=0, keepdims=True) for hot in hots]
    run_ref[...] = run_ref[...] + jnp.dot(chosen, jnp.ones((n_tok, n_tok), BF16), preferred_element_type=F32)
    return jnp.concatenate(idxs, axis=0), gates, jnp.concatenate(ranks, axis=0).astype(jnp.int32)


def _mixer_compute(x_ref, cos_ref, sin_ref, decay_ref, qdec_ref, kdec_ref, sdec_ref, win_ref, wout_ref,
                   convw_ref, convb_ref, bret_ref, bconv_ref, g1_ref, b1_ref, wr_ref, br_ref,
                   sret_ref, sconv_ref, h_ref, choice_ref, gate_ref, rank_ref, count_ref, sret_out, sconv_out,
                   mix_ref, run_ref, *, nb, C):
    c = pl.program_id(1)

    @pl.when((pl.program_id(0) == 0) & (c == 0))
    def _():
        run_ref[...] = jnp.zeros(run_ref.shape, F32)

    @pl.when(c == 0)
    def _():
        sret_out[...] = sret_ref[...]
        sconv_out[...] = sconv_ref[...]

    x = x_ref[...].reshape(nb * C, D_MODEL)
    xb = x.astype(BF16)

    def proj(col0, width):
        return jnp.dot(xb, win_ref[:, col0:col0 + width], preferred_element_type=F32)

    q = proj(0, D_RET)
    k = proj(D_RET, D_RET)
    v = proj(2 * D_RET, D_RET)
    g = proj(3 * D_RET, D_RET)
    bg = proj(4 * D_RET, D_CONV)
    cg = proj(4 * D_RET + D_CONV, D_CONV)
    hc = proj(4 * D_RET + 2 * D_CONV, D_CONV)

    cos = cos_ref[...]
    sin = sin_ref[...]
    row = lax.broadcasted_iota(jnp.int32, (C, D_CONV), 0)
    k_scale = HEAD_DIM ** -0.5

    for b in range(nb):
        r0 = b * C
        for h in range(N_RET_HEADS):
            c0 = h * HEAD_DIM
            qh = q[r0:r0 + C, c0:c0 + HEAD_DIM]
            kh = k[r0:r0 + C, c0:c0 + HEAD_DIM]
            vh = v[r0:r0 + C, c0:c0 + HEAD_DIM]
            qh = qh * cos + pltpu.roll(qh, HEAD_DIM // 2, axis=1) * sin
            kh = (kh * cos + pltpu.roll(kh, HEAD_DIM // 2, axis=1) * sin) * k_scale
            qb = qh.astype(BF16)
            kb = kh.astype(BF16)
            vb = vh.astype(BF16)
            s_old = sret_out[b, h]
            scores = lax.dot_general(qb, kb, (((1,), (1,)), ((), ())), preferred_element_type=F32)
            scores = scores * decay_ref[h]
            intra = jnp.dot(scores.astype(BF16), vb, preferred_element_type=F32)
            cross = jnp.dot(qb, s_old.astype(BF16), preferred_element_type=F32) * qdec_ref[h]
            o = intra + cross
            kd = (kh * kdec_ref[h]).astype(BF16)
            s_new = sdec_ref[h] * s_old + lax.dot_general(
                kd, vb, (((0,), (0,)), ((), ())), preferred_element_type=F32)
            sret_out[b, h] = s_new
            mu = jnp.mean(o, axis=-1, keepdims=True)
            oc = o - mu
            var = jnp.mean(oc * oc, axis=-1, keepdims=True)
            on = oc * lax.rsqrt(var + GN_EPS)
            gh = g[r0:r0 + C, c0:c0 + HEAD_DIM]
            ret = on * (gh * jax.nn.sigmoid(gh)) * bret_ref[:, c0:c0 + HEAD_DIM]
            mix_ref[r0:r0 + C, c0:c0 + HEAD_DIM] = ret.astype(BF16)

        u = cg[r0:r0 + C] * hc[r0:r0 + C]
        prev = sconv_out[b]
        u1 = jnp.where(row == 0, prev[1:2], pltpu.roll(u, 1, axis=0))
        u2 = jnp.where(row == 0, prev[0:1], jnp.where(row == 1, prev[1:2], pltpu.roll(u, 2, axis=0)))
        z = convb_ref[...] + convw_ref[0:1] * u2 + convw_ref[1:2] * u1 + convw_ref[2:3] * u
        conv_out = bg[r0:r0 + C] * z * bconv_ref[...]
        mix_ref[r0:r0 + C, D_RET:D_RET + D_CONV] = conv_out.astype(BF16)
        sconv_out[b] = u[C - 2:C]

    m = jnp.dot(mix_ref[...], wout_ref[...], preferred_element_type=F32)
    hval = _layer_norm(DEEPNORM_ALPHA * x + m, g1_ref[...], b1_ref[...])
    _store_rows(h_ref, hval, nb * C)
    h_hi = hval.astype(BF16)
    h_lo = (hval - h_hi.astype(F32)).astype(BF16)
    nt_dot = lambda a, b: lax.dot_general(a, b, (((1,), (1,)), ((), ())), preferred_element_type=F32)
    logits_t = nt_dot(wr_ref[0], h_hi) + nt_dot(wr_ref[0], h_lo) + nt_dot(wr_ref[1], h_hi) + br_ref[...]
    choice_ref[...], gate_ref[...], rank_ref[...] = _route(logits_t, run_ref)
    count_ref[...] = run_ref[:, :LANES]


def _retention_tables(C, pos0, L):
    lg = jnp.log1p(-jnp.power(2.0, -5.0 - jnp.arange(N_RET_HEADS, dtype=F32)))
    i = jnp.arange(C, dtype=F32)
    diff = i[:, None] - i[None, :]
    decay = jnp.where(diff[None] >= 0, jnp.exp(lg[:, None, None] * jnp.maximum(diff, 0.0)[None]), 0.0)
    qdec = jnp.exp(lg[:, None] * (i + 1.0)[None, :])
    kdec = jnp.exp(lg[:, None] * (C - 1.0 - i)[None, :])
    sdec = jnp.exp(lg * C)
    qdec = jnp.broadcast_to(qdec[:, :, None], (N_RET_HEADS, C, HEAD_DIM))
    kdec = jnp.broadcast_to(kdec[:, :, None], (N_RET_HEADS, C, HEAD_DIM))
    sdec = jnp.broadcast_to(sdec[:, None, None], (N_RET_HEADS, 1, HEAD_DIM))
    pos = pos0 + jnp.arange(L, dtype=jnp.int32)
    inv_freq = ROPE_BASE ** (-jnp.arange(0, HEAD_DIM, 2, dtype=F32) / HEAD_DIM)
    ang = pos.astype(F32)[:, None] * inv_freq[None, :]
    cos = jnp.cos(ang)
    sin = jnp.sin(ang)
    cos_full = jnp.concatenate([cos, cos], axis=-1)
    sin_signed = jnp.concatenate([-sin, sin], axis=-1)
    return cos_full, sin_signed, decay, qdec, kdec, sdec


def _mixer(x, s_ret, s_conv, pos0, nb, C, w, tail=None):
    B, L, _ = x.shape
    nc = L // C
    nbs = B // nb
    rows = nb * C
    n_tokens = B * L
    n_tail = 0
    if tail is not None:
        n_tokens += tail[1].shape[1]
        n_tail = tail[1].shape[1] // rows
        assert tail[1].shape[1] % rows == 0 and 0 < n_tail <= nc
    cos, sin, decay, qdec, kdec, sdec = _retention_tables(C, pos0, L)

    bb = lambda b: jnp.minimum(b, nbs - 1)
    cc = lambda b, c: jnp.where(b < nbs, c, nc - 1)
    out_blk = lambda b, c: jnp.where(b < nbs, b * nc + c, nbs * nc + jnp.minimum(c, n_tail - 1))
    const2 = lambda b, c: (0, 0)
    const3 = lambda b, c: (0, 0, 0)
    full = lambda a: pl.BlockSpec(a.shape, const2 if a.ndim == 2 else const3)
    per_token = lambda imap: pl.BlockSpec((TOP_K, rows), lambda b, c: (0, imap(b, c)))
    in_arrays = [x, cos, sin, decay, qdec, kdec, sdec, w["w_in"], w["w_out"], w["conv_w"], w["conv_b"],
                 w["beta_ret"], w["beta_conv"], w["ln1_g"], w["ln1_b"], w["w_router"], w["b_router"],
                 s_ret, s_conv]
    assert len(in_arrays) == N_MIXER_INPUTS
    in_specs = [pl.BlockSpec((nb, C, D_MODEL), lambda b, c: (bb(b), cc(b, c), 0)),
                pl.BlockSpec((C, HEAD_DIM), lambda b, c: (cc(b, c), 0)),
                pl.BlockSpec((C, HEAD_DIM), lambda b, c: (cc(b, c), 0))]
    in_specs += [full(a) for a in in_arrays[3:17]]
    in_specs += [pl.BlockSpec((nb, N_RET_HEADS, HEAD_DIM, HEAD_DIM), lambda b, c: (bb(b), 0, 0, 0)),
                 pl.BlockSpec((nb, CONV_W - 1, D_CONV), lambda b, c: (bb(b), 0, 0))]
    if n_tail:
        tail_blk = lambda b, c: jnp.where(b < nbs, 0, jnp.minimum(c, n_tail - 1))
        in_arrays += list(tail)
        in_specs += [pl.BlockSpec((rows * SUBLANES, LANES), lambda b, c: (tail_blk(b, c), 0))]
        in_specs += [per_token(tail_blk)] * (N_TOKEN_OUTPUTS - 1)
    out_shape = [jax.ShapeDtypeStruct((n_tokens * SUBLANES, LANES), F32),
                 jax.ShapeDtypeStruct((TOP_K, n_tokens), jnp.int32),
                 jax.ShapeDtypeStruct((TOP_K, n_tokens), F32),
                 jax.ShapeDtypeStruct((TOP_K, n_tokens), jnp.int32),
                 jax.ShapeDtypeStruct((N_EXPERTS, LANES), F32),
                 jax.ShapeDtypeStruct((B, N_RET_HEADS, HEAD_DIM, HEAD_DIM), F32),
                 jax.ShapeDtypeStruct((B, CONV_W - 1, D_CONV), F32)]
    assert len(out_shape) == N_TOKEN_OUTPUTS + 3
    out_specs = [pl.BlockSpec((rows * SUBLANES, LANES), lambda b, c: (out_blk(b, c), 0))]
    out_specs += [per_token(out_blk)] * (N_TOKEN_OUTPUTS - 1)
    out_specs += [pl.BlockSpec((N_EXPERTS, LANES), const2),
                  pl.BlockSpec((nb, N_RET_HEADS, HEAD_DIM, HEAD_DIM), lambda b, c: (bb(b), 0, 0, 0)),
                  pl.BlockSpec((nb, CONV_W - 1, D_CONV), lambda b, c: (bb(b), 0, 0))]
    return pl.pallas_call(
        functools.partial(_mixer_body, nb=nb, C=C, n_batch_steps=nbs, n_tail=n_tail),
        grid=(nbs + (1 if n_tail else 0), nc), in_specs=in_specs, out_specs=out_specs, out_shape=out_shape,
        scratch_shapes=[pltpu.VMEM((rows, D_MODEL), BF16), pltpu.VMEM((N_EXPERTS, rows), F32)],
        compiler_params=pltpu.CompilerParams(
            dimension_semantics=("arbitrary", "arbitrary"), vmem_limit_bytes=VMEM_LIMIT_BYTES),
        name="mixer",
    )(*in_arrays)


def _lookup(table, choices):
    experts = jnp.arange(N_EXPERTS, dtype=jnp.int32)
    return jnp.sum(jnp.where(choices[..., None] == experts, table, 0), axis=-1)


def _routing_plan(choices_t, ranks_t, counts_first, counts_second, n_first, blk):
    T = choices_t.shape[1]
    A = T * TOP_K
    assert A % blk == 0
    n_blocks = A // blk + N_EXPERTS
    experts = jnp.arange(N_EXPERTS, dtype=jnp.int32)
    counts = counts_first + counts_second
    nblk_e = (counts + blk - 1) // blk
    bend = jnp.cumsum(nblk_e)
    n_valid = bend[-1]
    start = (bend - nblk_e) * blk
    token = jnp.arange(T, dtype=jnp.int32)[None, :]
    rank = ranks_t + jnp.where(token >= n_first, _lookup(counts_first, choices_t), 0)
    pos = _lookup(start, choices_t) + rank

    bi = jnp.arange(n_blocks, dtype=jnp.int32)
    block_e = jnp.sum((bi[:, None] >= bend[None, :]).astype(jnp.int32), axis=1)
    block_e = jnp.where(bi < n_valid, block_e, block_e[n_valid - 1])
    pad_dst = (A + block_e[:, None] * blk + jnp.arange(blk, dtype=jnp.int32)[None, :]).reshape(n_blocks * blk)
    last_block = jnp.where(nblk_e > 0, bend - 1, -1)
    unused = n_valid + experts
    unused = jnp.where(unused < n_blocks, unused, -1)
    zero_blocks = jnp.concatenate([last_block, unused]).astype(jnp.int32)
    return pos, pad_dst, block_e, n_valid.reshape(1).astype(jnp.int32), zero_blocks


def _dispatch_body(zero_ref, pos_ref, h_ref, pad_dst_hbm, xs_hbm, dst_hbm, hbuf, dst_tab, sem, tab_sem,
                   *, tt, blk, n_tokens):
    i = pl.program_id(0)
    n = pl.num_programs(0)
    slot = i % 2

    def wait_rows(s):
        for _ in range(TOP_K):
            pltpu.make_async_copy(hbuf.at[s], xs_hbm.at[pl.ds(0, tt * SUBLANES)], sem.at[s]).wait()

    @pl.when(i == 0)
    def _():
        load_table = pltpu.make_async_copy(pad_dst_hbm, dst_tab, tab_sem)
        load_table.start()
        load_table.wait()
        hbuf[1] = jnp.zeros(hbuf.shape[1:], F32)
        zeros = hbuf.at[1, pl.ds(0, blk * SUBLANES)]
        for e in range(zero_ref.shape[0]):
            @pl.when(zero_ref[e] >= 0)
            def _():
                first_row = pl.multiple_of(zero_ref[e] * (blk * SUBLANES), blk * SUBLANES)
                pltpu.make_async_copy(zeros, xs_hbm.at[pl.ds(first_row, blk * SUBLANES)], sem.at[1]).start()
        for e in range(zero_ref.shape[0]):
            @pl.when(zero_ref[e] >= 0)
            def _():
                pltpu.make_async_copy(zeros, xs_hbm.at[pl.ds(0, blk * SUBLANES)], sem.at[1]).wait()

    @pl.when(i >= 2)
    def _():
        wait_rows(slot)

    hbuf[slot] = h_ref[...]
    first_token = i * tt
    for t in range(tt):
        row = hbuf.at[slot, _row_tile(t)]
        for kk in range(TOP_K):
            p = pos_ref[kk, t]
            dst_tab[p] = first_token + (kk * n_tokens + t)
            pltpu.make_async_copy(row, xs_hbm.at[_row_tile(p)], sem.at[slot]).start(kk % 2)

    @pl.when(i == n - 1)
    def _():
        wait_rows(slot)

        @pl.when(i >= 1)
        def _():
            wait_rows(1 - slot)

        store_table = pltpu.make_async_copy(dst_tab, dst_hbm, tab_sem)
        store_table.start()
        store_table.wait()


def _dispatch(h_all, pos, pad_dst, zero_blocks, tt, blk):
    T = pos.shape[1]
    n_slots = pad_dst.shape[0]
    n_tiles = T // tt
    assert T % tt == 0 and blk <= tt
    grid_spec = pltpu.PrefetchScalarGridSpec(
        num_scalar_prefetch=1,
        grid=(n_tiles,),
        in_specs=[pl.BlockSpec((TOP_K, tt), lambda i, zb: (0, i), memory_space=pltpu.SMEM),
                  pl.BlockSpec((tt * SUBLANES, LANES), lambda i, zb: (i, 0)),
                  pl.BlockSpec(memory_space=pl.ANY)],
        out_specs=[pl.BlockSpec(memory_space=pl.ANY), pl.BlockSpec(memory_space=pl.ANY)],
        scratch_shapes=[pltpu.VMEM((2, tt * SUBLANES, LANES), F32),
                        pltpu.SMEM((n_slots,), jnp.int32),
                        pltpu.SemaphoreType.DMA((2,)),
                        pltpu.SemaphoreType.DMA],
    )
    return pl.pallas_call(
        functools.partial(_dispatch_body, tt=tt, blk=blk, n_tokens=T),
        grid_spec=grid_spec,
        out_shape=[jax.ShapeDtypeStruct((n_slots * SUBLANES, LANES), F32),
                   jax.ShapeDtypeStruct((n_slots,), jnp.int32)],
        compiler_params=pltpu.CompilerParams(
            dimension_semantics=("arbitrary",), vmem_limit_bytes=VMEM_LIMIT_BYTES),
        name="dispatch",
    )(zero_blocks, pos, h_all, pad_dst)


def _expert_body(be_ref, nv_ref, dst_ref, x_ref, wgu_ref, bgu_ref, wd_ref, bd_ref,
                 y_hbm, ybuf, osem, wgu_bf, wd_bf, xb_ref, act_ref, *, blk, n_real_rows):
    i = pl.program_id(0)
    n_valid = nv_ref[0]

    def scatter_row(s, r, prio=0):
        pltpu.make_async_copy(
            ybuf.at[s, _row_tile(r)], y_hbm.at[_row_tile(dst_ref[0, 0, r])], osem.at[s]).start(prio)

    def wait_scatter(s):
        pltpu.make_async_copy(ybuf.at[s], y_hbm.at[pl.ds(0, blk * SUBLANES)], osem.at[s]).wait()

    @pl.when(i == 0)
    def _():
        ybuf[...] = jnp.zeros(ybuf.shape, F32)
        for e in range(N_EXPERTS):
            pltpu.make_async_copy(
                ybuf.at[0], y_hbm.at[pl.ds((n_real_rows + e * blk) * SUBLANES, blk * SUBLANES)],
                osem.at[0]).start()
        for e in range(N_EXPERTS):
            wait_scatter(0)

    @pl.when((i < n_valid) & ((i == 0) | (be_ref[i] != be_ref[jnp.maximum(i - 1, 0)])))
    def _():
        wgu_bf[...] = wgu_ref[0].astype(BF16)
        wd_bf[...] = wd_ref[0].astype(BF16)

    rows_per_chunk = blk // N_CHUNKS
    cols = D_FF // N_CHUNKS

    def block_step(s):
        @pl.when(i < n_valid)
        def _():
            @pl.when(i >= 1)
            def _():
                wait_scatter(s)

            xb_ref[...] = _load_rows(x_ref, blk).astype(BF16)
            for c in range(N_CHUNKS):
                lo = c * cols
                xb = xb_ref[...]
                gate = jnp.dot(xb, wgu_bf[:, lo:lo + cols], preferred_element_type=F32)
                up = jnp.dot(xb, wgu_bf[:, D_FF + lo:D_FF + lo + cols], preferred_element_type=F32)
                gate = jnp.minimum(gate + bgu_ref[0, :, lo:lo + cols], SWIGLU_LIMIT)
                up = jnp.clip(up + bgu_ref[0, :, D_FF + lo:D_FF + lo + cols], -SWIGLU_LIMIT, SWIGLU_LIMIT)
                act = (up + 1.0) * gate * jax.nn.sigmoid(SWIGLU_ALPHA * gate)
                act_ref[:, lo:lo + cols] = act.astype(BF16)
                for r in range(c * rows_per_chunk, (c + 1) * rows_per_chunk):
                    scatter_row(1 - s, r, r % 2)
            y = jnp.dot(act_ref[...], wd_bf[...], preferred_element_type=F32) + bd_ref[0]
            _store_rows(ybuf.at[s], y, blk)

        @pl.when(i == n_valid)
        def _():
            wait_scatter(s)

            def body(gi, carry):
                for u in range(DMA_UNROLL):
                    scatter_row(1 - s, gi * DMA_UNROLL + u)
                return carry
            lax.fori_loop(0, blk // DMA_UNROLL, body, 0)
            wait_scatter(1 - s)

    for s in range(2):
        pl.when(i % 2 == s)(functools.partial(block_step, s))


def _experts(x_slots, dst, block_e, n_valid, w_gu, b_gu, w_down, b_down, blk, n_real_rows):
    n_blocks = block_e.shape[0]
    assert n_blocks * blk == n_real_rows + N_EXPERTS * blk and blk % (2 * N_CHUNKS) == 0
    n_out_rows = n_real_rows + (N_EXPERTS + 1) * blk
    priming = n_real_rows + N_EXPERTS * blk + jnp.arange(blk, dtype=jnp.int32)
    dst3 = jnp.concatenate([priming, dst]).reshape(n_blocks + 1, 1, blk)
    used = lambda i, nv: jnp.minimum(i, nv[0] - 1)
    grid_spec = pltpu.PrefetchScalarGridSpec(
        num_scalar_prefetch=2,
        grid=(n_blocks,),
        in_specs=[
            pl.BlockSpec((1, 1, blk), lambda i, be, nv: (i, 0, 0), memory_space=pltpu.SMEM),
            pl.BlockSpec((blk * SUBLANES, LANES), lambda i, be, nv: (used(i, nv), 0)),
            pl.BlockSpec((1, D_MODEL, 2 * D_FF), lambda i, be, nv: (be[i], 0, 0)),
            pl.BlockSpec((1, 1, 2 * D_FF), lambda i, be, nv: (be[i], 0, 0)),
            pl.BlockSpec((1, D_FF, D_MODEL), lambda i, be, nv: (be[i], 0, 0)),
            pl.BlockSpec((1, 1, D_MODEL), lambda i, be, nv: (be[i], 0, 0)),
        ],
        out_specs=pl.BlockSpec(memory_space=pl.ANY),
        scratch_shapes=[pltpu.VMEM((2, blk * SUBLANES, LANES), F32),
                        pltpu.SemaphoreType.DMA((2,)),
                        pltpu.VMEM((D_MODEL, 2 * D_FF), BF16),
                        pltpu.VMEM((D_FF, D_MODEL), BF16),
                        pltpu.VMEM((blk, D_MODEL), BF16),
                        pltpu.VMEM((blk, D_FF), BF16)],
    )
    return pl.pallas_call(
        functools.partial(_expert_body, blk=blk, n_real_rows=n_real_rows),
        grid_spec=grid_spec,
        out_shape=jax.ShapeDtypeStruct((n_out_rows * SUBLANES, LANES), F32),
        compiler_params=pltpu.CompilerParams(
            dimension_semantics=("arbitrary",), vmem_limit_bytes=VMEM_LIMIT_BYTES),
        name="experts",
    )(block_e, n_valid, dst3, x_slots, w_gu, b_gu.reshape(N_EXPERTS, 1, 2 * D_FF),
      w_down, b_down.reshape(N_EXPERTS, 1, D_MODEL))


def _combine_body(*refs, tt, n_first):
    y_refs = refs[:TOP_K]
    h_ref, gate_ref, g2_ref, b2_ref, out_first, out_second = refs[TOP_K:]
    i = pl.program_id(0)
    gates = gate_ref[...]
    f = gates[:, 0:1] * _load_rows(y_refs[0], tt)
    for kk in range(1, TOP_K):
        f = f + gates[:, kk:kk + 1] * _load_rows(y_refs[kk], tt)
    out = _layer_norm(DEEPNORM_ALPHA * _load_rows(h_ref, tt) + f, g2_ref[...], b2_ref[...])

    @pl.when(i < n_first)
    def _():
        out_first[...] = out

    @pl.when(i >= n_first)
    def _():
        out_second[...] = out


def _combine(y_rows, gates, h_all, ln2_g, ln2_b, tt, n_tokens_first):
    T = gates.shape[0]
    n_tiles = T // tt
    n_first = n_tokens_first // tt
    assert T % tt == 0 and n_tokens_first % tt == 0 and 0 < n_first < n_tiles
    row_blk = lambda imap: pl.BlockSpec((tt * SUBLANES, LANES), imap)
    y_specs = [row_blk(functools.partial(lambda i, kk: (kk * n_tiles + i, 0), kk=kk)) for kk in range(TOP_K)]
    return pl.pallas_call(
        functools.partial(_combine_body, tt=tt, n_first=n_first),
        grid=(n_tiles,),
        in_specs=y_specs + [
            row_blk(lambda i: (i, 0)),
            pl.BlockSpec((tt, TOP_K), lambda i: (i, 0)),
            pl.BlockSpec((1, D_MODEL), lambda i: (0, 0)),
            pl.BlockSpec((1, D_MODEL), lambda i: (0, 0)),
        ],
        out_specs=[pl.BlockSpec((tt, D_MODEL), lambda i: (jnp.minimum(i, n_first - 1), 0)),
                   pl.BlockSpec((tt, D_MODEL), lambda i: (jnp.maximum(i - n_first, 0), 0))],
        out_shape=[jax.ShapeDtypeStruct((n_tokens_first, D_MODEL), F32),
                   jax.ShapeDtypeStruct((T - n_tokens_first, D_MODEL), F32)],
        compiler_params=pltpu.CompilerParams(
            dimension_semantics=("arbitrary",), vmem_limit_bytes=VMEM_LIMIT_BYTES),
        name="combine",
    )(*([y_rows] * TOP_K), h_all, gates, ln2_g, ln2_b)


def kernel(x_prompt, x_sample, state_retention, state_conv, w_in, beta_ret, beta_conv, conv_w, conv_b,
           w_out, ln1_g, ln1_b, w_router, b_router, w_gu, b_gu, w_down, b_down, ln2_g, ln2_b):
    assert w_in.shape[0] == DEPTH == 1
    B, S, _ = x_prompt.shape
    Bd, Ld, _ = x_sample.shape
    Tp, Ts = B * S, Bd * Ld
    T = Tp + Ts
    wr_t = w_router[0].T
    wr_hi = wr_t.astype(BF16)
    wr_lo = (wr_t - wr_hi.astype(F32)).astype(BF16)
    w = dict(w_in=w_in[0].astype(BF16), w_out=w_out[0].astype(BF16), conv_w=conv_w[0],
             conv_b=conv_b[0][None], beta_ret=beta_ret[0][None], beta_conv=beta_conv[0][None],
             ln1_g=ln1_g[0][None], ln1_b=ln1_b[0][None], w_router=jnp.stack([wr_hi, wr_lo]),
             b_router=b_router[0][:, None])

    s_ret0 = jnp.zeros((B, N_RET_HEADS, HEAD_DIM, HEAD_DIM), F32)
    s_conv0 = jnp.zeros((B, CONV_W - 1, D_CONV), F32)
    h_s, choices_s, gates_s, ranks_s, counts_s, sret_s, sconv_s = _mixer(
        x_sample, state_retention[0], state_conv[0], PAST_LEN, SAMPLE_BATCH_BLOCK, Ld, w)
    h_all, choices_t, gates_t, ranks_t, counts_p, sret_p, sconv_p = _mixer(
        x_prompt, s_ret0, s_conv0, 0, 1, PROMPT_CHUNK, w, tail=(h_s, choices_s, gates_s, ranks_s))

    pos, pad_dst, block_e, n_valid, zero_blocks = _routing_plan(
        choices_t, ranks_t, counts_p[:, 0].astype(jnp.int32), counts_s[:, 0].astype(jnp.int32), Tp,
        EXPERT_BLOCK)
    x_slots, dst = _dispatch(h_all, pos, pad_dst, zero_blocks, DISPATCH_TILE, EXPERT_BLOCK)
    y_rows = _experts(x_slots, dst, block_e, n_valid, w_gu[0], b_gu[0], w_down[0], b_down[0],
                      EXPERT_BLOCK, T * TOP_K)
    y_p, y_s = _combine(y_rows, gates_t.T, h_all, ln2_g[0][None], ln2_b[0][None], COMBINE_TILE, Tp)

    return (y_p.reshape(B, S, D_MODEL), y_s.reshape(Bd, Ld, D_MODEL),
            sret_p[None], sconv_p[None], sret_s[None], sconv_s[None])
```

```python
import functools

import jax
import jax.numpy as jnp
from jax import lax
from jax.experimental import pallas as pl
from jax.experimental.pallas import tpu as pltpu

D_MODEL = 1024
N_RET_HEADS = 4
HEAD_DIM = 128
D_RET = N_RET_HEADS * HEAD_DIM
D_CONV = D_MODEL - D_RET
CONV_W = 3
N_EXPERTS = 32
TOP_K = 4
D_FF = D_MODEL
SWIGLU_LIMIT = 7.0
SWIGLU_ALPHA = 1.702
ROPE_BASE = 10000.0
LN_EPS = 1e-5
GN_EPS = 1e-5
DEPTH = 1
PAST_LEN = 1024
DEEPNORM_ALPHA = (2.0 * DEPTH) ** 0.25

LANES = 128
SUBLANES = 8
ROW_CHUNKS = D_MODEL // LANES
assert ROW_CHUNKS == SUBLANES

PROMPT_CHUNK = 512
SAMPLE_BATCH_BLOCK = 8
EXPERT_BLOCK = 512
DISPATCH_TILE = 512
COMBINE_TILE = 512
DMA_UNROLL = 8
N_CHUNKS = 4
VMEM_LIMIT_BYTES = 56 * 1024 * 1024

F32 = jnp.float32
BF16 = jnp.bfloat16


def _layer_norm(x, g, b):
    mu = jnp.mean(x, axis=-1, keepdims=True)
    xc = x - mu
    var = jnp.mean(xc * xc, axis=-1, keepdims=True)
    return xc * lax.rsqrt(var + LN_EPS) * g + b


def _load_rows(ref, n_rows):
    return jnp.concatenate(
        [ref[pl.ds(j, n_rows, stride=SUBLANES), :] for j in range(ROW_CHUNKS)], axis=1)


def _store_rows(ref, val, n_rows):
    for j in range(ROW_CHUNKS):
        ref[pl.ds(j, n_rows, stride=SUBLANES), :] = val[:, j * LANES:(j + 1) * LANES]


def _row_tile(r):
    if isinstance(r, int):
        return pl.ds(r * SUBLANES, SUBLANES)
    return pl.ds(pl.multiple_of(r * SUBLANES, SUBLANES), SUBLANES)


N_MIXER_INPUTS = 19
N_TOKEN_OUTPUTS = 4


def _mixer_body(*refs, nb, C, n_batch_steps, n_tail):
    if not n_tail:
        _mixer_compute(*refs, nb=nb, C=C)
        return
    tails = refs[N_MIXER_INPUTS:N_MIXER_INPUTS + N_TOKEN_OUTPUTS]
    outs = refs[N_MIXER_INPUTS + N_TOKEN_OUTPUTS:N_MIXER_INPUTS + 2 * N_TOKEN_OUTPUTS]
    bstep = pl.program_id(0)

    @pl.when(bstep < n_batch_steps)
    def _():
        _mixer_compute(*refs[:N_MIXER_INPUTS], *refs[N_MIXER_INPUTS + N_TOKEN_OUTPUTS:], nb=nb, C=C)

    @pl.when((bstep == n_batch_steps) & (pl.program_id(1) < n_tail))
    def _():
        for tail_ref, out_ref in zip(tails, outs):
            out_ref[...] = tail_ref[...]


def _route(logits_t, run_ref):
    n_tok = logits_t.shape[1]
    expert = lax.broadcasted_iota(jnp.int32, logits_t.shape, 0)
    vals, idxs, hots = [], [], []
    for _ in range(TOP_K):
        m = jnp.max(logits_t, axis=0, keepdims=True)
        idx = jnp.min(jnp.where(logits_t == m, expert, N_EXPERTS), axis=0, keepdims=True)
        hot = expert == idx
        vals.append(m)
        idxs.append(idx)
        hots.append(hot)
        logits_t = jnp.where(hot, -jnp.inf, logits_t)
    ex = [jnp.exp(v - vals[0]) for v in vals]
    total = ex[0]
    for e in ex[1:]:
        total = total + e
    gates = jnp.concatenate([e / total for e in ex], axis=0)

    chosen = hots[0].astype(F32)
    for hot in hots[1:]:
        chosen = chosen + hot.astype(F32)
    chosen = chosen.astype(BF16)
    t_row = lax.broadcasted_iota(jnp.int32, (n_tok, n_tok), 0)
    t_col = lax.broadcasted_iota(jnp.int32, (n_tok, n_tok), 1)
    earlier = jnp.where(t_row < t_col, 1.0, 0.0).astype(BF16)
    before = run_ref[...] + jnp.dot(chosen, earlier, preferred_element_type=F32)
    ranks = [jnp.sum(jnp.where(hot, before, 0.0), axis=0, keepdims=True) for hot in hots]
    run_ref[...] = run_ref[...] + jnp.dot(chosen, jnp.ones((n_tok, n_tok), BF16), preferred_element_type=F32)
    return jnp.concatenate(idxs, axis=0), gates, jnp.concatenate(ranks, axis=0).astype(jnp.int32)


def _mixer_compute(x_ref, cos_ref, sin_ref, decay_ref, qdec_ref, kdec_ref, sdec_ref, win_ref, wout_ref,
                   convw_ref, convb_ref, bret_ref, bconv_ref, g1_ref, b1_ref, wr_ref, br_ref,
                   sret_ref, sconv_ref, h_ref, choice_ref, gate_ref, rank_ref, count_ref, sret_out, sconv_out,
                   mix_ref, run_ref, *, nb, C):
    c = pl.program_id(1)

    @pl.when((pl.program_id(0) == 0) & (c == 0))
    def _():
        run_ref[...] = jnp.zeros(run_ref.shape, F32)

    @pl.when(c == 0)
    def _():
        sret_out[...] = sret_ref[...]
        sconv_out[...] = sconv_ref[...]

    x = x_ref[...].reshape(nb * C, D_MODEL)
    xb = x.astype(BF16)

    def proj(col0, width):
        return jnp.dot(xb, win_ref[:, col0:col0 + width], preferred_element_type=F32)

    q = proj(0, D_RET)
    k = proj(D_RET, D_RET)
    v = proj(2 * D_RET, D_RET)
    g = proj(3 * D_RET, D_RET)
    bg = proj(4 * D_RET, D_CONV)
    cg = proj(4 * D_RET + D_CONV, D_CONV)
    hc = proj(4 * D_RET + 2 * D_CONV, D_CONV)

    cos = cos_ref[...]
    sin = sin_ref[...]
    row = lax.broadcasted_iota(jnp.int32, (C, D_CONV), 0)
    k_scale = HEAD_DIM ** -0.5

    for b in range(nb):
        r0 = b * C
        for h in range(N_RET_HEADS):
            c0 = h * HEAD_DIM
            qh = q[r0:r0 + C, c0:c0 + HEAD_DIM]
            kh = k[r0:r0 + C, c0:c0 + HEAD_DIM]
            vh = v[r0:r0 + C, c0:c0 + HEAD_DIM]
            qh = qh * cos + pltpu.roll(qh, HEAD_DIM // 2, axis=1) * sin
            kh = (kh * cos + pltpu.roll(kh, HEAD_DIM // 2, axis=1) * sin) * k_scale
            qb = qh.astype(BF16)
            kb = kh.astype(BF16)
            vb = vh.astype(BF16)
            s_old = sret_out[b, h]
            scores = lax.dot_general(qb, kb, (((1,), (1,)), ((), ())), preferred_element_type=F32)
            scores = scores * decay_ref[h]
            intra = jnp.dot(scores.astype(BF16), vb, preferred_element_type=F32)
            cross = jnp.dot(qb, s_old.astype(BF16), preferred_element_type=F32) * qdec_ref[h]
            o = intra + cross
            kd = (kh * kdec_ref[h]).astype(BF16)
            s_new = sdec_ref[h] * s_old + lax.dot_general(
                kd, vb, (((0,), (0,)), ((), ())), preferred_element_type=F32)
            sret_out[b, h] = s_new
            mu = jnp.mean(o, axis=-1, keepdims=True)
            oc = o - mu
            var = jnp.mean(oc * oc, axis=-1, keepdims=True)
            on = oc * lax.rsqrt(var + GN_EPS)
            gh = g[r0:r0 + C, c0:c0 + HEAD_DIM]
            ret = on * (gh * jax.nn.sigmoid(gh)) * bret_ref[:, c0:c0 + HEAD_DIM]
            mix_ref[r0:r0 + C, c0:c0 + HEAD_DIM] = ret.astype(BF16)

        u = cg[r0:r0 + C] * hc[r0:r0 + C]
        prev = sconv_out[b]
        u1 = jnp.where(row == 0, prev[1:2], pltpu.roll(u, 1, axis=0))
        u2 = jnp.where(row == 0, prev[0:1], jnp.where(row == 1, prev[1:2], pltpu.roll(u, 2, axis=0)))
        z = convb_ref[...] + convw_ref[0:1] * u2 + convw_ref[1:2] * u1 + convw_ref[2:3] * u
        conv_out = bg[r0:r0 + C] * z * bconv_ref[...]
        mix_ref[r0:r0 + C, D_RET:D_RET + D_CONV] = conv_out.astype(BF16)
        sconv_out[b] = u[C - 2:C]

    m = jnp.dot(mix_ref[...], wout_ref[...], preferred_element_type=F32)
    hval = _layer_norm(DEEPNORM_ALPHA * x + m, g1_ref[...], b1_ref[...])
    _store_rows(h_ref, hval, nb * C)
    h_hi = hval.astype(BF16)
    h_lo = (hval - h_hi.astype(F32)).astype(BF16)
    nt_dot = lambda a, b: lax.dot_general(a, b, (((1,), (1,)), ((), ())), preferred_element_type=F32)
    logits_t = nt_dot(wr_ref[0], h_hi) + nt_dot(wr_ref[0], h_lo) + nt_dot(wr_ref[1], h_hi) + br_ref[...]
    choice_ref[...], gate_ref[...], rank_ref[...] = _route(logits_t, run_ref)
    count_ref[...] = run_ref[:, :LANES]


def _retention_tables(C, pos0, L):
    lg = jnp.log1p(-jnp.power(2.0, -5.0 - jnp.arange(N_RET_HEADS, dtype=F32)))
    i = jnp.arange(C, dtype=F32)
    diff = i[:, None] - i[None, :]
    decay = jnp.where(diff[None] >= 0, jnp.exp(lg[:, None, None] * jnp.maximum(diff, 0.0)[None]), 0.0)
    qdec = jnp.exp(lg[:, None] * (i + 1.0)[None, :])
    kdec = jnp.exp(lg[:, None] * (C - 1.0 - i)[None, :])
    sdec = jnp.exp(lg * C)
    qdec = jnp.broadcast_to(qdec[:, :, None], (N_RET_HEADS, C, HEAD_DIM))
    kdec = jnp.broadcast_to(kdec[:, :, None], (N_RET_HEADS, C, HEAD_DIM))
    sdec = jnp.broadcast_to(sdec[:, None, None], (N_RET_HEADS, 1, HEAD_DIM))
    pos = pos0 + jnp.arange(L, dtype=jnp.int32)
    inv_freq = ROPE_BASE ** (-jnp.arange(0, HEAD_DIM, 2, dtype=F32) / HEAD_DIM)
    ang = pos.astype(F32)[:, None] * inv_freq[None, :]
    cos = jnp.cos(ang)
    sin = jnp.sin(ang)
    cos_full = jnp.concatenate([cos, cos], axis=-1)
    sin_signed = jnp.concatenate([-sin, sin], axis=-1)
    return cos_full, sin_signed, decay, qdec, kdec, sdec


def _mixer(x, s_ret, s_conv, pos0, nb, C, w, tail=None):
    B, L, _ = x.shape
    nc = L // C
    assert nb == 1 or nc == 1
    nbs = B // nb
    rows = nb * C
    n_tokens = B * L
    n_tail = 0
    if tail is not None:
        n_tokens += tail[1].shape[1]
        n_tail = tail[1].shape[1] // rows
        assert tail[1].shape[1] % rows == 0 and 0 < n_tail <= nc
    cos, sin, decay, qdec, kdec, sdec = _retention_tables(C, pos0, L)

    bb = lambda b: jnp.minimum(b, nbs - 1)
    cc = lambda b, c: jnp.where(b < nbs, c, nc - 1)
    out_blk = lambda b, c: jnp.where(b < nbs, b * nc + c, nbs * nc + jnp.minimum(c, n_tail - 1))
    const2 = lambda b, c: (0, 0)
    const3 = lambda b, c: (0, 0, 0)
    full = lambda a: pl.BlockSpec(a.shape, const2 if a.ndim == 2 else const3)
    per_token = lambda imap: pl.BlockSpec((TOP_K, rows), lambda b, c: (0, imap(b, c)))
    in_arrays = [x, cos, sin, decay, qdec, kdec, sdec, w["w_in"], w["w_out"], w["conv_w"], w["conv_b"],
                 w["beta_ret"], w["beta_conv"], w["ln1_g"], w["ln1_b"], w["w_router"], w["b_router"],
                 s_ret, s_conv]
    assert len(in_arrays) == N_MIXER_INPUTS
    in_specs = [pl.BlockSpec((nb, C, D_MODEL), lambda b, c: (bb(b), cc(b, c), 0)),
                pl.BlockSpec((C, HEAD_DIM), lambda b, c: (cc(b, c), 0)),
                pl.BlockSpec((C, HEAD_DIM), lambda b, c: (cc(b, c), 0))]
    in_specs += [full(a) for a in in_arrays[3:17]]
    in_specs += [pl.BlockSpec((nb, N_RET_HEADS, HEAD_DIM, HEAD_DIM), lambda b, c: (bb(b), 0, 0, 0)),
                 pl.BlockSpec((nb, CONV_W - 1, D_CONV), lambda b, c: (bb(b), 0, 0))]
    if n_tail:
        tail_blk = lambda b, c: jnp.where(b < nbs, 0, jnp.minimum(c, n_tail - 1))
        in_arrays += list(tail)
        in_specs += [pl.BlockSpec((rows * SUBLANES, LANES), lambda b, c: (tail_blk(b, c), 0))]
        in_specs += [per_token(tail_blk)] * (N_TOKEN_OUTPUTS - 1)
    out_shape = [jax.ShapeDtypeStruct((n_tokens * SUBLANES, LANES), F32),
                 jax.ShapeDtypeStruct((TOP_K, n_tokens), jnp.int32),
                 jax.ShapeDtypeStruct((TOP_K, n_tokens), F32),
                 jax.ShapeDtypeStruct((TOP_K, n_tokens), jnp.int32),
                 jax.ShapeDtypeStruct((N_EXPERTS, LANES), F32),
                 jax.ShapeDtypeStruct((B, N_RET_HEADS, HEAD_DIM, HEAD_DIM), F32),
                 jax.ShapeDtypeStruct((B, CONV_W - 1, D_CONV), F32)]
    assert len(out_shape) == N_TOKEN_OUTPUTS + 3
    out_specs = [pl.BlockSpec((rows * SUBLANES, LANES), lambda b, c: (out_blk(b, c), 0))]
    out_specs += [per_token(out_blk)] * (N_TOKEN_OUTPUTS - 1)
    out_specs += [pl.BlockSpec((N_EXPERTS, LANES), const2),
                  pl.BlockSpec((nb, N_RET_HEADS, HEAD_DIM, HEAD_DIM), lambda b, c: (bb(b), 0, 0, 0)),
                  pl.BlockSpec((nb, CONV_W - 1, D_CONV), lambda b, c: (bb(b), 0, 0))]
    return pl.pallas_call(
        functools.partial(_mixer_body, nb=nb, C=C, n_batch_steps=nbs, n_tail=n_tail),
        grid=(nbs + (1 if n_tail else 0), nc), in_specs=in_specs, out_specs=out_specs, out_shape=out_shape,
        scratch_shapes=[pltpu.VMEM((rows, D_MODEL), BF16), pltpu.VMEM((N_EXPERTS, rows), F32)],
        compiler_params=pltpu.CompilerParams(
            dimension_semantics=("arbitrary", "arbitrary"), vmem_limit_bytes=VMEM_LIMIT_BYTES),
        name="mixer",
    )(*in_arrays)


def _lookup(table, choices):
    experts = jnp.arange(N_EXPERTS, dtype=jnp.int32)
    return jnp.sum(jnp.where(choices[..., None] == experts, table, 0), axis=-1)


def _routing_plan(choices_t, ranks_t, counts_first, counts_second, n_first, blk):
    T = choices_t.shape[1]
    A = T * TOP_K
    assert A % blk == 0
    n_blocks = A // blk + N_EXPERTS
    experts = jnp.arange(N_EXPERTS, dtype=jnp.int32)
    counts = counts_first + counts_second
    nblk_e = (counts + blk - 1) // blk
    bend = jnp.cumsum(nblk_e)
    n_valid = bend[-1]
    start = (bend - nblk_e) * blk
    token = jnp.arange(T, dtype=jnp.int32)[None, :]
    rank = ranks_t + jnp.where(token >= n_first, _lookup(counts_first, choices_t), 0)
    pos = _lookup(start, choices_t) + rank

    bi = jnp.arange(n_blocks, dtype=jnp.int32)
    block_e = jnp.sum((bi[:, None] >= bend[None, :]).astype(jnp.int32), axis=1)
    block_e = jnp.where(bi < n_valid, block_e, block_e[n_valid - 1])
    pad_dst = (A + block_e[:, None] * blk + jnp.arange(blk, dtype=jnp.int32)[None, :]).reshape(n_blocks * blk)
    last_block = jnp.where(nblk_e > 0, bend - 1, -1)
    unused = n_valid + experts
    unused = jnp.where(unused < n_blocks, unused, -1)
    zero_blocks = jnp.concatenate([last_block, unused]).astype(jnp.int32)
    return pos, pad_dst, block_e, n_valid.reshape(1).astype(jnp.int32), zero_blocks


def _dispatch_body(zero_ref, pos_ref, h_ref, pad_dst_hbm, xs_hbm, dst_hbm, hbuf, dst_tab, sem, tab_sem,
                   *, tt, blk, n_tokens):
    i = pl.program_id(0)
    n = pl.num_programs(0)
    slot = i % 2

    def wait_rows(s):
        for _ in range(TOP_K):
            pltpu.make_async_copy(hbuf.at[s], xs_hbm.at[pl.ds(0, tt * SUBLANES)], sem.at[s]).wait()

    @pl.when(i == 0)
    def _():
        load_table = pltpu.make_async_copy(pad_dst_hbm, dst_tab, tab_sem)
        load_table.start()
        load_table.wait()
        hbuf[1] = jnp.zeros(hbuf.shape[1:], F32)
        zeros = hbuf.at[1, pl.ds(0, blk * SUBLANES)]
        for e in range(zero_ref.shape[0]):
            @pl.when(zero_ref[e] >= 0)
            def _():
                first_row = pl.multiple_of(zero_ref[e] * (blk * SUBLANES), blk * SUBLANES)
                pltpu.make_async_copy(zeros, xs_hbm.at[pl.ds(first_row, blk * SUBLANES)], sem.at[1]).start()
        for e in range(zero_ref.shape[0]):
            @pl.when(zero_ref[e] >= 0)
            def _():
                pltpu.make_async_copy(zeros, xs_hbm.at[pl.ds(0, blk * SUBLANES)], sem.at[1]).wait()

    @pl.when(i >= 2)
    def _():
        wait_rows(slot)

    hbuf[slot] = h_ref[...]
    first_token = i * tt
    for t in range(tt):
        row = hbuf.at[slot, _row_tile(t)]
        for kk in range(TOP_K):
            p = pos_ref[kk, t]
            dst_tab[p] = first_token + (kk * n_tokens + t)
            pltpu.make_async_copy(row, xs_hbm.at[_row_tile(p)], sem.at[slot]).start(kk % 2)

    @pl.when(i == n - 1)
    def _():
        wait_rows(slot)

        @pl.when(i >= 1)
        def _():
            wait_rows(1 - slot)

        store_table = pltpu.make_async_copy(dst_tab, dst_hbm, tab_sem)
        store_table.start()
        store_table.wait()


def _dispatch(h_all, pos, pad_dst, zero_blocks, tt, blk):
    T = pos.shape[1]
    n_slots = pad_dst.shape[0]
    n_tiles = T // tt
    assert T % tt == 0 and blk <= tt
    grid_spec = pltpu.PrefetchScalarGridSpec(
        num_scalar_prefetch=1,
        grid=(n_tiles,),
        in_specs=[pl.BlockSpec((TOP_K, tt), lambda i, zb: (0, i), memory_space=pltpu.SMEM),
                  pl.BlockSpec((tt * SUBLANES, LANES), lambda i, zb: (i, 0)),
                  pl.BlockSpec(memory_space=pl.ANY)],
        out_specs=[pl.BlockSpec(memory_space=pl.ANY), pl.BlockSpec(memory_space=pl.ANY)],
        scratch_shapes=[pltpu.VMEM((2, tt * SUBLANES, LANES), F32),
                        pltpu.SMEM((n_slots,), jnp.int32),
                        pltpu.SemaphoreType.DMA((2,)),
                        pltpu.SemaphoreType.DMA],
    )
    return pl.pallas_call(
        functools.partial(_dispatch_body, tt=tt, blk=blk, n_tokens=T),
        grid_spec=grid_spec,
        out_shape=[jax.ShapeDtypeStruct((n_slots * SUBLANES, LANES), F32),
                   jax.ShapeDtypeStruct((n_slots,), jnp.int32)],
        compiler_params=pltpu.CompilerParams(
            dimension_semantics=("arbitrary",), vmem_limit_bytes=VMEM_LIMIT_BYTES),
        name="dispatch",
    )(zero_blocks, pos, h_all, pad_dst)


def _expert_body(be_ref, nv_ref, dst_ref, x_ref, wgu_ref, bgu_ref, wd_ref, bd_ref,
                 y_hbm, ybuf, osem, wgu_bf, wd_bf, xb_ref, act_ref, *, blk, n_real_rows):
    i = pl.program_id(0)
    n_valid = nv_ref[0]

    def scatter_row(s, r, prio=0):
        pltpu.make_async_copy(
            ybuf.at[s, _row_tile(r)], y_hbm.at[_row_tile(dst_ref[0, 0, r])], osem.at[s]).start(prio)

    def wait_scatter(s):
        pltpu.make_async_copy(ybuf.at[s], y_hbm.at[pl.ds(0, blk * SUBLANES)], osem.at[s]).wait()

    @pl.when(i == 0)
    def _():
        ybuf[...] = jnp.zeros(ybuf.shape, F32)
        for e in range(N_EXPERTS):
            pltpu.make_async_copy(
                ybuf.at[0], y_hbm.at[pl.ds((n_real_rows + e * blk) * SUBLANES, blk * SUBLANES)],
                osem.at[0]).start()
        for e in range(N_EXPERTS):
            wait_scatter(0)

    @pl.when((i < n_valid) & ((i == 0) | (be_ref[i] != be_ref[jnp.maximum(i - 1, 0)])))
    def _():
        wgu_bf[...] = wgu_ref[0].astype(BF16)
        wd_bf[...] = wd_ref[0].astype(BF16)

    rows_per_chunk = blk // N_CHUNKS
    cols = D_FF // N_CHUNKS

    def block_step(s):
        @pl.when(i < n_valid)
        def _():
            @pl.when(i >= 1)
            def _():
                wait_scatter(s)

            xb_ref[...] = _load_rows(x_ref, blk).astype(BF16)
            for c in range(N_CHUNKS):
                lo = c * cols
                xb = xb_ref[...]
                gate = jnp.dot(xb, wgu_bf[:, lo:lo + cols], preferred_element_type=F32)
                up = jnp.dot(xb, wgu_bf[:, D_FF + lo:D_FF + lo + cols], preferred_element_type=F32)
                gate = jnp.minimum(gate + bgu_ref[0, :, lo:lo + cols], SWIGLU_LIMIT)
                up = jnp.clip(up + bgu_ref[0, :, D_FF + lo:D_FF + lo + cols], -SWIGLU_LIMIT, SWIGLU_LIMIT)
                act = (up + 1.0) * gate * jax.nn.sigmoid(SWIGLU_ALPHA * gate)
                act_ref[:, lo:lo + cols] = act.astype(BF16)
                for r in range(c * rows_per_chunk, (c + 1) * rows_per_chunk):
                    scatter_row(1 - s, r, r % 2)
            y = jnp.dot(act_ref[...], wd_bf[...], preferred_element_type=F32) + bd_ref[0]
            _store_rows(ybuf.at[s], y, blk)

        @pl.when(i == n_valid)
        def _():
            wait_scatter(s)

            def body(gi, carry):
                for u in range(DMA_UNROLL):
                    scatter_row(1 - s, gi * DMA_UNROLL + u)
                return carry
            lax.fori_loop(0, blk // DMA_UNROLL, body, 0)
            wait_scatter(1 - s)

    for s in range(2):
        pl.when(i % 2 == s)(functools.partial(block_step, s))


def _experts(x_slots, dst, block_e, n_valid, w_gu, b_gu, w_down, b_down, blk, n_real_rows):
    n_blocks = block_e.shape[0]
    assert n_blocks * blk == n_real_rows + N_EXPERTS * blk and blk % (2 * N_CHUNKS) == 0
    n_out_rows = n_real_rows + (N_EXPERTS + 1) * blk
    priming = n_real_rows + N_EXPERTS * blk + jnp.arange(blk, dtype=jnp.int32)
    dst3 = jnp.concatenate([priming, dst]).reshape(n_blocks + 1, 1, blk)
    used = lambda i, nv: jnp.minimum(i, nv[0] - 1)
    grid_spec = pltpu.PrefetchScalarGridSpec(
        num_scalar_prefetch=2,
        grid=(n_blocks,),
        in_specs=[
            pl.BlockSpec((1, 1, blk), lambda i, be, nv: (i, 0, 0), memory_space=pltpu.SMEM),
            pl.BlockSpec((blk * SUBLANES, LANES), lambda i, be, nv: (used(i, nv), 0)),
            pl.BlockSpec((1, D_MODEL, 2 * D_FF), lambda i, be, nv: (be[i], 0, 0)),
            pl.BlockSpec((1, 1, 2 * D_FF), lambda i, be, nv: (be[i], 0, 0)),
            pl.BlockSpec((1, D_FF, D_MODEL), lambda i, be, nv: (be[i], 0, 0)),
            pl.BlockSpec((1, 1, D_MODEL), lambda i, be, nv: (be[i], 0, 0)),
        ],
        out_specs=pl.BlockSpec(memory_space=pl.ANY),
        scratch_shapes=[pltpu.VMEM((2, blk * SUBLANES, LANES), F32),
                        pltpu.SemaphoreType.DMA((2,)),
                        pltpu.VMEM((D_MODEL, 2 * D_FF), BF16),
                        pltpu.VMEM((D_FF, D_MODEL), BF16),
                        pltpu.VMEM((blk, D_MODEL), BF16),
                        pltpu.VMEM((blk, D_FF), BF16)],
    )
    return pl.pallas_call(
        functools.partial(_expert_body, blk=blk, n_real_rows=n_real_rows),
        grid_spec=grid_spec,
        out_shape=jax.ShapeDtypeStruct((n_out_rows * SUBLANES, LANES), F32),
        compiler_params=pltpu.CompilerParams(
            dimension_semantics=("arbitrary",), vmem_limit_bytes=VMEM_LIMIT_BYTES),
        name="experts",
    )(block_e, n_valid, dst3, x_slots, w_gu, b_gu.reshape(N_EXPERTS, 1, 2 * D_FF),
      w_down, b_down.reshape(N_EXPERTS, 1, D_MODEL))


def _combine_body(*refs, tt, n_first):
    y_refs = refs[:TOP_K]
    h_ref, gate_ref, g2_ref, b2_ref, out_first, out_second = refs[TOP_K:]
    i = pl.program_id(0)
    gates = gate_ref[...]
    f = gates[:, 0:1] * _load_rows(y_refs[0], tt)
    for kk in range(1, TOP_K):
        f = f + gates[:, kk:kk + 1] * _load_rows(y_refs[kk], tt)
    out = _layer_norm(DEEPNORM_ALPHA * _load_rows(h_ref, tt) + f, g2_ref[...], b2_ref[...])

    @pl.when(i < n_first)
    def _():
        out_first[...] = out

    @pl.when(i >= n_first)
    def _():
        out_second[...] = out


def _combine(y_rows, gates, h_all, ln2_g, ln2_b, tt, n_tokens_first):
    T = gates.shape[0]
    n_tiles = T // tt
    n_first = n_tokens_first // tt
    assert T % tt == 0 and n_tokens_first % tt == 0 and 0 < n_first < n_tiles
    row_blk = lambda imap: pl.BlockSpec((tt * SUBLANES, LANES), imap)
    y_specs = [row_blk(functools.partial(lambda i, kk: (kk * n_tiles + i, 0), kk=kk)) for kk in range(TOP_K)]
    return pl.pallas_call(
        functools.partial(_combine_body, tt=tt, n_first=n_first),
        grid=(n_tiles,),
        in_specs=y_specs + [
            row_blk(lambda i: (i, 0)),
            pl.BlockSpec((tt, TOP_K), lambda i: (i, 0)),
            pl.BlockSpec((1, D_MODEL), lambda i: (0, 0)),
            pl.BlockSpec((1, D_MODEL), lambda i: (0, 0)),
        ],
        out_specs=[pl.BlockSpec((tt, D_MODEL), lambda i: (jnp.minimum(i, n_first - 1), 0)),
                   pl.BlockSpec((tt, D_MODEL), lambda i: (jnp.maximum(i - n_first, 0), 0))],
        out_shape=[jax.ShapeDtypeStruct((n_tokens_first, D_MODEL), F32),
                   jax.ShapeDtypeStruct((T - n_tokens_first, D_MODEL), F32)],
        compiler_params=pltpu.CompilerParams(
            dimension_semantics=("arbitrary",), vmem_limit_bytes=VMEM_LIMIT_BYTES),
        name="combine",
    )(*([y_rows] * TOP_K), h_all, gates, ln2_g, ln2_b)


def kernel(x_prompt, x_sample, state_retention, state_conv, w_in, beta_ret, beta_conv, conv_w, conv_b,
           w_out, ln1_g, ln1_b, w_router, b_router, w_gu, b_gu, w_down, b_down, ln2_g, ln2_b):
    assert w_in.shape[0] == DEPTH == 1
    B, S, _ = x_prompt.shape
    Bd, Ld, _ = x_sample.shape
    Tp, Ts = B * S, Bd * Ld
    T = Tp + Ts
    wr_t = w_router[0].T
    wr_hi = wr_t.astype(BF16)
    wr_lo = (wr_t - wr_hi.astype(F32)).astype(BF16)
    w = dict(w_in=w_in[0].astype(BF16), w_out=w_out[0].astype(BF16), conv_w=conv_w[0],
             conv_b=conv_b[0][None], beta_ret=beta_ret[0][None], beta_conv=beta_conv[0][None],
             ln1_g=ln1_g[0][None], ln1_b=ln1_b[0][None], w_router=jnp.stack([wr_hi, wr_lo]),
             b_router=b_router[0][:, None])

    s_ret0 = jnp.zeros((B, N_RET_HEADS, HEAD_DIM, HEAD_DIM), F32)
    s_conv0 = jnp.zeros((B, CONV_W - 1, D_CONV), F32)
    h_s, choices_s, gates_s, ranks_s, counts_s, sret_s, sconv_s = _mixer(
        x_sample, state_retention[0], state_conv[0], PAST_LEN, SAMPLE_BATCH_BLOCK, Ld, w)
    h_all, choices_t, gates_t, ranks_t, counts_p, sret_p, sconv_p = _mixer(
        x_prompt, s_ret0, s_conv0, 0, 1, PROMPT_CHUNK, w, tail=(h_s, choices_s, gates_s, ranks_s))

    pos, pad_dst, block_e, n_valid, zero_blocks = _routing_plan(
        choices_t, ranks_t, counts_p[:, 0].astype(jnp.int32), counts_s[:, 0].astype(jnp.int32), Tp,
        EXPERT_BLOCK)
    x_slots, dst = _dispatch(h_all, pos, pad_dst, zero_blocks, DISPATCH_TILE, EXPERT_BLOCK)
    y_rows = _experts(x_slots, dst, block_e, n_valid, w_gu[0], b_gu[0], w_down[0], b_down[0],
                      EXPERT_BLOCK, T * TOP_K)
    y_p, y_s = _combine(y_rows, gates_t.T, h_all, ln2_g[0][None], ln2_b[0][None], COMBINE_TILE, Tp)

    return (y_p.reshape(B, S, D_MODEL), y_s.reshape(Bd, Ld, D_MODEL),
            sret_p[None], sconv_p[None], sret_s[None], sconv_s[None])
```

```python
import functools

import jax
import jax.numpy as jnp
from jax import lax
from jax.experimental import pallas as pl
from jax.experimental.pallas import tpu as pltpu

D_MODEL = 1024
N_RET_HEADS = 4
HEAD_DIM = 128
D_RET = N_RET_HEADS * HEAD_DIM
D_CONV = D_MODEL - D_RET
CONV_W = 3
N_EXPERTS = 32
TOP_K = 4
D_FF = D_MODEL
SWIGLU_LIMIT = 7.0
SWIGLU_ALPHA = 1.702
ROPE_BASE = 10000.0
LN_EPS = 1e-5
GN_EPS = 1e-5
DEPTH = 1
PAST_LEN = 1024
DEEPNORM_ALPHA = (2.0 * DEPTH) ** 0.25

LANES = 128
SUBLANES = 8
ROW_CHUNKS = D_MODEL // LANES
assert ROW_CHUNKS == SUBLANES

PROMPT_CHUNK = 512
SAMPLE_BATCH_BLOCK = 8
EXPERT_BLOCK = 512
DISPATCH_TILE = 512
COMBINE_TILE = 512
DMA_UNROLL = 8
N_CHUNKS = 4
VMEM_LIMIT_BYTES = 56 * 1024 * 1024

F32 = jnp.float32
BF16 = jnp.bfloat16


def _layer_norm(x, g, b):
    mu = jnp.mean(x, axis=-1, keepdims=True)
    xc = x - mu
    var = jnp.mean(xc * xc, axis=-1, keepdims=True)
    return xc * lax.rsqrt(var + LN_EPS) * g + b


def _load_rows(ref, n_rows):
    return jnp.concatenate(
        [ref[pl.ds(j, n_rows, stride=SUBLANES), :] for j in range(ROW_CHUNKS)], axis=1)


def _store_rows(ref, val, n_rows):
    for j in range(ROW_CHUNKS):
        ref[pl.ds(j, n_rows, stride=SUBLANES), :] = val[:, j * LANES:(j + 1) * LANES]


def _row_tile(r):
    if isinstance(r, int):
        return pl.ds(r * SUBLANES, SUBLANES)
    return pl.ds(pl.multiple_of(r * SUBLANES, SUBLANES), SUBLANES)


N_MIXER_INPUTS = 19
N_TOKEN_OUTPUTS = 4


def _mixer_body(*refs, nb, C, n_batch_steps, n_tail):
    if not n_tail:
        _mixer_compute(*refs, nb=nb, C=C)
        return
    tails = refs[N_MIXER_INPUTS:N_MIXER_INPUTS + N_TOKEN_OUTPUTS]
    outs = refs[N_MIXER_INPUTS + N_TOKEN_OUTPUTS:N_MIXER_INPUTS + 2 * N_TOKEN_OUTPUTS]
    bstep = pl.program_id(0)

    @pl.when(bstep < n_batch_steps)
    def _():
        _mixer_compute(*refs[:N_MIXER_INPUTS], *refs[N_MIXER_INPUTS + N_TOKEN_OUTPUTS:], nb=nb, C=C)

    @pl.when((bstep == n_batch_steps) & (pl.program_id(1) < n_tail))
    def _():
        for tail_ref, out_ref in zip(tails, outs):
            out_ref[...] = tail_ref[...]


def _route(logits_t, run_ref):
    n_tok = logits_t.shape[1]
    expert = lax.broadcasted_iota(jnp.int32, logits_t.shape, 0)
    vals, idxs, hots = [], [], []
    for _ in range(TOP_K):
        m = jnp.max(logits_t, axis=0, keepdims=True)
        idx = jnp.min(jnp.where(logits_t == m, expert, N_EXPERTS), axis=0, keepdims=True)
        hot = expert == idx
        vals.append(m)
        idxs.append(idx)
        hots.append(hot)
        logits_t = jnp.where(hot, -jnp.inf, logits_t)
    ex = [jnp.exp(v - vals[0]) for v in vals]
    total = ex[0]
    for e in ex[1:]:
        total = total + e
    gates = jnp.concatenate([e / total for e in ex], axis=0)

    chosen = hots[0].astype(F32)
    for hot in hots[1:]:
        chosen = chosen + hot.astype(F32)
    chosen = chosen.astype(BF16)
    t_row = lax.broadcasted_iota(jnp.int32, (n_tok, n_tok), 0)
    t_col = lax.broadcasted_iota(jnp.int32, (n_tok, n_tok), 1)
    earlier = jnp.where(t_row < t_col, 1.0, 0.0).astype(BF16)
    before = run_ref[...] + jnp.dot(chosen, earlier, preferred_element_type=F32)
    ranks = [jnp.sum(jnp.where(hot, before, 0.0), axis=0, keepdims=True) for hot in hots]
    run_ref[...] = run_ref[...] + jnp.dot(chosen, jnp.ones((n_tok, n_tok), BF16), preferred_element_type=F32)
    return jnp.concatenate(idxs, axis=0), gates, jnp.concatenate(ranks, axis=0).astype(jnp.int32)


def _mixer_compute(x_ref, cos_ref, sin_ref, decay_ref, qdec_ref, kdec_ref, sdec_ref, win_ref, wout_ref,
                   convw_ref, convb_ref, bret_ref, bconv_ref, g1_ref, b1_ref, wr_ref, br_ref,
                   sret_ref, sconv_ref, h_ref, choice_ref, gate_ref, rank_ref, count_ref, sret_out, sconv_out,
                   mix_ref, run_ref, *, nb, C):
    c = pl.program_id(1)

    @pl.when((pl.program_id(0) == 0) & (c == 0))
    def _():
        run_ref[...] = jnp.zeros(run_ref.shape, F32)

    @pl.when(c == 0)
    def _():
        sret_out[...] = sret_ref[...]
        sconv_out[...] = sconv_ref[...]

    x = x_ref[...].reshape(nb * C, D_MODEL)
    xb = x.astype(BF16)

    def proj(col0, width):
        return jnp.dot(xb, win_ref[:, col0:col0 + width], preferred_element_type=F32)

    q = proj(0, D_RET)
    k = proj(D_RET, D_RET)
    v = proj(2 * D_RET, D_RET)
    g = proj(3 * D_RET, D_RET)
    bg = proj(4 * D_RET, D_CONV)
    cg = proj(4 * D_RET + D_CONV, D_CONV)
    hc = proj(4 * D_RET + 2 * D_CONV, D_CONV)

    cos = cos_ref[...]
    sin = sin_ref[...]
    row = lax.broadcasted_iota(jnp.int32, (C, D_CONV), 0)
    k_scale = HEAD_DIM ** -0.5

    for b in range(nb):
        r0 = b * C
        for h in range(N_RET_HEADS):
            c0 = h * HEAD_DIM
            qh = q[r0:r0 + C, c0:c0 + HEAD_DIM]
            kh = k[r0:r0 + C, c0:c0 + HEAD_DIM]
            vh = v[r0:r0 + C, c0:c0 + HEAD_DIM]
            qh = qh * cos + pltpu.roll(qh, HEAD_DIM // 2, axis=1) * sin
            kh = (kh * cos + pltpu.roll(kh, HEAD_DIM // 2, axis=1) * sin) * k_scale
            qb = qh.astype(BF16)
            kb = kh.astype(BF16)
            vb = vh.astype(BF16)
            s_old = sret_out[b, h]
            scores = lax.dot_general(qb, kb, (((1,), (1,)), ((), ())), preferred_element_type=F32)
            scores = scores * decay_ref[h]
            intra = jnp.dot(scores.astype(BF16), vb, preferred_element_type=F32)
            cross = jnp.dot(qb, s_old.astype(BF16), preferred_element_type=F32) * qdec_ref[h]
            o = intra + cross
            kd = (kh * kdec_ref[h]).astype(BF16)
            s_new = sdec_ref[h] * s_old + lax.dot_general(
                kd, vb, (((0,), (0,)), ((), ())), preferred_element_type=F32)
            sret_out[b, h] = s_new
            mu = jnp.mean(o, axis=-1, keepdims=True)
            oc = o - mu
            var = jnp.mean(oc * oc, axis=-1, keepdims=True)
            on = oc * lax.rsqrt(var + GN_EPS)
            gh = g[r0:r0 + C, c0:c0 + HEAD_DIM]
            ret = on * (gh * jax.nn.sigmoid(gh)) * bret_ref[:, c0:c0 + HEAD_DIM]
            mix_ref[r0:r0 + C, c0:c0 + HEAD_DIM] = ret.astype(BF16)

        u = cg[r0:r0 + C] * hc[r0:r0 + C]
        prev = sconv_out[b]
        u1 = jnp.where(row == 0, prev[1:2], pltpu.roll(u, 1, axis=0))
        u2 = jnp.where(row == 0, prev[0:1], jnp.where(row == 1, prev[1:2], pltpu.roll(u, 2, axis=0)))
        z = convb_ref[...] + convw_ref[0:1] * u2 + convw_ref[1:2] * u1 + convw_ref[2:3] * u
        conv_out = bg[r0:r0 + C] * z * bconv_ref[...]
        mix_ref[r0:r0 + C, D_RET:D_RET + D_CONV] = conv_out.astype(BF16)
        sconv_out[b] = u[C - 2:C]

    m = jnp.dot(mix_ref[...], wout_ref[...], preferred_element_type=F32)
    hval = _layer_norm(DEEPNORM_ALPHA * x + m, g1_ref[...], b1_ref[...])
    _store_rows(h_ref, hval, nb * C)
    h_hi = hval.astype(BF16)
    h_lo = (hval - h_hi.astype(F32)).astype(BF16)
    nt_dot = lambda a, b: lax.dot_general(a, b, (((1,), (1,)), ((), ())), preferred_element_type=F32)
    logits_t = nt_dot(wr_ref[0], h_hi) + nt_dot(wr_ref[0], h_lo) + nt_dot(wr_ref[1], h_hi) + br_ref[...]
    choice_ref[...], gate_ref[...], rank_ref[...] = _route(logits_t, run_ref)
    count_ref[...] = run_ref[:, :LANES]


def _retention_tables(C, pos0, L):
    lg = jnp.log1p(-jnp.power(2.0, -5.0 - jnp.arange(N_RET_HEADS, dtype=F32)))
    i = jnp.arange(C, dtype=F32)
    diff = i[:, None] - i[None, :]
    decay = jnp.where(diff[None] >= 0, jnp.exp(lg[:, None, None] * jnp.maximum(diff, 0.0)[None]), 0.0)
    qdec = jnp.exp(lg[:, None] * (i + 1.0)[None, :])
    kdec = jnp.exp(lg[:, None] * (C - 1.0 - i)[None, :])
    sdec = jnp.exp(lg * C)
    qdec = jnp.broadcast_to(qdec[:, :, None], (N_RET_HEADS, C, HEAD_DIM))
    kdec = jnp.broadcast_to(kdec[:, :, None], (N_RET_HEADS, C, HEAD_DIM))
    sdec = jnp.broadcast_to(sdec[:, None, None], (N_RET_HEADS, 1, HEAD_DIM))
    pos = pos0 + jnp.arange(L, dtype=jnp.int32)
    inv_freq = ROPE_BASE ** (-jnp.arange(0, HEAD_DIM, 2, dtype=F32) / HEAD_DIM)
    ang = pos.astype(F32)[:, None] * inv_freq[None, :]
    cos = jnp.cos(ang)
    sin = jnp.sin(ang)
    cos_full = jnp.concatenate([cos, cos], axis=-1)
    sin_signed = jnp.concatenate([-sin, sin], axis=-1)
    return cos_full, sin_signed, decay, qdec, kdec, sdec


def _mixer(x, s_ret, s_conv, pos0, nb, C, w, tail=None):
    B, L, _ = x.shape
    nc = L // C
    assert nb == 1 or nc == 1
    nbs = B // nb
    rows = nb * C
    n_tokens = B * L
    n_tail = 0
    if tail is not None:
        n_tokens += tail[1].shape[1]
        n_tail = tail[1].shape[1] // rows
        assert tail[1].shape[1] % rows == 0 and 0 < n_tail <= nc
    cos, sin, decay, qdec, kdec, sdec = _retention_tables(C, pos0, L)

    bb = lambda b: jnp.minimum(b, nbs - 1)
    cc = lambda b, c: jnp.where(b < nbs, c, nc - 1)
    out_blk = lambda b, c: jnp.where(b < nbs, b * nc + c, nbs * nc + jnp.minimum(c, n_tail - 1))
    const2 = lambda b, c: (0, 0)
    const3 = lambda b, c: (0, 0, 0)
    full = lambda a: pl.BlockSpec(a.shape, const2 if a.ndim == 2 else const3)
    per_token = lambda imap: pl.BlockSpec((TOP_K, rows), lambda b, c: (0, imap(b, c)))
    in_arrays = [x, cos, sin, decay, qdec, kdec, sdec, w["w_in"], w["w_out"], w["conv_w"], w["conv_b"],
                 w["beta_ret"], w["beta_conv"], w["ln1_g"], w["ln1_b"], w["w_router"], w["b_router"],
                 s_ret, s_conv]
    assert len(in_arrays) == N_MIXER_INPUTS
    in_specs = [pl.BlockSpec((nb, C, D_MODEL), lambda b, c: (bb(b), cc(b, c), 0)),
                pl.BlockSpec((C, HEAD_DIM), lambda b, c: (cc(b, c), 0)),
                pl.BlockSpec((C, HEAD_DIM), lambda b, c: (cc(b, c), 0))]
    in_specs += [full(a) for a in in_arrays[3:17]]
    in_specs += [pl.BlockSpec((nb, N_RET_HEADS, HEAD_DIM, HEAD_DIM), lambda b, c: (bb(b), 0, 0, 0)),
                 pl.BlockSpec((nb, CONV_W - 1, D_CONV), lambda b, c: (bb(b), 0, 0))]
    if n_tail:
        tail_blk = lambda b, c: jnp.where(b < nbs, 0, jnp.minimum(c, n_tail - 1))
        in_arrays += list(tail)
        in_specs += [pl.BlockSpec((rows * SUBLANES, LANES), lambda b, c: (tail_blk(b, c), 0))]
        in_specs += [per_token(tail_blk)] * (N_TOKEN_OUTPUTS - 1)
    out_shape = [jax.ShapeDtypeStruct((n_tokens * SUBLANES, LANES), F32),
                 jax.ShapeDtypeStruct((TOP_K, n_tokens), jnp.int32),
                 jax.ShapeDtypeStruct((TOP_K, n_tokens), F32),
                 jax.ShapeDtypeStruct((TOP_K, n_tokens), jnp.int32),
                 jax.ShapeDtypeStruct((N_EXPERTS, LANES), F32),
                 jax.ShapeDtypeStruct((B, N_RET_HEADS, HEAD_DIM, HEAD_DIM), F32),
                 jax.ShapeDtypeStruct((B, CONV_W - 1, D_CONV), F32)]
    assert len(out_shape) == N_TOKEN_OUTPUTS + 3
    out_specs = [pl.BlockSpec((rows * SUBLANES, LANES), lambda b, c: (out_blk(b, c), 0))]
    out_specs += [per_token(out_blk)] * (N_TOKEN_OUTPUTS - 1)
    out_specs += [pl.BlockSpec((N_EXPERTS, LANES), const2),
                  pl.BlockSpec((nb, N_RET_HEADS, HEAD_DIM, HEAD_DIM), lambda b, c: (bb(b), 0, 0, 0)),
                  pl.BlockSpec((nb, CONV_W - 1, D_CONV), lambda b, c: (bb(b), 0, 0))]
    return pl.pallas_call(
        functools.partial(_mixer_body, nb=nb, C=C, n_batch_steps=nbs, n_tail=n_tail),
        grid=(nbs + (1 if n_tail else 0), nc), in_specs=in_specs, out_specs=out_specs, out_shape=out_shape,
        scratch_shapes=[pltpu.VMEM((rows, D_MODEL), BF16), pltpu.VMEM((N_EXPERTS, rows), F32)],
        compiler_params=pltpu.CompilerParams(
            dimension_semantics=("arbitrary", "arbitrary"), vmem_limit_bytes=VMEM_LIMIT_BYTES),
        name="mixer",
    )(*in_arrays)


def _lookup(table, choices):
    experts = jnp.arange(N_EXPERTS, dtype=jnp.int32)
    return jnp.sum(jnp.where(choices[..., None] == experts, table, 0), axis=-1)


def _routing_plan(choices_t, ranks_t, counts_first, counts_second, n_first, blk):
    T = choices_t.shape[1]
    A = T * TOP_K
    assert A % blk == 0
    n_blocks = A // blk + N_EXPERTS
    experts = jnp.arange(N_EXPERTS, dtype=jnp.int32)
    counts = counts_first + counts_second
    nblk_e = (counts + blk - 1) // blk
    bend = jnp.cumsum(nblk_e)
    n_valid = bend[-1]
    start = (bend - nblk_e) * blk
    token = jnp.arange(T, dtype=jnp.int32)[None, :]
    rank = ranks_t + jnp.where(token >= n_first, _lookup(counts_first, choices_t), 0)
    pos = _lookup(start, choices_t) + rank

    bi = jnp.arange(n_blocks, dtype=jnp.int32)
    block_e = jnp.sum((bi[:, None] >= bend[None, :]).astype(jnp.int32), axis=1)
    block_e = jnp.where(bi < n_valid, block_e, block_e[n_valid - 1])
    pad_dst = (A + block_e[:, None] * blk + jnp.arange(blk, dtype=jnp.int32)[None, :]).reshape(n_blocks * blk)
    last_block = jnp.where(nblk_e > 0, bend - 1, -1)
    unused = n_valid + experts
    unused = jnp.where(unused < n_blocks, unused, -1)
    zero_blocks = jnp.concatenate([last_block, unused]).astype(jnp.int32)
    return pos, pad_dst, block_e, n_valid.reshape(1).astype(jnp.int32), zero_blocks


def _dispatch_body(zero_ref, pos_ref, h_ref, pad_dst_hbm, xs_hbm, dst_hbm, hbuf, dst_tab, sem, tab_sem,
                   *, tt, blk, n_tokens):
    i = pl.program_id(0)
    n = pl.num_programs(0)
    slot = i % 2

    def wait_rows(s):
        for _ in range(TOP_K):
            pltpu.make_async_copy(hbuf.at[s], xs_hbm.at[pl.ds(0, tt * SUBLANES)], sem.at[s]).wait()

    @pl.when(i == 0)
    def _():
        load_table = pltpu.make_async_copy(pad_dst_hbm, dst_tab, tab_sem)
        load_table.start()
        load_table.wait()
        hbuf[1] = jnp.zeros(hbuf.shape[1:], F32)
        zeros = hbuf.at[1, pl.ds(0, blk * SUBLANES)]
        for e in range(zero_ref.shape[0]):
            @pl.when(zero_ref[e] >= 0)
            def _():
                first_row = pl.multiple_of(zero_ref[e] * (blk * SUBLANES), blk * SUBLANES)
                pltpu.make_async_copy(zeros, xs_hbm.at[pl.ds(first_row, blk * SUBLANES)], sem.at[1]).start()
        for e in range(zero_ref.shape[0]):
            @pl.when(zero_ref[e] >= 0)
            def _():
                pltpu.make_async_copy(zeros, xs_hbm.at[pl.ds(0, blk * SUBLANES)], sem.at[1]).wait()

    @pl.when(i >= 2)
    def _():
        wait_rows(slot)

    hbuf[slot] = h_ref[...]
    first_token = i * tt
    for t in range(tt):
        row = hbuf.at[slot, _row_tile(t)]
        for kk in range(TOP_K):
            p = pos_ref[kk, t]
            dst_tab[p] = first_token + (kk * n_tokens + t)
            pltpu.make_async_copy(row, xs_hbm.at[_row_tile(p)], sem.at[slot]).start(kk % 2)

    @pl.when(i == n - 1)
    def _():
        wait_rows(slot)

        @pl.when(i >= 1)
        def _():
            wait_rows(1 - slot)

        store_table = pltpu.make_async_copy(dst_tab, dst_hbm, tab_sem)
        store_table.start()
        store_table.wait()


def _dispatch(h_all, pos, pad_dst, zero_blocks, tt, blk):
    T = pos.shape[1]
    n_slots = pad_dst.shape[0]
    n_tiles = T // tt
    assert T % tt == 0 and blk <= tt
    grid_spec = pltpu.PrefetchScalarGridSpec(
        num_scalar_prefetch=1,
        grid=(n_tiles,),
        in_specs=[pl.BlockSpec((TOP_K, tt), lambda i, zb: (0, i), memory_space=pltpu.SMEM),
                  pl.BlockSpec((tt * SUBLANES, LANES), lambda i, zb: (i, 0)),
                  pl.BlockSpec(memory_space=pl.ANY)],
        out_specs=[pl.BlockSpec(memory_space=pl.ANY), pl.BlockSpec(memory_space=pl.ANY)],
        scratch_shapes=[pltpu.VMEM((2, tt * SUBLANES, LANES), F32),
                        pltpu.SMEM((n_slots,), jnp.int32),
                        pltpu.SemaphoreType.DMA((2,)),
                        pltpu.SemaphoreType.DMA],
    )
    return pl.pallas_call(
        functools.partial(_dispatch_body, tt=tt, blk=blk, n_tokens=T),
        grid_spec=grid_spec,
        out_shape=[jax.ShapeDtypeStruct((n_slots * SUBLANES, LANES), F32),
                   jax.ShapeDtypeStruct((n_slots,), jnp.int32)],
        compiler_params=pltpu.CompilerParams(
            dimension_semantics=("arbitrary",), vmem_limit_bytes=VMEM_LIMIT_BYTES),
        name="dispatch",
    )(zero_blocks, pos, h_all, pad_dst)


def _expert_body(be_ref, nv_ref, dst_ref, x_ref, wgu_ref, bgu_ref, wd_ref, bd_ref,
                 y_hbm, ybuf, osem, xb_ref, act_ref, *, blk, n_real_rows):
    i = pl.program_id(0)
    n_valid = nv_ref[0]

    def scatter_row(s, r, prio=0):
        pltpu.make_async_copy(
            ybuf.at[s, _row_tile(r)], y_hbm.at[_row_tile(dst_ref[0, 0, r])], osem.at[s]).start(prio)

    def wait_scatter(s):
        pltpu.make_async_copy(ybuf.at[s], y_hbm.at[pl.ds(0, blk * SUBLANES)], osem.at[s]).wait()

    @pl.when(i == 0)
    def _():
        ybuf[...] = jnp.zeros(ybuf.shape, F32)
        for e in range(N_EXPERTS):
            pltpu.make_async_copy(
                ybuf.at[0], y_hbm.at[pl.ds((n_real_rows + e * blk) * SUBLANES, blk * SUBLANES)],
                osem.at[0]).start()
        for e in range(N_EXPERTS):
            wait_scatter(0)

    rows_per_chunk = blk // N_CHUNKS
    cols = D_FF // N_CHUNKS

    def block_step(s):
        @pl.when(i < n_valid)
        def _():
            @pl.when(i >= 1)
            def _():
                wait_scatter(s)

            xb_ref[...] = _load_rows(x_ref, blk).astype(BF16)
            for c in range(N_CHUNKS):
                lo = c * cols
                xb = xb_ref[...]
                gate = jnp.dot(xb, wgu_ref[0, :, lo:lo + cols], preferred_element_type=F32)
                up = jnp.dot(xb, wgu_ref[0, :, D_FF + lo:D_FF + lo + cols], preferred_element_type=F32)
                gate = jnp.minimum(gate + bgu_ref[0, :, lo:lo + cols], SWIGLU_LIMIT)
                up = jnp.clip(up + bgu_ref[0, :, D_FF + lo:D_FF + lo + cols], -SWIGLU_LIMIT, SWIGLU_LIMIT)
                act = (up + 1.0) * gate * jax.nn.sigmoid(SWIGLU_ALPHA * gate)
                act_ref[:, lo:lo + cols] = act.astype(BF16)
                for r in range(c * rows_per_chunk, (c + 1) * rows_per_chunk):
                    scatter_row(1 - s, r, r % 2)
            y = jnp.dot(act_ref[...], wd_ref[0], preferred_element_type=F32) + bd_ref[0]
            _store_rows(ybuf.at[s], y, blk)

        @pl.when(i == n_valid)
        def _():
            wait_scatter(s)

            def body(gi, carry):
                for u in range(DMA_UNROLL):
                    scatter_row(1 - s, gi * DMA_UNROLL + u)
                return carry
            lax.fori_loop(0, blk // DMA_UNROLL, body, 0)
            wait_scatter(1 - s)

    for s in range(2):
        pl.when(i % 2 == s)(functools.partial(block_step, s))


def _experts(x_slots, dst, block_e, n_valid, w_gu, b_gu, w_down, b_down, blk, n_real_rows):
    n_blocks = block_e.shape[0]
    assert n_blocks * blk == n_real_rows + N_EXPERTS * blk and blk % (2 * N_CHUNKS) == 0
    n_out_rows = n_real_rows + (N_EXPERTS + 1) * blk
    priming = n_real_rows + N_EXPERTS * blk + jnp.arange(blk, dtype=jnp.int32)
    dst3 = jnp.concatenate([priming, dst]).reshape(n_blocks + 1, 1, blk)
    used = lambda i, nv: jnp.minimum(i, nv[0] - 1)
    grid_spec = pltpu.PrefetchScalarGridSpec(
        num_scalar_prefetch=2,
        grid=(n_blocks,),
        in_specs=[
            pl.BlockSpec((1, 1, blk), lambda i, be, nv: (i, 0, 0), memory_space=pltpu.SMEM),
            pl.BlockSpec((blk * SUBLANES, LANES), lambda i, be, nv: (used(i, nv), 0)),
            pl.BlockSpec((1, D_MODEL, 2 * D_FF), lambda i, be, nv: (be[i], 0, 0)),
            pl.BlockSpec((1, 1, 2 * D_FF), lambda i, be, nv: (be[i], 0, 0)),
            pl.BlockSpec((1, D_FF, D_MODEL), lambda i, be, nv: (be[i], 0, 0)),
            pl.BlockSpec((1, 1, D_MODEL), lambda i, be, nv: (be[i], 0, 0)),
        ],
        out_specs=pl.BlockSpec(memory_space=pl.ANY),
        scratch_shapes=[pltpu.VMEM((2, blk * SUBLANES, LANES), F32),
                        pltpu.SemaphoreType.DMA((2,)),
                        pltpu.VMEM((blk, D_MODEL), BF16),
                        pltpu.VMEM((blk, D_FF), BF16)],
    )
    return pl.pallas_call(
        functools.partial(_expert_body, blk=blk, n_real_rows=n_real_rows),
        grid_spec=grid_spec,
        out_shape=jax.ShapeDtypeStruct((n_out_rows * SUBLANES, LANES), F32),
        compiler_params=pltpu.CompilerParams(
            dimension_semantics=("arbitrary",), vmem_limit_bytes=VMEM_LIMIT_BYTES),
        name="experts",
    )(block_e, n_valid, dst3, x_slots, w_gu, b_gu.reshape(N_EXPERTS, 1, 2 * D_FF),
      w_down, b_down.reshape(N_EXPERTS, 1, D_MODEL))


def _combine_body(*refs, tt, n_first):
    y_refs = refs[:TOP_K]
    h_ref, gate_ref, g2_ref, b2_ref, out_first, out_second = refs[TOP_K:]
    i = pl.program_id(0)
    gates = gate_ref[...]
    f = gates[:, 0:1] * _load_rows(y_refs[0], tt)
    for kk in range(1, TOP_K):
        f = f + gates[:, kk:kk + 1] * _load_rows(y_refs[kk], tt)
    out = _layer_norm(DEEPNORM_ALPHA * _load_rows(h_ref, tt) + f, g2_ref[...], b2_ref[...])

    @pl.when(i < n_first)
    def _():
        out_first[...] = out

    @pl.when(i >= n_first)
    def _():
        out_second[...] = out


def _combine(y_rows, gates, h_all, ln2_g, ln2_b, tt, n_tokens_first):
    T = gates.shape[0]
    n_tiles = T // tt
    n_first = n_tokens_first // tt
    assert T % tt == 0 and n_tokens_first % tt == 0 and 0 < n_first < n_tiles
    row_blk = lambda imap: pl.BlockSpec((tt * SUBLANES, LANES), imap)
    y_specs = [row_blk(functools.partial(lambda i, kk: (kk * n_tiles + i, 0), kk=kk)) for kk in range(TOP_K)]
    return pl.pallas_call(
        functools.partial(_combine_body, tt=tt, n_first=n_first),
        grid=(n_tiles,),
        in_specs=y_specs + [
            row_blk(lambda i: (i, 0)),
            pl.BlockSpec((tt, TOP_K), lambda i: (i, 0)),
            pl.BlockSpec((1, D_MODEL), lambda i: (0, 0)),
            pl.BlockSpec((1, D_MODEL), lambda i: (0, 0)),
        ],
        out_specs=[pl.BlockSpec((tt, D_MODEL), lambda i: (jnp.minimum(i, n_first - 1), 0)),
                   pl.BlockSpec((tt, D_MODEL), lambda i: (jnp.maximum(i - n_first, 0), 0))],
        out_shape=[jax.ShapeDtypeStruct((n_tokens_first, D_MODEL), F32),
                   jax.ShapeDtypeStruct((T - n_tokens_first, D_MODEL), F32)],
        compiler_params=pltpu.CompilerParams(
            dimension_semantics=("arbitrary",), vmem_limit_bytes=VMEM_LIMIT_BYTES),
        name="combine",
    )(*([y_rows] * TOP_K), h_all, gates, ln2_g, ln2_b)


def kernel(x_prompt, x_sample, state_retention, state_conv, w_in, beta_ret, beta_conv, conv_w, conv_b,
           w_out, ln1_g, ln1_b, w_router, b_router, w_gu, b_gu, w_down, b_down, ln2_g, ln2_b):
    assert w_in.shape[0] == DEPTH == 1
    B, S, _ = x_prompt.shape
    Bd, Ld, _ = x_sample.shape
    Tp, Ts = B * S, Bd * Ld
    T = Tp + Ts
    wr_t = w_router[0].T
    wr_hi = wr_t.astype(BF16)
    wr_lo = (wr_t - wr_hi.astype(F32)).astype(BF16)
    w = dict(w_in=w_in[0].astype(BF16), w_out=w_out[0].astype(BF16), conv_w=conv_w[0],
             conv_b=conv_b[0][None], beta_ret=beta_ret[0][None], beta_conv=beta_conv[0][None],
             ln1_g=ln1_g[0][None], ln1_b=ln1_b[0][None], w_router=jnp.stack([wr_hi, wr_lo]),
             b_router=b_router[0][:, None])

    s_ret0 = jnp.zeros((B, N_RET_HEADS, HEAD_DIM, HEAD_DIM), F32)
    s_conv0 = jnp.zeros((B, CONV_W - 1, D_CONV), F32)
    h_s, choices_s, gates_s, ranks_s, counts_s, sret_s, sconv_s = _mixer(
        x_sample, state_retention[0], state_conv[0], PAST_LEN, SAMPLE_BATCH_BLOCK, Ld, w)
    h_all, choices_t, gates_t, ranks_t, counts_p, sret_p, sconv_p = _mixer(
        x_prompt, s_ret0, s_conv0, 0, 1, PROMPT_CHUNK, w, tail=(h_s, choices_s, gates_s, ranks_s))

    pos, pad_dst, block_e, n_valid, zero_blocks = _routing_plan(
        choices_t, ranks_t, counts_p[:, 0].astype(jnp.int32), counts_s[:, 0].astype(jnp.int32), Tp,
        EXPERT_BLOCK)
    x_slots, dst = _dispatch(h_all, pos, pad_dst, zero_blocks, DISPATCH_TILE, EXPERT_BLOCK)
    y_rows = _experts(x_slots, dst, block_e, n_valid, w_gu[0].astype(BF16), b_gu[0], w_down[0].astype(BF16), b_down[0],
                      EXPERT_BLOCK, T * TOP_K)
    y_p, y_s = _combine(y_rows, gates_t.T, h_all, ln2_g[0][None], ln2_b[0][None], COMBINE_TILE, Tp)

    return (y_p.reshape(B, S, D_MODEL), y_s.reshape(Bd, Ld, D_MODEL),
            sret_p[None], sconv_p[None], sret_s[None], sconv_s[None])
```

```python
import functools

import jax
import jax.numpy as jnp
from jax import lax
from jax.experimental import pallas as pl
from jax.experimental.pallas import tpu as pltpu

D_MODEL = 1024
N_RET_HEADS = 4
HEAD_DIM = 128
D_RET = N_RET_HEADS * HEAD_DIM
D_CONV = D_MODEL - D_RET
CONV_W = 3
N_EXPERTS = 32
TOP_K = 4
D_FF = D_MODEL
SWIGLU_LIMIT = 7.0
SWIGLU_ALPHA = 1.702
ROPE_BASE = 10000.0
LN_EPS = 1e-5
GN_EPS = 1e-5
DEPTH = 1
PAST_LEN = 1024
DEEPNORM_ALPHA = (2.0 * DEPTH) ** 0.25

LANES = 128
SUBLANES = 8
ROW_CHUNKS = D_MODEL // LANES
assert ROW_CHUNKS == SUBLANES

PROMPT_CHUNK = 512
SAMPLE_BATCH_BLOCK = 8
EXPERT_BLOCK = 512
DISPATCH_TILE = 512
COMBINE_TILE = 512
DMA_UNROLL = 8
N_CHUNKS = 4
VMEM_LIMIT_BYTES = 56 * 1024 * 1024

F32 = jnp.float32
BF16 = jnp.bfloat16


def _layer_norm(x, g, b):
    mu = jnp.mean(x, axis=-1, keepdims=True)
    xc = x - mu
    var = jnp.mean(xc * xc, axis=-1, keepdims=True)
    return xc * lax.rsqrt(var + LN_EPS) * g + b


def _load_rows(ref, n_rows):
    return jnp.concatenate(
        [ref[pl.ds(j, n_rows, stride=SUBLANES), :] for j in range(ROW_CHUNKS)], axis=1)


def _store_rows(ref, val, n_rows):
    for j in range(ROW_CHUNKS):
        ref[pl.ds(j, n_rows, stride=SUBLANES), :] = val[:, j * LANES:(j + 1) * LANES]


def _row_tile(r):
    if isinstance(r, int):
        return pl.ds(r * SUBLANES, SUBLANES)
    return pl.ds(pl.multiple_of(r * SUBLANES, SUBLANES), SUBLANES)


N_MIXER_INPUTS = 19
N_TOKEN_OUTPUTS = 4


def _mixer_body(*refs, nb, C, n_batch_steps, n_tail):
    if not n_tail:
        _mixer_compute(*refs, nb=nb, C=C)
        return
    tails = refs[N_MIXER_INPUTS:N_MIXER_INPUTS + N_TOKEN_OUTPUTS]
    outs = refs[N_MIXER_INPUTS + N_TOKEN_OUTPUTS:N_MIXER_INPUTS + 2 * N_TOKEN_OUTPUTS]
    bstep = pl.program_id(0)

    @pl.when(bstep < n_batch_steps)
    def _():
        _mixer_compute(*refs[:N_MIXER_INPUTS], *refs[N_MIXER_INPUTS + N_TOKEN_OUTPUTS:], nb=nb, C=C)

    @pl.when((bstep == n_batch_steps) & (pl.program_id(1) < n_tail))
    def _():
        for tail_ref, out_ref in zip(tails, outs):
            out_ref[...] = tail_ref[...]


def _route(logits_t, run_ref):
    n_tok = logits_t.shape[1]
    expert = lax.broadcasted_iota(jnp.int32, logits_t.shape, 0)
    vals, idxs, hots = [], [], []
    for _ in range(TOP_K):
        m = jnp.max(logits_t, axis=0, keepdims=True)
        idx = jnp.min(jnp.where(logits_t == m, expert, N_EXPERTS), axis=0, keepdims=True)
        hot = expert == idx
        vals.append(m)
        idxs.append(idx)
        hots.append(hot)
        logits_t = jnp.where(hot, -jnp.inf, logits_t)
    ex = [jnp.exp(v - vals[0]) for v in vals]
    total = ex[0]
    for e in ex[1:]:
        total = total + e
    gates = jnp.concatenate([e / total for e in ex], axis=0)

    chosen = hots[0].astype(F32)
    for hot in hots[1:]:
        chosen = chosen + hot.astype(F32)
    chosen = chosen.astype(BF16)
    t_row = lax.broadcasted_iota(jnp.int32, (n_tok, n_tok), 0)
    t_col = lax.broadcasted_iota(jnp.int32, (n_tok, n_tok), 1)
    earlier = jnp.where(t_row < t_col, 1.0, 0.0).astype(BF16)
    before = run_ref[...] + jnp.dot(chosen, earlier, preferred_element_type=F32)
    ranks = [jnp.sum(jnp.where(hot, before, 0.0), axis=0, keepdims=True) for hot in hots]
    run_ref[...] = run_ref[...] + jnp.dot(chosen, jnp.ones((n_tok, n_tok), BF16), preferred_element_type=F32)
    return jnp.concatenate(idxs, axis=0), gates, jnp.concatenate(ranks, axis=0).astype(jnp.int32)


def _mixer_compute(x_ref, cos_ref, sin_ref, decay_ref, qdec_ref, kdec_ref, sdec_ref, win_ref, wout_ref,
                   convw_ref, convb_ref, bret_ref, bconv_ref, g1_ref, b1_ref, wr_ref, br_ref,
                   sret_ref, sconv_ref, h_ref, choice_ref, gate_ref, rank_ref, count_ref, sret_out, sconv_out,
                   mix_ref, run_ref, *, nb, C):
    c = pl.program_id(1)

    @pl.when((pl.program_id(0) == 0) & (c == 0))
    def _():
        run_ref[...] = jnp.zeros(run_ref.shape, F32)

    @pl.when(c == 0)
    def _():
        sret_out[...] = sret_ref[...]
        sconv_out[...] = sconv_ref[...]

    x = x_ref[...].reshape(nb * C, D_MODEL)
    xb = x.astype(BF16)

    def proj(col0, width):
        return jnp.dot(xb, win_ref[:, col0:col0 + width], preferred_element_type=F32)

    q = proj(0, D_RET)
    k = proj(D_RET, D_RET)
    v = proj(2 * D_RET, D_RET)
    g = proj(3 * D_RET, D_RET)
    bg = proj(4 * D_RET, D_CONV)
    cg = proj(4 * D_RET + D_CONV, D_CONV)
    hc = proj(4 * D_RET + 2 * D_CONV, D_CONV)

    cos = cos_ref[...]
    sin = sin_ref[...]
    row = lax.broadcasted_iota(jnp.int32, (C, D_CONV), 0)
    k_scale = HEAD_DIM ** -0.5

    for b in range(nb):
        r0 = b * C
        for h in range(N_RET_HEADS):
            c0 = h * HEAD_DIM
            qh = q[r0:r0 + C, c0:c0 + HEAD_DIM]
            kh = k[r0:r0 + C, c0:c0 + HEAD_DIM]
            vh = v[r0:r0 + C, c0:c0 + HEAD_DIM]
            qh = qh * cos + pltpu.roll(qh, HEAD_DIM // 2, axis=1) * sin
            kh = (kh * cos + pltpu.roll(kh, HEAD_DIM // 2, axis=1) * sin) * k_scale
            qb = qh.astype(BF16)
            kb = kh.astype(BF16)
            vb = vh.astype(BF16)
            s_old = sret_out[b, h]
            scores = lax.dot_general(qb, kb, (((1,), (1,)), ((), ())), preferred_element_type=F32)
            scores = scores * decay_ref[h]
            intra = jnp.dot(scores.astype(BF16), vb, preferred_element_type=F32)
            cross = jnp.dot(qb, s_old.astype(BF16), preferred_element_type=F32) * qdec_ref[h]
            o = intra + cross
            kd = (kh * kdec_ref[h]).astype(BF16)
            s_new = sdec_ref[h] * s_old + lax.dot_general(
                kd, vb, (((0,), (0,)), ((), ())), preferred_element_type=F32)
            sret_out[b, h] = s_new
            mu = jnp.mean(o, axis=-1, keepdims=True)
            oc = o - mu
            var = jnp.mean(oc * oc, axis=-1, keepdims=True)
            on = oc * lax.rsqrt(var + GN_EPS)
            gh = g[r0:r0 + C, c0:c0 + HEAD_DIM]
            ret = on * (gh * jax.nn.sigmoid(gh)) * bret_ref[:, c0:c0 + HEAD_DIM]
            mix_ref[r0:r0 + C, c0:c0 + HEAD_DIM] = ret.astype(BF16)

        u = cg[r0:r0 + C] * hc[r0:r0 + C]
        prev = sconv_out[b]
        u1 = jnp.where(row == 0, prev[1:2], pltpu.roll(u, 1, axis=0))
        u2 = jnp.where(row == 0, prev[0:1], jnp.where(row == 1, prev[1:2], pltpu.roll(u, 2, axis=0)))
        z = convb_ref[...] + convw_ref[0:1] * u2 + convw_ref[1:2] * u1 + convw_ref[2:3] * u
        conv_out = bg[r0:r0 + C] * z * bconv_ref[...]
        mix_ref[r0:r0 + C, D_RET:D_RET + D_CONV] = conv_out.astype(BF16)
        sconv_out[b] = u[C - 2:C]

    m = jnp.dot(mix_ref[...], wout_ref[...], preferred_element_type=F32)
    hval = _layer_norm(DEEPNORM_ALPHA * x + m, g1_ref[...], b1_ref[...])
    _store_rows(h_ref, hval, nb * C)
    h_hi = hval.astype(BF16)
    h_lo = (hval - h_hi.astype(F32)).astype(BF16)
    nt_dot = lambda a, b: lax.dot_general(a, b, (((1,), (1,)), ((), ())), preferred_element_type=F32)
    logits_t = nt_dot(wr_ref[0], h_hi) + nt_dot(wr_ref[0], h_lo) + nt_dot(wr_ref[1], h_hi) + br_ref[...]
    choice_ref[...], gate_ref[...], rank_ref[...] = _route(logits_t, run_ref)
    count_ref[...] = run_ref[:, :LANES]


def _retention_tables(C, pos0, L):
    lg = jnp.log1p(-jnp.power(2.0, -5.0 - jnp.arange(N_RET_HEADS, dtype=F32)))
    i = jnp.arange(C, dtype=F32)
    diff = i[:, None] - i[None, :]
    decay = jnp.where(diff[None] >= 0, jnp.exp(lg[:, None, None] * jnp.maximum(diff, 0.0)[None]), 0.0)
    qdec = jnp.exp(lg[:, None] * (i + 1.0)[None, :])
    kdec = jnp.exp(lg[:, None] * (C - 1.0 - i)[None, :])
    sdec = jnp.exp(lg * C)
    qdec = jnp.broadcast_to(qdec[:, :, None], (N_RET_HEADS, C, HEAD_DIM))
    kdec = jnp.broadcast_to(kdec[:, :, None], (N_RET_HEADS, C, HEAD_DIM))
    sdec = jnp.broadcast_to(sdec[:, None, None], (N_RET_HEADS, 1, HEAD_DIM))
    pos = pos0 + jnp.arange(L, dtype=jnp.int32)
    inv_freq = ROPE_BASE ** (-jnp.arange(0, HEAD_DIM, 2, dtype=F32) / HEAD_DIM)
    ang = pos.astype(F32)[:, None] * inv_freq[None, :]
    cos = jnp.cos(ang)
    sin = jnp.sin(ang)
    cos_full = jnp.concatenate([cos, cos], axis=-1)
    sin_signed = jnp.concatenate([-sin, sin], axis=-1)
    return cos_full, sin_signed, decay, qdec, kdec, sdec


def _mixer(x, s_ret, s_conv, pos0, nb, C, w, tail=None):
    B, L, _ = x.shape
    nc = L // C
    assert nb == 1 or nc == 1
    nbs = B // nb
    rows = nb * C
    n_tokens = B * L
    n_tail = 0
    if tail is not None:
        n_tokens += tail[1].shape[1]
        n_tail = tail[1].shape[1] // rows
        assert tail[1].shape[1] % rows == 0 and 0 < n_tail <= nc
    cos, sin, decay, qdec, kdec, sdec = _retention_tables(C, pos0, L)

    bb = lambda b: jnp.minimum(b, nbs - 1)
    cc = lambda b, c: jnp.where(b < nbs, c, nc - 1)
    out_blk = lambda b, c: jnp.where(b < nbs, b * nc + c, nbs * nc + jnp.minimum(c, n_tail - 1))
    const2 = lambda b, c: (0, 0)
    const3 = lambda b, c: (0, 0, 0)
    full = lambda a: pl.BlockSpec(a.shape, const2 if a.ndim == 2 else const3)
    per_token = lambda imap: pl.BlockSpec((TOP_K, rows), lambda b, c: (0, imap(b, c)))
    in_arrays = [x, cos, sin, decay, qdec, kdec, sdec, w["w_in"], w["w_out"], w["conv_w"], w["conv_b"],
                 w["beta_ret"], w["beta_conv"], w["ln1_g"], w["ln1_b"], w["w_router"], w["b_router"],
                 s_ret, s_conv]
    assert len(in_arrays) == N_MIXER_INPUTS
    in_specs = [pl.BlockSpec((nb, C, D_MODEL), lambda b, c: (bb(b), cc(b, c), 0)),
                pl.BlockSpec((C, HEAD_DIM), lambda b, c: (cc(b, c), 0)),
                pl.BlockSpec((C, HEAD_DIM), lambda b, c: (cc(b, c), 0))]
    in_specs += [full(a) for a in in_arrays[3:17]]
    in_specs += [pl.BlockSpec((nb, N_RET_HEADS, HEAD_DIM, HEAD_DIM), lambda b, c: (bb(b), 0, 0, 0)),
                 pl.BlockSpec((nb, CONV_W - 1, D_CONV), lambda b, c: (bb(b), 0, 0))]
    if n_tail:
        tail_blk = lambda b, c: jnp.where(b < nbs, 0, jnp.minimum(c, n_tail - 1))
        in_arrays += list(tail)
        in_specs += [pl.BlockSpec((rows * SUBLANES, LANES), lambda b, c: (tail_blk(b, c), 0))]
        in_specs += [per_token(tail_blk)] * (N_TOKEN_OUTPUTS - 1)
    out_shape = [jax.ShapeDtypeStruct((n_tokens * SUBLANES, LANES), F32),
                 jax.ShapeDtypeStruct((TOP_K, n_tokens), jnp.int32),
                 jax.ShapeDtypeStruct((TOP_K, n_tokens), F32),
                 jax.ShapeDtypeStruct((TOP_K, n_tokens), jnp.int32),
                 jax.ShapeDtypeStruct((N_EXPERTS, LANES), F32),
                 jax.ShapeDtypeStruct((B, N_RET_HEADS, HEAD_DIM, HEAD_DIM), F32),
                 jax.ShapeDtypeStruct((B, CONV_W - 1, D_CONV), F32)]
    assert len(out_shape) == N_TOKEN_OUTPUTS + 3
    out_specs = [pl.BlockSpec((rows * SUBLANES, LANES), lambda b, c: (out_blk(b, c), 0))]
    out_specs += [per_token(out_blk)] * (N_TOKEN_OUTPUTS - 1)
    out_specs += [pl.BlockSpec((N_EXPERTS, LANES), const2),
                  pl.BlockSpec((nb, N_RET_HEADS, HEAD_DIM, HEAD_DIM), lambda b, c: (bb(b), 0, 0, 0)),
                  pl.BlockSpec((nb, CONV_W - 1, D_CONV), lambda b, c: (bb(b), 0, 0))]
    return pl.pallas_call(
        functools.partial(_mixer_body, nb=nb, C=C, n_batch_steps=nbs, n_tail=n_tail),
        grid=(nbs + (1 if n_tail else 0), nc), in_specs=in_specs, out_specs=out_specs, out_shape=out_shape,
        scratch_shapes=[pltpu.VMEM((rows, D_MODEL), BF16), pltpu.VMEM((N_EXPERTS, rows), F32)],
        compiler_params=pltpu.CompilerParams(
            dimension_semantics=("arbitrary", "arbitrary"), vmem_limit_bytes=VMEM_LIMIT_BYTES),
        name="mixer",
    )(*in_arrays)


def _lookup(table, choices):
    experts = jnp.arange(N_EXPERTS, dtype=jnp.int32)
    return jnp.sum(jnp.where(choices[..., None] == experts, table, 0), axis=-1)


def _routing_plan(choices_t, ranks_t, counts_first, counts_second, n_first, blk):
    T = choices_t.shape[1]
    A = T * TOP_K
    assert A % blk == 0
    n_blocks = A // blk + N_EXPERTS
    experts = jnp.arange(N_EXPERTS, dtype=jnp.int32)
    counts = counts_first + counts_second
    nblk_e = (counts + blk - 1) // blk
    bend = jnp.cumsum(nblk_e)
    n_valid = bend[-1]
    start = (bend - nblk_e) * blk
    token = jnp.arange(T, dtype=jnp.int32)[None, :]
    rank = ranks_t + jnp.where(token >= n_first, _lookup(counts_first, choices_t), 0)
    pos = _lookup(start, choices_t) + rank

    bi = jnp.arange(n_blocks, dtype=jnp.int32)
    block_e = jnp.sum((bi[:, None] >= bend[None, :]).astype(jnp.int32), axis=1)
    block_e = jnp.where(bi < n_valid, block_e, block_e[n_valid - 1])
    pad_dst = (A + block_e[:, None] * blk + jnp.arange(blk, dtype=jnp.int32)[None, :]).reshape(n_blocks * blk)
    last_block = jnp.where(nblk_e > 0, bend - 1, -1)
    unused = n_valid + experts
    unused = jnp.where(unused < n_blocks, unused, -1)
    zero_blocks = jnp.concatenate([last_block, unused]).astype(jnp.int32)
    return pos, pad_dst, block_e, n_valid.reshape(1).astype(jnp.int32), zero_blocks


def _dispatch_body(zero_ref, pos_ref, h_ref, pad_dst_hbm, xs_hbm, dst_hbm, hbuf, dst_tab, sem, tab_sem,
                   *, tt, blk, n_tokens):
    i = pl.program_id(0)
    n = pl.num_programs(0)
    slot = i % 2

    def wait_rows(s):
        for _ in range(TOP_K):
            pltpu.make_async_copy(hbuf.at[s], xs_hbm.at[pl.ds(0, tt * SUBLANES)], sem.at[s]).wait()

    @pl.when(i == 0)
    def _():
        load_table = pltpu.make_async_copy(pad_dst_hbm, dst_tab, tab_sem)
        load_table.start()
        load_table.wait()
        hbuf[1] = jnp.zeros(hbuf.shape[1:], F32)
        zeros = hbuf.at[1, pl.ds(0, blk * SUBLANES)]
        for e in range(zero_ref.shape[0]):
            @pl.when(zero_ref[e] >= 0)
            def _():
                first_row = pl.multiple_of(zero_ref[e] * (blk * SUBLANES), blk * SUBLANES)
                pltpu.make_async_copy(zeros, xs_hbm.at[pl.ds(first_row, blk * SUBLANES)], sem.at[1]).start()
        for e in range(zero_ref.shape[0]):
            @pl.when(zero_ref[e] >= 0)
            def _():
                pltpu.make_async_copy(zeros, xs_hbm.at[pl.ds(0, blk * SUBLANES)], sem.at[1]).wait()

    @pl.when(i >= 2)
    def _():
        wait_rows(slot)

    hbuf[slot] = h_ref[...]
    first_token = i * tt
    for t in range(tt):
        row = hbuf.at[slot, _row_tile(t)]
        for kk in range(TOP_K):
            p = pos_ref[kk, t]
            dst_tab[p] = first_token + (kk * n_tokens + t)
            pltpu.make_async_copy(row, xs_hbm.at[_row_tile(p)], sem.at[slot]).start(kk % 2)

    @pl.when(i == n - 1)
    def _():
        wait_rows(slot)

        @pl.when(i >= 1)
        def _():
            wait_rows(1 - slot)

        store_table = pltpu.make_async_copy(dst_tab, dst_hbm, tab_sem)
        store_table.start()
        store_table.wait()


def _dispatch(h_all, pos, pad_dst, zero_blocks, tt, blk):
    T = pos.shape[1]
    n_slots = pad_dst.shape[0]
    n_tiles = T // tt
    assert T % tt == 0 and blk <= tt
    grid_spec = pltpu.PrefetchScalarGridSpec(
        num_scalar_prefetch=1,
        grid=(n_tiles,),
        in_specs=[pl.BlockSpec((TOP_K, tt), lambda i, zb: (0, i), memory_space=pltpu.SMEM),
                  pl.BlockSpec((tt * SUBLANES, LANES), lambda i, zb: (i, 0)),
                  pl.BlockSpec(memory_space=pl.ANY)],
        out_specs=[pl.BlockSpec(memory_space=pl.ANY), pl.BlockSpec(memory_space=pl.ANY)],
        scratch_shapes=[pltpu.VMEM((2, tt * SUBLANES, LANES), F32),
                        pltpu.SMEM((n_slots,), jnp.int32),
                        pltpu.SemaphoreType.DMA((2,)),
                        pltpu.SemaphoreType.DMA],
    )
    return pl.pallas_call(
        functools.partial(_dispatch_body, tt=tt, blk=blk, n_tokens=T),
        grid_spec=grid_spec,
        out_shape=[jax.ShapeDtypeStruct((n_slots * SUBLANES, LANES), F32),
                   jax.ShapeDtypeStruct((n_slots,), jnp.int32)],
        compiler_params=pltpu.CompilerParams(
            dimension_semantics=("arbitrary",), vmem_limit_bytes=VMEM_LIMIT_BYTES),
        name="dispatch",
    )(zero_blocks, pos, h_all, pad_dst)


def _expert_body(be_ref, nv_ref, dst_ref, x_ref, wgu_ref, bgu_ref, wd_ref, bd_ref,
                 y_hbm, ybuf, osem, wgu_bf, wd_bf, xb_ref, act_ref, *, blk, n_real_rows):
    i = pl.program_id(0)
    n_valid = nv_ref[0]

    def scatter_row(s, r, prio=0):
        pltpu.make_async_copy(
            ybuf.at[s, _row_tile(r)], y_hbm.at[_row_tile(dst_ref[0, 0, r])], osem.at[s]).start(prio)

    def wait_scatter(s):
        pltpu.make_async_copy(ybuf.at[s], y_hbm.at[pl.ds(0, blk * SUBLANES)], osem.at[s]).wait()

    @pl.when(i == 0)
    def _():
        ybuf[...] = jnp.zeros(ybuf.shape, F32)
        for e in range(N_EXPERTS):
            pltpu.make_async_copy(
                ybuf.at[0], y_hbm.at[pl.ds((n_real_rows + e * blk) * SUBLANES, blk * SUBLANES)],
                osem.at[0]).start()
        for e in range(N_EXPERTS):
            wait_scatter(0)

    @pl.when((i < n_valid) & ((i == 0) | (be_ref[i] != be_ref[jnp.maximum(i - 1, 0)])))
    def _():
        wgu_bf[...] = wgu_ref[0].astype(BF16)
        wd_bf[...] = wd_ref[0].astype(BF16)

    rows_per_chunk = blk // N_CHUNKS
    cols = D_FF // N_CHUNKS

    def block_step(s):
        @pl.when(i < n_valid)
        def _():
            @pl.when(i >= 1)
            def _():
                wait_scatter(s)

            xb_ref[...] = _load_rows(x_ref, blk).astype(BF16)
            for c in range(N_CHUNKS):
                lo = c * cols
                xb = xb_ref[...]
                gate = jnp.dot(xb, wgu_bf[:, lo:lo + cols], preferred_element_type=F32)
                up = jnp.dot(xb, wgu_bf[:, D_FF + lo:D_FF + lo + cols], preferred_element_type=F32)
                gate = jnp.minimum(gate + bgu_ref[be_ref[i], :, lo:lo + cols], SWIGLU_LIMIT)
                up = jnp.clip(up + bgu_ref[be_ref[i], :, D_FF + lo:D_FF + lo + cols], -SWIGLU_LIMIT, SWIGLU_LIMIT)
                act = (up + 1.0) * gate * jax.nn.sigmoid(SWIGLU_ALPHA * gate)
                act_ref[:, lo:lo + cols] = act.astype(BF16)
                for r in range(c * rows_per_chunk, (c + 1) * rows_per_chunk):
                    scatter_row(1 - s, r, r % 2)
            y = jnp.dot(act_ref[...], wd_bf[...], preferred_element_type=F32) + bd_ref[be_ref[i]]
            _store_rows(ybuf.at[s], y, blk)

        @pl.when(i == n_valid)
        def _():
            wait_scatter(s)

            def body(gi, carry):
                for u in range(DMA_UNROLL):
                    scatter_row(1 - s, gi * DMA_UNROLL + u)
                return carry
            lax.fori_loop(0, blk // DMA_UNROLL, body, 0)
            wait_scatter(1 - s)

    for s in range(2):
        pl.when(i % 2 == s)(functools.partial(block_step, s))


def _experts(x_slots, dst, block_e, n_valid, w_gu, b_gu, w_down, b_down, blk, n_real_rows):
    n_blocks = block_e.shape[0]
    assert n_blocks * blk == n_real_rows + N_EXPERTS * blk and blk % (2 * N_CHUNKS) == 0
    n_out_rows = n_real_rows + (N_EXPERTS + 1) * blk
    priming = n_real_rows + N_EXPERTS * blk + jnp.arange(blk, dtype=jnp.int32)
    dst3 = jnp.concatenate([priming, dst]).reshape(n_blocks + 1, 1, blk)
    used = lambda i, nv: jnp.minimum(i, nv[0] - 1)
    grid_spec = pltpu.PrefetchScalarGridSpec(
        num_scalar_prefetch=2,
        grid=(n_blocks,),
        in_specs=[
            pl.BlockSpec((1, 1, blk), lambda i, be, nv: (i, 0, 0), memory_space=pltpu.SMEM),
            pl.BlockSpec((blk * SUBLANES, LANES), lambda i, be, nv: (used(i, nv), 0)),
            pl.BlockSpec((1, D_MODEL, 2 * D_FF), lambda i, be, nv: (be[i], 0, 0)),
            pl.BlockSpec((N_EXPERTS, 1, 2 * D_FF), lambda i, be, nv: (0, 0, 0)),
            pl.BlockSpec((1, D_FF, D_MODEL), lambda i, be, nv: (be[i], 0, 0)),
            pl.BlockSpec((N_EXPERTS, 1, D_MODEL), lambda i, be, nv: (0, 0, 0)),
        ],
        out_specs=pl.BlockSpec(memory_space=pl.ANY),
        scratch_shapes=[pltpu.VMEM((2, blk * SUBLANES, LANES), F32),
                        pltpu.SemaphoreType.DMA((2,)),
                        pltpu.VMEM((D_MODEL, 2 * D_FF), BF16),
                        pltpu.VMEM((D_FF, D_MODEL), BF16),
                        pltpu.VMEM((blk, D_MODEL), BF16),
                        pltpu.VMEM((blk, D_FF), BF16)],
    )
    return pl.pallas_call(
        functools.partial(_expert_body, blk=blk, n_real_rows=n_real_rows),
        grid_spec=grid_spec,
        out_shape=jax.ShapeDtypeStruct((n_out_rows * SUBLANES, LANES), F32),
        compiler_params=pltpu.CompilerParams(
            dimension_semantics=("arbitrary",), vmem_limit_bytes=VMEM_LIMIT_BYTES),
        name="experts",
    )(block_e, n_valid, dst3, x_slots, w_gu, b_gu.reshape(N_EXPERTS, 1, 2 * D_FF),
      w_down, b_down.reshape(N_EXPERTS, 1, D_MODEL))


def _combine_body(*refs, tt, n_first):
    y_refs = refs[:TOP_K]
    h_ref, gate_ref, g2_ref, b2_ref, out_first, out_second = refs[TOP_K:]
    i = pl.program_id(0)
    gates = gate_ref[...]
    f = gates[:, 0:1] * _load_rows(y_refs[0], tt)
    for kk in range(1, TOP_K):
        f = f + gates[:, kk:kk + 1] * _load_rows(y_refs[kk], tt)
    out = _layer_norm(DEEPNORM_ALPHA * _load_rows(h_ref, tt) + f, g2_ref[...], b2_ref[...])

    @pl.when(i < n_first)
    def _():
        out_first[...] = out

    @pl.when(i >= n_first)
    def _():
        out_second[...] = out


def _combine(y_rows, gates, h_all, ln2_g, ln2_b, tt, n_tokens_first):
    T = gates.shape[0]
    n_tiles = T // tt
    n_first = n_tokens_first // tt
    assert T % tt == 0 and n_tokens_first % tt == 0 and 0 < n_first < n_tiles
    row_blk = lambda imap: pl.BlockSpec((tt * SUBLANES, LANES), imap)
    y_specs = [row_blk(functools.partial(lambda i, kk: (kk * n_tiles + i, 0), kk=kk)) for kk in range(TOP_K)]
    return pl.pallas_call(
        functools.partial(_combine_body, tt=tt, n_first=n_first),
        grid=(n_tiles,),
        in_specs=y_specs + [
            row_blk(lambda i: (i, 0)),
            pl.BlockSpec((tt, TOP_K), lambda i: (i, 0)),
            pl.BlockSpec((1, D_MODEL), lambda i: (0, 0)),
            pl.BlockSpec((1, D_MODEL), lambda i: (0, 0)),
        ],
        out_specs=[pl.BlockSpec((tt, D_MODEL), lambda i: (jnp.minimum(i, n_first - 1), 0)),
                   pl.BlockSpec((tt, D_MODEL), lambda i: (jnp.maximum(i - n_first, 0), 0))],
        out_shape=[jax.ShapeDtypeStruct((n_tokens_first, D_MODEL), F32),
                   jax.ShapeDtypeStruct((T - n_tokens_first, D_MODEL), F32)],
        compiler_params=pltpu.CompilerParams(
            dimension_semantics=("arbitrary",), vmem_limit_bytes=VMEM_LIMIT_BYTES),
        name="combine",
    )(*([y_rows] * TOP_K), h_all, gates, ln2_g, ln2_b)


def kernel(x_prompt, x_sample, state_retention, state_conv, w_in, beta_ret, beta_conv, conv_w, conv_b,
           w_out, ln1_g, ln1_b, w_router, b_router, w_gu, b_gu, w_down, b_down, ln2_g, ln2_b):
    assert w_in.shape[0] == DEPTH == 1
    B, S, _ = x_prompt.shape
    Bd, Ld, _ = x_sample.shape
    Tp, Ts = B * S, Bd * Ld
    T = Tp + Ts
    wr_t = w_router[0].T
    wr_hi = wr_t.astype(BF16)
    wr_lo = (wr_t - wr_hi.astype(F32)).astype(BF16)
    w = dict(w_in=w_in[0].astype(BF16), w_out=w_out[0].astype(BF16), conv_w=conv_w[0],
             conv_b=conv_b[0][None], beta_ret=beta_ret[0][None], beta_conv=beta_conv[0][None],
             ln1_g=ln1_g[0][None], ln1_b=ln1_b[0][None], w_router=jnp.stack([wr_hi, wr_lo]),
             b_router=b_router[0][:, None])

    s_ret0 = jnp.zeros((B, N_RET_HEADS, HEAD_DIM, HEAD_DIM), F32)
    s_conv0 = jnp.zeros((B, CONV_W - 1, D_CONV), F32)
    h_s, choices_s, gates_s, ranks_s, counts_s, sret_s, sconv_s = _mixer(
        x_sample, state_retention[0], state_conv[0], PAST_LEN, SAMPLE_BATCH_BLOCK, Ld, w)
    h_all, choices_t, gates_t, ranks_t, counts_p, sret_p, sconv_p = _mixer(
        x_prompt, s_ret0, s_conv0, 0, 1, PROMPT_CHUNK, w, tail=(h_s, choices_s, gates_s, ranks_s))

    pos, pad_dst, block_e, n_valid, zero_blocks = _routing_plan(
        choices_t, ranks_t, counts_p[:, 0].astype(jnp.int32), counts_s[:, 0].astype(jnp.int32), Tp,
        EXPERT_BLOCK)
    x_slots, dst = _dispatch(h_all, pos, pad_dst, zero_blocks, DISPATCH_TILE, EXPERT_BLOCK)
    y_rows = _experts(x_slots, dst, block_e, n_valid, w_gu[0], b_gu[0], w_down[0], b_down[0],
                      EXPERT_BLOCK, T * TOP_K)
    y_p, y_s = _combine(y_rows, gates_t.T, h_all, ln2_g[0][None], ln2_b[0][None], COMBINE_TILE, Tp)

    return (y_p.reshape(B, S, D_MODEL), y_s.reshape(Bd, Ld, D_MODEL),
            sret_p[None], sconv_p[None], sret_s[None], sconv_s[None])
```
